```python
import math
import jax
import jax.numpy as jnp
from jax import lax

D_MODEL = 1024
BATCH = 2
SEQ = 8192
DEPTH = 2

GRID_W = 64
CTX_LEN = 256
N_MOD = 6

S5_WIDTH = 768
S5_GROUP = 16
S5_GROUPS = S5_WIDTH // S5_GROUP
S5_STATE = 64

LRU_WIDTH = 1024
LRU_BLOCKS = 16
LRU_BLOCK = LRU_WIDTH // LRU_BLOCKS
LRU_CONV = 4
LRU_C = 8.0

SSD_INNER = 1024
SSD_HEAD_DIM = 64
SSD_HEADS = SSD_INNER // SSD_HEAD_DIM
SSD_GROUPS = 4
SSD_HPG = SSD_HEADS // SSD_GROUPS
SSD_STATE = 128
SSD_CONV = 4
SSD_CHUNK = 128
SSD_XBC = SSD_INNER + 2 * SSD_GROUPS * SSD_STATE

N_BRANCH = 3
IN_SIZES = (S5_WIDTH, LRU_WIDTH, LRU_WIDTH, SSD_INNER, SSD_XBC, SSD_HEADS, N_BRANCH * D_MODEL)
IN_TOTAL = sum(IN_SIZES)
IN_SPLITS = tuple(sum(IN_SIZES[: i + 1]) for i in range(len(IN_SIZES) - 1))

D_FF = 2816
N_EXPERTS = 8
TOP_K = 2
D_FF_EXPERT = 3584
N_DENSE = (DEPTH + 1) // 2
N_MOE = DEPTH // 2
EPS = 1e-6

kernel_name = 'hybrid_s5_rglru_ssd_moe_dit'


def rmsnorm(x, w):
    xf = x.astype(jnp.float32)
    y = xf * lax.rsqrt(jnp.mean(xf * xf, axis=-1, keepdims=True) + EPS)
    return (y * w.astype(jnp.float32)).astype(x.dtype)


def modulate(h, shift, scale):
    return h * (1.0 + scale) + shift


def dwconv(x, w, b):
    k, ch = w.shape
    lo = k // 2
    y = lax.conv_general_dilated(x, w.astype(x.dtype)[:, None, :], window_strides=(1,),
                                 padding=((lo, k - 1 - lo),), dimension_numbers=('NWC', 'WIO', 'NWC'),
                                 feature_group_count=ch)
    return y + b.astype(x.dtype)


def to_col_major(x):
    b, l, ch = x.shape
    rows = l // GRID_W
    return x.reshape(b, rows, GRID_W, ch).transpose(0, 2, 1, 3).reshape(b, l, ch)


def to_row_major(x):
    b, l, ch = x.shape
    rows = l // GRID_W
    return x.reshape(b, GRID_W, rows, ch).transpose(0, 2, 1, 3).reshape(b, l, ch)


def linear_scan(a, b):
    def combine(left, right):
        a_l, b_l = left
        a_r, b_r = right
        return a_r * a_l, a_r * b_l + b_r
    return lax.associative_scan(combine, (a, b), axis=1)


def s5_direction(u_ctx, u_lat, lam_re, lam_im, log_dt, b_re, b_im, c_re, c_im, reverse, ctx_out):
    f32 = jnp.float32
    if reverse:
        u_ctx, u_lat = u_ctx[:, ::-1], u_lat[:, ::-1]
    lam = lax.complex(lam_re.astype(f32), lam_im.astype(f32))
    lam_dt = lam * jnp.exp(log_dt.astype(f32))[:, None]
    a_bar = jnp.exp(lam_dt)
    b_bar = ((a_bar - 1.0) / lam)[:, :, None] * lax.complex(b_re.astype(f32), b_im.astype(f32))
    c_mat = lax.complex(c_re.astype(f32), c_im.astype(f32))

    def run(u, h0):
        bu = jnp.einsum('blgh,gph->blgp', u.astype(jnp.complex64), b_bar)
        _, h = linear_scan(jnp.broadcast_to(a_bar, bu.shape), bu)
        if h0 is not None:
            steps = jnp.arange(1, u.shape[1] + 1, dtype=f32)[:, None, None]
            h = h + jnp.exp(steps * lam_dt)[None] * h0[:, None]
        return h

    def readout(h):
        return jnp.einsum('blgp,ghp->blgh', h, c_mat).real

    h_ctx = run(u_ctx, None)
    h_lat = run(u_lat, h_ctx[:, -1])
    y_lat = readout(h_lat)
    y_ctx = readout(h_ctx) if ctx_out else None
    if reverse:
        y_lat = y_lat[:, ::-1]
        y_ctx = y_ctx[:, ::-1] if ctx_out else None
    return y_lat, y_ctx


def s5_branch(u_ctx, u_lat, lam_re, lam_im, log_dt, b_re, b_im, c_re, c_im, d_skip, w_glu, ctx_out):
    def groups(u):
        return u.astype(jnp.float32).reshape(u.shape[0], u.shape[1], S5_GROUPS, S5_GROUP)
    g_ctx, g_lat = groups(u_ctx), groups(u_lat)
    dirs = [s5_direction(g_ctx, g_lat, lam_re[d], lam_im[d], log_dt[d], b_re[d], b_im[d], c_re[d], c_im[d],
                         d == 1, ctx_out) for d in range(2)]

    def finish(y_fwd, y_bwd, u):
        y = (y_fwd + y_bwd).reshape(u.shape) + d_skip * u.astype(jnp.float32)
        y = jax.nn.gelu(y).astype(u.dtype)
        return y * jax.nn.sigmoid(y @ w_glu)

    y_lat = finish(dirs[0][0], dirs[1][0], u_lat)
    y_ctx = finish(dirs[0][1], dirs[1][1], u_ctx) if ctx_out else None
    return y_lat, y_ctx


def blockdiag(x, w, b):
    xb = x.reshape(x.shape[0], x.shape[1], LRU_BLOCKS, LRU_BLOCK)
    return jnp.einsum('blnk,nkj->blnj', xb, w).reshape(x.shape) + b


def rglru_direction(x_ctx, x_lat, w_a, b_a, w_x, b_x, lam, reverse, ctx_out):
    if reverse:
        x_ctx, x_lat = x_ctx[:, ::-1], x_lat[:, ::-1]
    log_sig = jax.nn.log_sigmoid(lam.astype(jnp.float32))

    def coeffs(x):
        r = jax.nn.sigmoid(blockdiag(x, w_a, b_a))
        i = jax.nn.sigmoid(blockdiag(x, w_x, b_x))
        log_a = LRU_C * r * log_sig
        return jnp.exp(log_a), jnp.sqrt(-jnp.expm1(2.0 * log_a)) * (i * x)

    _, h_ctx = linear_scan(*coeffs(x_ctx))
    a_cum, h_lat = linear_scan(*coeffs(x_lat))
    h_lat = h_lat + a_cum * h_ctx[:, -1:]
    if reverse:
        h_lat = h_lat[:, ::-1]
        h_ctx = h_ctx[:, ::-1]
    return h_lat, (h_ctx if ctx_out else None)


def rglru_branch(x_ctx, x_lat, g_ctx, g_lat, conv_w, conv_b, w_a, b_a, w_x, b_x, lam, ctx_out):
    f32 = jnp.float32
    xc = dwconv(x_ctx.astype(f32), conv_w, conv_b)
    xl = dwconv(to_col_major(x_lat.astype(f32)), conv_w, conv_b)
    dirs = [rglru_direction(xc, xl, w_a[d], b_a[d], w_x[d], b_x[d], lam[d], d == 1, ctx_out) for d in range(2)]
    y_lat = to_row_major(dirs[0][0] + dirs[1][0]).astype(x_lat.dtype) * jax.nn.gelu(g_lat)
    y_ctx = ((dirs[0][1] + dirs[1][1]).astype(x_ctx.dtype) * jax.nn.gelu(g_ctx)) if ctx_out else None
    return y_lat, y_ctx


def segsum_exp(a):
    cs = jnp.cumsum(a, axis=-1)
    n = a.shape[-1]
    mask = jnp.tril(jnp.ones((n, n), dtype=bool))
    return jnp.exp(jnp.where(mask, cs[..., :, None] - cs[..., None, :], -jnp.inf))


def ssd_scan(x, dt, a, bm, cm, h0, with_y):
    bsz, n_tok = x.shape[0], x.shape[1]
    nc = n_tok // SSD_CHUNK
    shp = (bsz, nc, SSD_CHUNK, SSD_GROUPS, SSD_HPG)
    xc = (x * dt[..., None]).reshape(shp + (SSD_HEAD_DIM,))
    a_dt = (dt * a).reshape(shp).transpose(0, 3, 4, 1, 2)
    bc = bm.reshape(bsz, nc, SSD_CHUNK, SSD_GROUPS, SSD_STATE)
    cc = cm.reshape(bsz, nc, SSD_CHUNK, SSD_GROUPS, SSD_STATE)
    a_cs = jnp.cumsum(a_dt, axis=-1)
    decay_to_end = jnp.exp(a_cs[..., -1:] - a_cs).transpose(0, 3, 4, 1, 2)
    states = jnp.einsum('bcsgn,bcsgjp->bcgjpn', bc, xc * decay_to_end[..., None])
    chunk_decay = jnp.exp(a_cs[..., -1])

    def step(h, inp):
        st, dec = inp
        return dec[..., None, None] * h + st, h

    h_last, h_prev = lax.scan(step, h0, (jnp.moveaxis(states, 1, 0), jnp.moveaxis(chunk_decay, 3, 0)))
    if not with_y:
        return None, h_last
    h_prev = jnp.moveaxis(h_prev, 0, 1)
    scores = jnp.einsum('bclgn,bcsgn->bgcls', cc, bc)
    y_diag = jnp.einsum('bgjcls,bcsgjp->bclgjp', scores[:, :, None] * segsum_exp(a_dt), xc)
    y_off = jnp.einsum('bclgn,bcgjpn,bgjcl->bclgjp', cc, h_prev, jnp.exp(a_cs))
    return (y_diag + y_off).reshape(bsz, n_tok, SSD_HEADS, SSD_HEAD_DIM), h_last


def ssd_direction(ctx_in, lat_in, a_log, dt_bias, reverse, ctx_out):
    if reverse:
        ctx_in = tuple(t[:, ::-1] for t in ctx_in)
        lat_in = tuple(t[:, ::-1] for t in lat_in)
    a = -jnp.exp(a_log.astype(jnp.float32))

    def run(inp, h0, with_y):
        xs, bm, cm, dt_raw = inp
        dt = jax.nn.softplus(dt_raw + dt_bias.astype(jnp.float32))
        return ssd_scan(xs, dt, a, bm, cm, h0, with_y)

    bsz = lat_in[0].shape[0]
    h0 = jnp.zeros((bsz, SSD_GROUPS, SSD_HPG, SSD_HEAD_DIM, SSD_STATE), jnp.float32)
    y_ctx, h_ctx = run(ctx_in, h0, ctx_out)
    y_lat, _ = run(lat_in, h_ctx, True)
    if reverse:
        y_lat = y_lat[:, ::-1]
        y_ctx = y_ctx[:, ::-1] if ctx_out else None
    return y_lat, y_ctx


def ssd_branch(z_ctx, xbc_ctx, dt_ctx, z_lat, xbc_lat, dt_lat, conv_w, conv_b, a_log, dt_bias, d_skip, norm_w,
               ctx_out):
    f32 = jnp.float32

    def prep(xbc, dt_raw):
        xbc = jax.nn.silu(dwconv(xbc.astype(f32), conv_w, conv_b))
        xs, bm, cm = jnp.split(xbc, (SSD_INNER, SSD_INNER + SSD_GROUPS * SSD_STATE), axis=-1)
        b, l = xs.shape[0], xs.shape[1]
        return (xs.reshape(b, l, SSD_HEADS, SSD_HEAD_DIM), bm.reshape(b, l, SSD_GROUPS, SSD_STATE),
                cm.reshape(b, l, SSD_GROUPS, SSD_STATE), dt_raw.astype(f32))

    ctx_in, lat_in = prep(xbc_ctx, dt_ctx), prep(xbc_lat, dt_lat)
    dirs = [ssd_direction(ctx_in, lat_in, a_log[d], dt_bias[d], d == 1, ctx_out) for d in range(2)]

    def finish(y_fwd, y_bwd, inp, z):
        xs = inp[0]
        y = y_fwd + y_bwd + d_skip.astype(f32)[:, None] * xs
        y = y.reshape(xs.shape[0], xs.shape[1], SSD_INNER).astype(z.dtype)
        return rmsnorm(y * jax.nn.silu(z), norm_w)

    y_lat = finish(dirs[0][0], dirs[1][0], lat_in, z_lat)
    y_ctx = finish(dirs[0][1], dirs[1][1], ctx_in, z_ctx) if ctx_out else None
    return y_lat, y_ctx


def token_mixer(h_ctx, h_lat, w_in, s5_lam_re, s5_lam_im, s5_log_dt, s5_b_re, s5_b_im, s5_c_re, s5_c_im, s5_d,
                s5_w_glu, lru_conv_w, lru_conv_b, lru_w_a, lru_b_a, lru_w_x, lru_b_x, lru_lam, ssd_conv_w,
                ssd_conv_b, ssd_a_log, ssd_dt_bias, ssd_d, ssd_norm_w, w_br_a, w_br_b, w_br_c, w_out, ctx_out):
    pc = jnp.split(h_ctx @ w_in, IN_SPLITS, axis=-1)
    pl = jnp.split(h_lat @ w_in, IN_SPLITS, axis=-1)
    ya_lat, ya_ctx = s5_branch(pc[0], pl[0], s5_lam_re, s5_lam_im, s5_log_dt, s5_b_re, s5_b_im, s5_c_re,
                               s5_c_im, s5_d, s5_w_glu, ctx_out)
    yb_lat, yb_ctx = rglru_branch(pc[1], pl[1], pc[2], pl[2], lru_conv_w, lru_conv_b, lru_w_a, lru_b_a,
                                  lru_w_x, lru_b_x, lru_lam, ctx_out)
    yc_lat, yc_ctx = ssd_branch(pc[3], pc[4], pc[5], pl[3], pl[4], pl[5], ssd_conv_w, ssd_conv_b, ssd_a_log,
                                ssd_dt_bias, ssd_d, ssd_norm_w, ctx_out)

    def merge(ya, yb, yc, gates):
        ga, gb, gc = jnp.split(jax.nn.sigmoid(gates), N_BRANCH, axis=-1)
        return (ga * (ya @ w_br_a) + gb * (yb @ w_br_b) + gc * (yc @ w_br_c)) @ w_out

    y_lat = merge(ya_lat, yb_lat, yc_lat, pl[6])
    y_ctx = merge(ya_ctx, yb_ctx, yc_ctx, pc[6]) if ctx_out else None
    return y_lat, y_ctx


def swiglu(h, w1, w3, w2):
    return (jax.nn.silu(h @ w1) * (h @ w3)) @ w2


def moe_swiglu(h, w_router, b_router, w1, w3, w2):
    logits = (h @ w_router).astype(jnp.float32) + b_router.astype(jnp.float32)
    top_logit, top_idx = lax.top_k(logits, TOP_K)
    weights = jax.nn.softmax(top_logit, axis=-1)
    gate = jnp.sum(jax.nn.one_hot(top_idx, N_EXPERTS, dtype=jnp.float32) * weights[..., None], axis=-2)
    gate = gate.astype(h.dtype)
    out = jnp.zeros_like(h)
    for e in range(N_EXPERTS):
        out = out + gate[..., e:e + 1] * swiglu(h, w1[e], w3[e], w2[e])
    return out


def setup_inputs(seed: int = 0) -> dict:
    key = jax.random.key(seed)
    ks = iter(jax.random.split(key, 64))
    f32 = jnp.float32

    def nrm(shape, scale):
        return jax.random.normal(next(ks), shape, f32) * scale

    def unif(shape, lo, hi):
        return jax.random.uniform(next(ks), shape, f32, lo, hi)

    x = nrm((BATCH, SEQ, D_MODEL), 1.0)
    c = nrm((BATCH, D_MODEL), 1.0)
    ctx = nrm((BATCH, CTX_LEN, D_MODEL), 1.0)
    c_ctx = nrm((D_MODEL,), 1.0)
    w_mod = nrm((DEPTH, D_MODEL, N_MOD * D_MODEL), 0.5 * D_MODEL ** -0.5)
    b_mod = nrm((DEPTH, N_MOD * D_MODEL), 0.02)
    norm_w = 1.0 + nrm((DEPTH, 2, D_MODEL), 0.05)
    w_in = nrm((DEPTH, D_MODEL, IN_TOTAL), D_MODEL ** -0.5)
    s5_shape = (DEPTH, 2, S5_GROUPS, S5_STATE)
    s5_lam_re = -0.5 + nrm(s5_shape, 0.01)
    s5_lam_im = math.pi * jnp.arange(S5_STATE, dtype=f32) + nrm(s5_shape, 0.01)
    s5_log_dt = unif((DEPTH, 2, S5_GROUPS), math.log(1e-3), math.log(1e-1))
    s5_b_re = nrm((DEPTH, 2, S5_GROUPS, S5_STATE, S5_GROUP), (2 * S5_GROUP) ** -0.5)
    s5_b_im = nrm((DEPTH, 2, S5_GROUPS, S5_STATE, S5_GROUP), (2 * S5_GROUP) ** -0.5)
    s5_c_re = nrm((DEPTH, 2, S5_GROUPS, S5_GROUP, S5_STATE), 0.25)
    s5_c_im = nrm((DEPTH, 2, S5_GROUPS, S5_GROUP, S5_STATE), 0.25)
    s5_d = nrm((DEPTH, S5_WIDTH), 0.5)
    s5_w_glu = nrm((DEPTH, S5_WIDTH, S5_WIDTH), S5_WIDTH ** -0.5)
    lru_conv_w = nrm((DEPTH, LRU_CONV, LRU_WIDTH), LRU_CONV ** -0.5)
    lru_conv_b = nrm((DEPTH, LRU_WIDTH), 0.01)
    lru_w_a = nrm((DEPTH, 2, LRU_BLOCKS, LRU_BLOCK, LRU_BLOCK), LRU_BLOCK ** -0.5)
    lru_b_a = nrm((DEPTH, 2, LRU_WIDTH), 0.01)
    lru_w_x = nrm((DEPTH, 2, LRU_BLOCKS, LRU_BLOCK, LRU_BLOCK), LRU_BLOCK ** -0.5)
    lru_b_x = nrm((DEPTH, 2, LRU_WIDTH), 0.01)
    a0 = unif((DEPTH, 2, LRU_WIDTH), 0.9, 0.999)
    sig = a0 ** (1.0 / LRU_C)
    lru_lam = jnp.log(sig) - jnp.log1p(-sig)
    ssd_conv_w = nrm((DEPTH, SSD_CONV, SSD_XBC), SSD_CONV ** -0.5)
    ssd_conv_b = nrm((DEPTH, SSD_XBC), 0.01)
    ssd_a_log = jnp.log(unif((DEPTH, 2, SSD_HEADS), 1.0, 16.0))
    dt0 = jnp.exp(unif((DEPTH, 2, SSD_HEADS), math.log(1e-3), math.log(1e-1)))
    ssd_dt_bias = dt0 + jnp.log(-jnp.expm1(-dt0))
    ssd_d = 1.0 + nrm((DEPTH, SSD_HEADS), 0.1)
    ssd_norm_w = 1.0 + nrm((DEPTH, SSD_INNER), 0.05)
    w_br_a = nrm((DEPTH, S5_WIDTH, D_MODEL), S5_WIDTH ** -0.5)
    w_br_b = nrm((DEPTH, LRU_WIDTH, D_MODEL), LRU_WIDTH ** -0.5)
    w_br_c = nrm((DEPTH, SSD_INNER, D_MODEL), SSD_INNER ** -0.5)
    w_out = nrm((DEPTH, D_MODEL, D_MODEL), D_MODEL ** -0.5)
    ffn_w1 = nrm((N_DENSE, D_MODEL, D_FF), D_MODEL ** -0.5)
    ffn_w3 = nrm((N_DENSE, D_MODEL, D_FF), D_MODEL ** -0.5)
    ffn_w2 = nrm((N_DENSE, D_FF, D_MODEL), D_FF ** -0.5)
    moe_w_router = nrm((N_MOE, D_MODEL, N_EXPERTS), D_MODEL ** -0.5)
    moe_b_router = nrm((N_MOE, N_EXPERTS), 0.01)
    moe_w1 = nrm((N_MOE, N_EXPERTS, D_MODEL, D_FF_EXPERT), D_MODEL ** -0.5)
    moe_w3 = nrm((N_MOE, N_EXPERTS, D_MODEL, D_FF_EXPERT), D_MODEL ** -0.5)
    moe_w2 = nrm((N_MOE, N_EXPERTS, D_FF_EXPERT, D_MODEL), D_FF_EXPERT ** -0.5)
    final_norm_w = 1.0 + nrm((D_MODEL,), 0.05)
    return {'x': x, 'c': c, 'ctx': ctx, 'c_ctx': c_ctx, 'w_mod': w_mod, 'b_mod': b_mod, 'norm_w': norm_w,
            'w_in': w_in, 's5_lam_re': s5_lam_re, 's5_lam_im': s5_lam_im, 's5_log_dt': s5_log_dt,
            's5_b_re': s5_b_re, 's5_b_im': s5_b_im, 's5_c_re': s5_c_re, 's5_c_im': s5_c_im, 's5_d': s5_d,
            's5_w_glu': s5_w_glu, 'lru_conv_w': lru_conv_w, 'lru_conv_b': lru_conv_b, 'lru_w_a': lru_w_a,
            'lru_b_a': lru_b_a, 'lru_w_x': lru_w_x, 'lru_b_x': lru_b_x, 'lru_lam': lru_lam,
            'ssd_conv_w': ssd_conv_w, 'ssd_conv_b': ssd_conv_b, 'ssd_a_log': ssd_a_log, 'ssd_dt_bias': ssd_dt_bias,
            'ssd_d': ssd_d, 'ssd_norm_w': ssd_norm_w, 'w_br_a': w_br_a, 'w_br_b': w_br_b, 'w_br_c': w_br_c,
            'w_out': w_out, 'ffn_w1': ffn_w1, 'ffn_w3': ffn_w3, 'ffn_w2': ffn_w2, 'moe_w_router': moe_w_router,
            'moe_b_router': moe_b_router, 'moe_w1': moe_w1, 'moe_w3': moe_w3, 'moe_w2': moe_w2,
            'final_norm_w': final_norm_w}


def reference(x, c, ctx, c_ctx, w_mod, b_mod, norm_w, w_in, s5_lam_re, s5_lam_im, s5_log_dt, s5_b_re, s5_b_im,
              s5_c_re, s5_c_im, s5_d, s5_w_glu, lru_conv_w, lru_conv_b, lru_w_a, lru_b_a, lru_w_x, lru_b_x, lru_lam,
              ssd_conv_w, ssd_conv_b, ssd_a_log, ssd_dt_bias, ssd_d, ssd_norm_w, w_br_a, w_br_b, w_br_c, w_out,
              ffn_w1, ffn_w3, ffn_w2, moe_w_router, moe_b_router, moe_w1, moe_w3, moe_w2, final_norm_w):
    cond_lat = jax.nn.silu(c)
    cond_ctx = jax.nn.silu(c_ctx)
    x_lat, x_ctx = x, ctx
    n_ctx = ctx.shape[1]
    for l in range(DEPTH):
        ctx_out = l < DEPTH - 1
        m_lat = jnp.split((cond_lat @ w_mod[l] + b_mod[l])[:, None, :], N_MOD, axis=-1)
        m_ctx = jnp.split(cond_ctx @ w_mod[l] + b_mod[l], N_MOD, axis=-1)
        h_lat = modulate(rmsnorm(x_lat, norm_w[l, 0]), m_lat[0], m_lat[1])
        h_ctx = modulate(rmsnorm(x_ctx, norm_w[l, 0]), m_ctx[0], m_ctx[1])
        y_lat, y_ctx = token_mixer(
            h_ctx, h_lat, w_in[l], s5_lam_re[l], s5_lam_im[l], s5_log_dt[l], s5_b_re[l], s5_b_im[l], s5_c_re[l],
            s5_c_im[l], s5_d[l], s5_w_glu[l], lru_conv_w[l], lru_conv_b[l], lru_w_a[l], lru_b_a[l], lru_w_x[l],
            lru_b_x[l], lru_lam[l], ssd_conv_w[l], ssd_conv_b[l], ssd_a_log[l], ssd_dt_bias[l], ssd_d[l],
            ssd_norm_w[l], w_br_a[l], w_br_b[l], w_br_c[l], w_out[l], ctx_out)
        x_lat = x_lat + m_lat[2] * y_lat
        h_lat = modulate(rmsnorm(x_lat, norm_w[l, 1]), m_lat[3], m_lat[4])
        if ctx_out:
            x_ctx = x_ctx + m_ctx[2] * y_ctx
            h_ctx = modulate(rmsnorm(x_ctx, norm_w[l, 1]), m_ctx[3], m_ctx[4])
            h_all = jnp.concatenate([h_ctx, h_lat], axis=1)
        else:
            h_all = h_lat
        if l % 2 == 0:
            f = swiglu(h_all, ffn_w1[l // 2], ffn_w3[l // 2], ffn_w2[l // 2])
        else:
            f = moe_swiglu(h_all, moe_w_router[l // 2], moe_b_router[l // 2], moe_w1[l // 2], moe_w3[l // 2],
                           moe_w2[l // 2])
        if ctx_out:
            x_ctx = x_ctx + m_ctx[5] * f[:, :n_ctx]
            x_lat = x_lat + m_lat[5] * f[:, n_ctx:]
        else:
            x_lat = x_lat + m_lat[5] * f
    return rmsnorm(x_lat, final_norm_w)
```

```python
import functools
import math

import jax
import jax.numpy as jnp
from jax import lax
from jax.experimental import pallas as pl
from jax.experimental.pallas import tpu as pltpu

F32 = jnp.float32
BF16 = jnp.bfloat16
HIGHEST = lax.Precision.HIGHEST

D_MODEL = 1024
GRID_W = 64
N_MOD = 6
EPS = 1e-6

S5_WIDTH = 768
S5_GROUP = 16
S5_GROUPS = S5_WIDTH // S5_GROUP
S5_STATE = 64
S5_CHUNK = 32
S5_CW = S5_CHUNK * S5_GROUP

LRU_WIDTH = 1024
LRU_BLOCKS = 16
LRU_BLOCK = LRU_WIDTH // LRU_BLOCKS
LRU_CONV = 4
LRU_C = 8.0
LRU_SLAB = 128

SSD_INNER = 1024
SSD_HEAD_DIM = 64
SSD_HEADS = SSD_INNER // SSD_HEAD_DIM
SSD_GROUPS = 4
SSD_HPG = SSD_HEADS // SSD_GROUPS
SSD_STATE = 128
SSD_CONV = 4
SSD_CHUNK = 128
SSD_XBC = SSD_INNER + 2 * SSD_GROUPS * SSD_STATE
SSD_GW = SSD_HPG * SSD_HEAD_DIM

N_EXPERTS = 8
LANE = 128

P_U, P_LX, P_LG, P_Z, P_XBC, P_GATE = 0, 1024, 2048, 3072, 4096, 6144
P_TOTAL = 9216

VMEM_LIMIT = 56 * 1024 * 1024


def _cparams(sem):
    return pltpu.CompilerParams(dimension_semantics=sem, vmem_limit_bytes=VMEM_LIMIT)


def _tile(n, pref):
    t = min(n, pref)
    while n % t:
        t //= 2
    return t


def _sigmoid(x):
    return jax.nn.sigmoid(x)


def _silu(x):
    return x * jax.nn.sigmoid(x)


def _softplus(x):
    return jnp.maximum(x, 0.0) + jnp.log(1.0 + jnp.exp(-jnp.abs(x)))


def _rms_mod(x, nw, shift, scale):
    y = x * lax.rsqrt(jnp.mean(x * x, axis=-1, keepdims=True) + EPS) * nw
    return y * (1.0 + scale) + shift


def _split_dot(x, w_bf16):
    hi = x.astype(BF16)
    lo = (x - hi.astype(F32)).astype(BF16)
    return (jnp.dot(hi, w_bf16, preferred_element_type=F32)
            + jnp.dot(lo, w_bf16, preferred_element_type=F32))


def _mod_kernel(c_ref, w_ref, b_ref, o_ref):
    cond = _silu(c_ref[...])
    o_ref[...] = jnp.dot(cond, w_ref[...], preferred_element_type=F32, precision=HIGHEST) + b_ref[...]


def _modulation(cond_in, w_mod, b_mod):
    depth = w_mod.shape[0]
    n = w_mod.shape[2]
    tn = 1024
    return pl.pallas_call(
        _mod_kernel,
        grid=(depth, n // tn),
        in_specs=[pl.BlockSpec((8, D_MODEL), lambda l, j: (0, 0)),
                  pl.BlockSpec((None, D_MODEL, tn), lambda l, j: (l, 0, j)),
                  pl.BlockSpec((None, 1, tn), lambda l, j: (l, 0, j))],
        out_specs=pl.BlockSpec((None, 8, tn), lambda l, j: (l, 0, j)),
        out_shape=jax.ShapeDtypeStruct((depth, 8, n), F32),
        compiler_params=_cparams(("arbitrary", "arbitrary")),
        name="modulation",
    )(cond_in, w_mod, b_mod.reshape(depth, 1, n))


def _inproj_kernel(x_ref, mod_ref, nw_ref, w_ref, wdt_ref, p_ref, dt_ref, h_scr):
    @pl.when(pl.program_id(2) == 0)
    def _():
        h = _rms_mod(x_ref[...], nw_ref[...], mod_ref[0:1, :], mod_ref[1:2, :]).astype(BF16)
        h_scr[...] = h
        dt_ref[...] = jnp.dot(h, wdt_ref[...], preferred_element_type=F32)

    p_ref[...] = jnp.dot(h_scr[...], w_ref[...], preferred_element_type=F32).astype(p_ref.dtype)


def _inproj(x, mod, nw, w, wdt):
    b, t, _ = x.shape
    tm = _tile(t, 1024)
    tn = 1536
    return pl.pallas_call(
        _inproj_kernel,
        grid=(b, t // tm, P_TOTAL // tn),
        in_specs=[pl.BlockSpec((None, tm, D_MODEL), lambda b_, i, j: (b_, i, 0)),
                  pl.BlockSpec((None, 8, D_MODEL), lambda b_, i, j: (b_, 0, 0)),
                  pl.BlockSpec((1, D_MODEL), lambda b_, i, j: (0, 0)),
                  pl.BlockSpec((D_MODEL, tn), lambda b_, i, j: (0, j)),
                  pl.BlockSpec((D_MODEL, LANE), lambda b_, i, j: (0, 0))],
        out_specs=[pl.BlockSpec((None, tm, tn), lambda b_, i, j: (b_, i, j)),
                   pl.BlockSpec((None, tm, LANE), lambda b_, i, j: (b_, i, 0))],
        out_shape=[jax.ShapeDtypeStruct((b, t, P_TOTAL), BF16),
                   jax.ShapeDtypeStruct((b, t, LANE), F32)],
        scratch_shapes=[pltpu.VMEM((tm, D_MODEL), BF16)],
        compiler_params=_cparams(("arbitrary", "arbitrary", "arbitrary")),
        name="inproj",
    )(x, mod, nw, w, wdt)


def _s5_setup(lam_re, lam_im, log_dt, b_re, b_im, c_re, c_im, n_levels):
    t = S5_CHUNK
    dt = jnp.exp(log_dt.astype(F32))[..., None]
    lre, lim = lam_re.astype(F32), lam_im.astype(F32)
    are, aim = lre * dt, lim * dt
    k = jnp.arange(t + 1, dtype=F32)[:, None]
    mag = jnp.exp(are[:, :, None, :] * k)
    pw_re = mag * jnp.cos(aim[:, :, None, :] * k)
    pw_im = mag * jnp.sin(aim[:, :, None, :] * k)
    a_re, a_im = pw_re[:, :, 1], pw_im[:, :, 1]
    den = lre * lre + lim * lim
    q_re = ((a_re - 1.0) * lre + a_im * lim) / den
    q_im = (a_im * lre - (a_re - 1.0) * lim) / den
    bb_re = q_re[..., None] * b_re - q_im[..., None] * b_im
    bb_im = q_re[..., None] * b_im + q_im[..., None] * b_re
    cr, ci = c_re.astype(F32), c_im.astype(F32)

    cb_re = cr[..., None] * bb_re[:, :, None] - ci[..., None] * bb_im[:, :, None]
    cb_im = cr[..., None] * bb_im[:, :, None] + ci[..., None] * bb_re[:, :, None]
    kern = (jnp.einsum('dgkp,dgjpi->dgkji', pw_re[:, :, :t], cb_re, precision=HIGHEST)
            - jnp.einsum('dgkp,dgjpi->dgkji', pw_im[:, :, :t], cb_im, precision=HIGHEST))
    kf, kb = kern[0], kern[1]
    s_idx = jnp.arange(t)[:, None]
    t_idx = jnp.arange(t)[None, :]
    lag = t_idx - s_idx
    kf_g = kf[:, jnp.abs(lag)]
    kb_g = kb[:, jnp.abs(lag)]
    lag5 = lag[None, :, :, None, None]
    tz = jnp.where(lag5 > 0, kf_g, jnp.where(lag5 < 0, kb_g, kf_g + kb_g))
    tz = tz.transpose(0, 1, 4, 2, 3).reshape(S5_GROUPS, S5_CW, S5_CW)

    pf_re, pf_im = pw_re[0, :, :t][:, ::-1], pw_im[0, :, :t][:, ::-1]
    pb_re, pb_im = pw_re[1, :, :t], pw_im[1, :, :t]

    def in_map(p_re, p_im, d):
        w_re = p_re[:, :, None, :] * bb_re[d].transpose(0, 2, 1)[:, None] - p_im[:, :, None, :] * bb_im[d].transpose(0, 2, 1)[:, None]
        w_im = p_re[:, :, None, :] * bb_im[d].transpose(0, 2, 1)[:, None] + p_im[:, :, None, :] * bb_re[d].transpose(0, 2, 1)[:, None]
        return w_re, w_im

    wf_re, wf_im = in_map(pf_re, pf_im, 0)
    wb_re, wb_im = in_map(pb_re, pb_im, 1)
    pm = jnp.concatenate([wf_re, wf_im, wb_re, wb_im], axis=-1).reshape(S5_GROUPS, S5_CW, 4 * S5_STATE)

    def out_map(p_re, p_im, d):
        c_r = cr[d].transpose(0, 2, 1)[:, :, None, :]
        c_i = ci[d].transpose(0, 2, 1)[:, :, None, :]
        e_r = p_re.transpose(0, 2, 1)[..., None]
        e_i = p_im.transpose(0, 2, 1)[..., None]
        m_re = c_r * e_r - c_i * e_i
        m_im = c_r * e_i + c_i * e_r
        return m_re, -m_im

    rf_re, rf_im = out_map(pw_re[0, :, 1:t + 1], pw_im[0, :, 1:t + 1], 0)
    rb_re, rb_im = out_map(pw_re[1, :, 1:t + 1][:, ::-1], pw_im[1, :, 1:t + 1][:, ::-1], 1)
    rm = jnp.concatenate([rf_re, rf_im, rb_re, rb_im], axis=1).reshape(S5_GROUPS, 4 * S5_STATE, S5_CW)

    lev = (t * 2.0 ** jnp.arange(n_levels, dtype=F32))[:, None]
    lmag = jnp.exp(are[:, :, None, :] * lev)
    l_re = lmag * jnp.cos(aim[:, :, None, :] * lev)
    l_im = lmag * jnp.sin(aim[:, :, None, :] * lev)
    row_a = jnp.concatenate([l_re[0], l_re[0], l_re[1], l_re[1]], axis=-1)
    row_b = jnp.concatenate([-l_im[0], l_im[0], -l_im[1], l_im[1]], axis=-1)
    atab = jnp.stack([row_a, row_b], axis=2)
    return tz.astype(BF16), pm.astype(BF16), rm.astype(BF16), atab


def _shift_rows(h, d, down):
    n = h.shape[0]
    if d >= n:
        return jnp.zeros_like(h)
    row = lax.broadcasted_iota(jnp.int32, h.shape, 0)
    if down:
        return jnp.where(row >= d, pltpu.roll(h, d, 0), 0.0)
    return jnp.where(row < n - d, pltpu.roll(h, n - d, 0), 0.0)


def _s5_kernel(x_ref, tz_ref, pm_ref, rm_ref, at_ref, h0_ref, y_ref, ht_ref, hin_scr, *, nb, nc, n_levels):
    half = 2 * S5_STATE
    x = x_ref[...]
    s_all = jnp.dot(x, pm_ref[...], preferred_element_type=F32)
    row = lax.broadcasted_iota(jnp.int32, (nc, half), 0)

    def cmul(h, k, lo):
        a = at_ref[k, 0:1, lo:lo + half]
        b = at_ref[k, 1:2, lo:lo + half]
        return a * h + b * pltpu.roll(h, S5_STATE, 1)

    for b in range(nb):
        for d in range(2):
            lo = d * half
            h = s_all[b * nc:(b + 1) * nc, lo:lo + half]
            h0 = h0_ref[b:b + 1, lo:lo + half]
            edge = 0 if d == 0 else nc - 1
            h = h + jnp.where(row == edge, cmul(jnp.broadcast_to(h0, (nc, half)), 0, lo), 0.0)
            for k in range(n_levels):
                if (1 << k) < nc:
                    h = h + cmul(_shift_rows(h, 1 << k, d == 0), k, lo)
            ht_ref[b:b + 1, lo:lo + half] = h[nc - 1 - edge:nc - edge]
            hin = _shift_rows(h, 1, d == 0)
            hin = jnp.where(row == edge, jnp.broadcast_to(h0, (nc, half)), hin)
            hin_scr[b * nc:(b + 1) * nc, lo:lo + half] = hin

    y = jnp.dot(x, tz_ref[...], preferred_element_type=F32)
    y = y + jnp.dot(hin_scr[...].astype(BF16), rm_ref[...], preferred_element_type=F32)
    y_ref[...] = y.astype(y_ref.dtype)


def _s5_scan(xg, tz, pm, rm, atab, h0, nb):
    g, rows, _ = xg.shape
    nc = rows // nb
    n_levels = atab.shape[1]
    kern = functools.partial(_s5_kernel, nb=nb, nc=nc, n_levels=n_levels)
    return pl.pallas_call(
        kern,
        grid=(g,),
        in_specs=[pl.BlockSpec((None, rows, S5_CW), lambda i: (i, 0, 0)),
                  pl.BlockSpec((None, S5_CW, S5_CW), lambda i: (i, 0, 0)),
                  pl.BlockSpec((None, S5_CW, 4 * S5_STATE), lambda i: (i, 0, 0)),
                  pl.BlockSpec((None, 4 * S5_STATE, S5_CW), lambda i: (i, 0, 0)),
                  pl.BlockSpec((None, n_levels, 2, 4 * S5_STATE), lambda i: (i, 0, 0, 0)),
                  pl.BlockSpec((None, nb, 4 * S5_STATE), lambda i: (i, 0, 0))],
        out_specs=[pl.BlockSpec((None, rows, S5_CW), lambda i: (i, 0, 0)),
                   pl.BlockSpec((None, nb, 4 * S5_STATE), lambda i: (i, 0, 0))],
        out_shape=[jax.ShapeDtypeStruct((g, rows, S5_CW), BF16),
                   jax.ShapeDtypeStruct((g, nb, 4 * S5_STATE), F32)],
        scratch_shapes=[pltpu.VMEM((rows, 4 * S5_STATE), F32)],
        compiler_params=_cparams(("arbitrary",)),
        name="s5_scan",
    )(xg, tz, pm, rm, atab, h0)


def _s5_chunks(p, nb, t):
    nc = t // S5_CHUNK
    u = p[:, :, P_U:P_U + S5_WIDTH].reshape(nb, nc, S5_CHUNK, S5_GROUPS, S5_GROUP)
    return u.transpose(3, 0, 1, 2, 4).reshape(S5_GROUPS, nb * nc, S5_CW)


def _s5_unchunk(y, nb, t):
    nc = t // S5_CHUNK
    y = y.reshape(S5_GROUPS, nb, nc, S5_CHUNK, S5_GROUP)
    return y.transpose(1, 2, 3, 0, 4).reshape(nb, t, S5_WIDTH)


def _lru_kernel(x_ref, cw_ref, cb_ref, wg_ref, bg_ref, lam_ref, h0_ref, out_ref, ht_ref,
                xp_scr, xc_scr, af_scr, bf_scr, ab_scr, bb_scr, cf_scr, cr_scr, *, w, s, pad_top, ch):
    l = w * s
    n_ch = l // ch
    zero_slab = jnp.zeros((w, LRU_SLAB), F32)

    xp_scr[0:pad_top, :] = jnp.zeros((pad_top, LRU_SLAB), F32)
    xp_scr[pad_top + l:pad_top + l + pad_top, :] = jnp.zeros((pad_top, LRU_SLAB), F32)

    def copy_body(i, _):
        r0 = pl.multiple_of(i * ch, ch)
        xp_scr[pl.ds(pad_top + r0, ch), :] = x_ref[pl.ds(r0, ch), :].astype(F32)
        return 0
    lax.fori_loop(0, n_ch, copy_body, 0)

    def conv_body(i, _):
        r0 = pl.multiple_of(i * ch, ch)
        acc = cb_ref[...] + cw_ref[2:3, :] * xp_scr[pl.ds(pad_top + r0, ch), :]
        acc = acc + cw_ref[0:1, :] * xp_scr[pl.ds(pad_top + r0 - 2 * w, ch), :]
        acc = acc + cw_ref[1:2, :] * xp_scr[pl.ds(pad_top + r0 - w, ch), :]
        acc = acc + cw_ref[3:4, :] * xp_scr[pl.ds(pad_top + r0 + w, ch), :]
        xc_scr[pl.ds(r0, ch), :] = acc
        return 0
    lax.fori_loop(0, n_ch, conv_body, 0)

    if w > 1:
        def prev_col(v):
            return _shift_rows(v, 1, True)

        def next_col(v):
            return _shift_rows(v, 1, False)

        x_last = xp_scr[pad_top + (s - 1) * w:pad_top + s * w, :]
        x_last2 = xp_scr[pad_top + (s - 2) * w:pad_top + (s - 1) * w, :]
        x_first = xp_scr[pad_top:pad_top + w, :]
        xc_scr[0:w, :] = (xc_scr[0:w, :] + cw_ref[0:1, :] * prev_col(x_last2)
                          + cw_ref[1:2, :] * prev_col(x_last))
        xc_scr[w:2 * w, :] = xc_scr[w:2 * w, :] + cw_ref[0:1, :] * prev_col(x_last)
        xc_scr[(s - 1) * w:s * w, :] = xc_scr[(s - 1) * w:s * w, :] + cw_ref[3:4, :] * next_col(x_first)

    lam = lam_ref[...]
    log_sig = -_softplus(-lam)

    def coef_body(i, _):
        r0 = pl.multiple_of(i * ch, ch)
        xc = xc_scr[pl.ds(r0, ch), :]
        g = jnp.dot(xc.astype(BF16), wg_ref[...], preferred_element_type=F32) + bg_ref[...]
        for d, (a_scr, b_scr) in enumerate(((af_scr, bf_scr), (ab_scr, bb_scr))):
            rg = _sigmoid(g[:, (2 * d) * LRU_SLAB:(2 * d + 1) * LRU_SLAB])
            ig = _sigmoid(g[:, (2 * d + 1) * LRU_SLAB:(2 * d + 2) * LRU_SLAB])
            log_a = LRU_C * rg * log_sig[d:d + 1, :]
            a_scr[pl.ds(r0, ch), :] = jnp.exp(log_a)
            b_scr[pl.ds(r0, ch), :] = jnp.sqrt(1.0 - jnp.exp(2.0 * log_a)) * (ig * xc)
        return 0
    lax.fori_loop(0, n_ch, coef_body, 0)

    def scan_body(r, carry):
        hf, pf, hb, pb = carry
        rf = pl.multiple_of(r * w, w)
        rb = pl.multiple_of((s - 1 - r) * w, w)
        a = af_scr[pl.ds(rf, w), :]
        hf = a * hf + bf_scr[pl.ds(rf, w), :]
        pf = a * pf
        bf_scr[pl.ds(rf, w), :] = hf
        af_scr[pl.ds(rf, w), :] = pf
        a = ab_scr[pl.ds(rb, w), :]
        hb = a * hb + bb_scr[pl.ds(rb, w), :]
        pb = a * pb
        bb_scr[pl.ds(rb, w), :] = hb
        ab_scr[pl.ds(rb, w), :] = pb
        return hf, pf, hb, pb
    one_slab = jnp.ones((w, LRU_SLAB), F32)
    lax.fori_loop(0, s, scan_body, (zero_slab, one_slab, zero_slab, one_slab))

    def carry_f(c, carry):
        cf_scr[pl.ds(c, 1), :] = carry
        return bf_scr[pl.ds((s - 1) * w + c, 1), :] + af_scr[pl.ds((s - 1) * w + c, 1), :] * carry
    ht_ref[0:1, :] = lax.fori_loop(0, w, carry_f, h0_ref[0:1, :])

    def carry_b(i, carry):
        c = w - 1 - i
        cr_scr[pl.ds(c, 1), :] = carry
        return bb_scr[pl.ds(c, 1), :] + ab_scr[pl.ds(c, 1), :] * carry
    ht_ref[1:2, :] = lax.fori_loop(0, w, carry_b, h0_ref[1:2, :])

    def out_body(r, _):
        r0 = pl.multiple_of(r * w, w)
        y = (bf_scr[pl.ds(r0, w), :] + af_scr[pl.ds(r0, w), :] * cf_scr[...]
             + bb_scr[pl.ds(r0, w), :] + ab_scr[pl.ds(r0, w), :] * cr_scr[...])
        bf_scr[pl.ds(r0, w), :] = y
        return 0
    lax.fori_loop(0, s, out_body, 0)

    def store_body(i, _):
        r0 = pl.multiple_of(i * ch, ch)
        out_ref[pl.ds(r0, ch), :] = bf_scr[pl.ds(r0, ch), :].astype(out_ref.dtype)
        return 0
    lax.fori_loop(0, n_ch, store_body, 0)


def _lru_scan(p, cw, cb, wg, bg, lam, h0, w):
    b, t, _ = p.shape
    s = t // w
    assert s * w == t and s >= 4
    ch = _tile(t, 256)
    pad_top = max(2 * w, 8)
    n_slab = LRU_WIDTH // LRU_SLAB
    col0 = P_LX // LRU_SLAB
    kern = functools.partial(_lru_kernel, w=w, s=s, pad_top=pad_top, ch=ch)
    return pl.pallas_call(
        kern,
        grid=(b, n_slab),
        in_specs=[pl.BlockSpec((None, t, LRU_SLAB), lambda b_, k: (b_, 0, col0 + k)),
                  pl.BlockSpec((LRU_CONV, LRU_SLAB), lambda b_, k: (0, k)),
                  pl.BlockSpec((1, LRU_SLAB), lambda b_, k: (0, k)),
                  pl.BlockSpec((None, LRU_SLAB, 4 * LRU_SLAB), lambda b_, k: (k, 0, 0)),
                  pl.BlockSpec((None, 1, 4 * LRU_SLAB), lambda b_, k: (k, 0, 0)),
                  pl.BlockSpec((2, LRU_SLAB), lambda b_, k: (0, k)),
                  pl.BlockSpec((None, 2, LRU_SLAB), lambda b_, k: (b_, 0, k))],
        out_specs=[pl.BlockSpec((None, t, LRU_SLAB), lambda b_, k: (b_, 0, k)),
                   pl.BlockSpec((None, 2, LRU_SLAB), lambda b_, k: (b_, 0, k))],
        out_shape=[jax.ShapeDtypeStruct((b, t, LRU_WIDTH), BF16),
                   jax.ShapeDtypeStruct((b, 2, LRU_WIDTH), F32)],
        scratch_shapes=[pltpu.VMEM((t + 2 * pad_top, LRU_SLAB), F32),
                        pltpu.VMEM((t, LRU_SLAB), F32),
                        pltpu.VMEM((t, LRU_SLAB), F32), pltpu.VMEM((t, LRU_SLAB), F32),
                        pltpu.VMEM((t, LRU_SLAB), F32), pltpu.VMEM((t, LRU_SLAB), F32),
                        pltpu.VMEM((w, LRU_SLAB), F32), pltpu.VMEM((w, LRU_SLAB), F32)],
        compiler_params=_cparams(("arbitrary", "arbitrary")),
        name="lru_scan",
    )(p, cw, cb, wg, bg, lam, h0)


def _lru_gate_weights(w_a, b_a, w_x, b_x):
    n_slab = LRU_WIDTH // LRU_SLAB
    per = LRU_SLAB // LRU_BLOCK

    def slab_diag(wm):
        wm = wm.reshape(n_slab, per, LRU_BLOCK, LRU_BLOCK)
        eye = jnp.eye(per, dtype=wm.dtype)
        return jnp.einsum('spkj,pq->spkqj', wm, eye).reshape(n_slab, LRU_SLAB, LRU_SLAB)

    wg = jnp.concatenate([slab_diag(w_a[0]), slab_diag(w_x[0]), slab_diag(w_a[1]), slab_diag(w_x[1])], axis=2)
    bg = jnp.concatenate([b_a[0].reshape(n_slab, 1, LRU_SLAB), b_x[0].reshape(n_slab, 1, LRU_SLAB),
                          b_a[1].reshape(n_slab, 1, LRU_SLAB), b_x[1].reshape(n_slab, 1, LRU_SLAB)], axis=2)
    return wg.astype(BF16), bg.astype(F32)


def _ssd_core(xs, bm, cm, dtr, alog, dbias, e_ref, h_scr, y_scr, reverse):
    q = SSD_CHUNK
    a = -jnp.exp(alog)
    dt = _softplus(dtr + dbias)
    adt = dt * a
    row = lax.broadcasted_iota(jnp.int32, (q, q), 0)
    col = lax.broadcasted_iota(jnp.int32, (q, q), 1)
    tri = (col >= row) if reverse else (col <= row)
    cs = jnp.dot(tri.astype(F32), adt, preferred_element_type=F32, precision=HIGHEST)
    cs_t = cs.T
    end = 0 if reverse else q - 1
    tot = cs[end:end + 1, :]
    e = e_ref[...]
    dt_x = _split_dot(dt, e)
    w_x = _split_dot(dt * jnp.exp(tot - cs), e)
    ecs_x = _split_dot(jnp.exp(cs), e)
    xdt = (xs * dt_x).astype(BF16)
    xw = (xs * w_x).astype(BF16)
    decay_row = ecs_x[end:end + 1, :]

    for g in range(SSD_GROUPS):
        cg = cm[:, g * SSD_STATE:(g + 1) * SSD_STATE]
        bg = bm[:, g * SSD_STATE:(g + 1) * SSD_STATE]
        sc = lax.dot_general(cg, bg, (((1,), (1,)), ((), ())), preferred_element_type=F32)
        for j in range(SSD_HPG):
            hd = g * SSD_HPG + j
            diff = cs[:, hd:hd + 1] - cs_t[hd:hd + 1, :]
            lm = jnp.where(tri, jnp.exp(diff), 0.0)
            m = (sc * lm).astype(BF16)
            y_scr[:, hd * SSD_HEAD_DIM:(hd + 1) * SSD_HEAD_DIM] = jnp.dot(
                m, xdt[:, hd * SSD_HEAD_DIM:(hd + 1) * SSD_HEAD_DIM], preferred_element_type=F32)
        ch = slice(g * SSD_GW, (g + 1) * SSD_GW)
        rows = slice(g * SSD_STATE, (g + 1) * SSD_STATE)
        h_old = h_scr[rows, :]
        y_off = jnp.dot(cg, h_old.astype(BF16), preferred_element_type=F32) * ecs_x[:, ch]
        y_scr[:, ch] = y_scr[:, ch] + y_off
        st = lax.dot_general(bg, xw[:, ch], (((0,), (0,)), ((), ())), preferred_element_type=F32)
        h_scr[rows, :] = decay_row[:, ch] * h_old + st
    return y_scr[...]


def _ssd_fwd_kernel(x_ref, xprev_ref, xnext_ref, dt_ref, cw_ref, cb_ref, alog_ref, dbias_ref, e_ref, h0_ref,
                    y_ref, xc_ref, ht_ref, h_scr, y_scr, *, nc):
    c = pl.program_id(1)

    @pl.when(c == 0)
    def _():
        h_scr[...] = h0_ref[...]

    q = SSD_CHUNK
    x = x_ref[...].astype(F32)
    prev = jnp.where(c > 0, xprev_ref[...].astype(F32), 0.0)
    nxt = jnp.where(c < nc - 1, xnext_ref[...].astype(F32), 0.0)
    row8 = lax.broadcasted_iota(jnp.int32, (8, SSD_XBC), 0)

    def behind(k):
        xr = pltpu.roll(x, k, 0)
        top = jnp.where(row8 < k, pltpu.roll(prev, k, 0)[0:8], xr[0:8])
        return jnp.concatenate([top, xr[8:]], axis=0)

    xm2, xm1 = behind(2), behind(1)
    xr = pltpu.roll(x, q - 1, 0)
    bottom = jnp.where(row8 == 7, pltpu.roll(nxt, 15, 0)[8:16], xr[q - 8:q])
    xp1 = jnp.concatenate([xr[:q - 8], bottom], axis=0)
    conv = (cb_ref[...] + cw_ref[0:1, :] * xm2 + cw_ref[1:2, :] * xm1
            + cw_ref[2:3, :] * x + cw_ref[3:4, :] * xp1)
    xc = _silu(conv)
    xc_ref[...] = xc.astype(xc_ref.dtype)
    xs = xc[:, :SSD_INNER]
    bm = xc[:, SSD_INNER:SSD_INNER + SSD_GROUPS * SSD_STATE].astype(BF16)
    cm = xc[:, SSD_INNER + SSD_GROUPS * SSD_STATE:].astype(BF16)
    y_ref[...] = _ssd_core(xs, bm, cm, dt_ref[...], alog_ref[...], dbias_ref[...], e_ref, h_scr, y_scr, False)

    @pl.when(c == nc - 1)
    def _():
        ht_ref[...] = h_scr[...]


def _ssd_bwd_kernel(xc_ref, dt_ref, yf_ref, alog_ref, dbias_ref, dskip_ref, e_ref, h0_ref,
                    y_ref, ht_ref, h_scr, y_scr, *, nc):
    c = pl.program_id(1)

    @pl.when(c == 0)
    def _():
        h_scr[...] = h0_ref[...]

    xc = xc_ref[...]
    xs = xc[:, :SSD_INNER].astype(F32)
    bm = xc[:, SSD_INNER:SSD_INNER + SSD_GROUPS * SSD_STATE]
    cm = xc[:, SSD_INNER + SSD_GROUPS * SSD_STATE:]
    y = _ssd_core(xs, bm, cm, dt_ref[...], alog_ref[...], dbias_ref[...], e_ref, h_scr, y_scr, True)
    y_ref[...] = y + yf_ref[...] + dskip_ref[...] * xs

    @pl.when(c == nc - 1)
    def _():
        ht_ref[...] = h_scr[...]


def _ssd_scan(p, dt, cw, cb, a_log, dt_bias, dskip_x, e_mat, h0f, h0b):
    b, t, _ = p.shape
    q = SSD_CHUNK
    nc = t // q
    assert nc * q == t
    xbc_blk = P_XBC // SSD_XBC
    n16 = t // 16
    hs = (SSD_GROUPS * SSD_STATE, SSD_GW)
    const2 = lambda b_, c: (0, 0)
    state_spec = pl.BlockSpec((None,) + hs, lambda b_, c: (b_, 0, 0))

    yf, xc, htf = pl.pallas_call(
        functools.partial(_ssd_fwd_kernel, nc=nc),
        grid=(b, nc),
        in_specs=[pl.BlockSpec((None, q, SSD_XBC), lambda b_, c: (b_, c, xbc_blk)),
                  pl.BlockSpec((None, 16, SSD_XBC),
                               lambda b_, c: (b_, jnp.maximum(c * (q // 16) - 1, 0), xbc_blk)),
                  pl.BlockSpec((None, 16, SSD_XBC),
                               lambda b_, c: (b_, jnp.minimum((c + 1) * (q // 16), n16 - 1), xbc_blk)),
                  pl.BlockSpec((None, q, LANE), lambda b_, c: (b_, c, 0)),
                  pl.BlockSpec((SSD_CONV, SSD_XBC), const2),
                  pl.BlockSpec((1, SSD_XBC), const2),
                  pl.BlockSpec((1, LANE), const2),
                  pl.BlockSpec((1, LANE), const2),
                  pl.BlockSpec((LANE, SSD_INNER), const2),
                  state_spec],
        out_specs=[pl.BlockSpec((None, q, SSD_INNER), lambda b_, c: (b_, c, 0)),
                   pl.BlockSpec((None, q, SSD_XBC), lambda b_, c: (b_, c, 0)),
                   state_spec],
        out_shape=[jax.ShapeDtypeStruct((b, t, SSD_INNER), F32),
                   jax.ShapeDtypeStruct((b, t, SSD_XBC), BF16),
                   jax.ShapeDtypeStruct((b,) + hs, F32)],
        scratch_shapes=[pltpu.VMEM(hs, F32), pltpu.VMEM((q, SSD_INNER), F32)],
        compiler_params=_cparams(("arbitrary", "arbitrary")),
        name="ssd_fwd",
    )(p, p, p, dt, cw, cb, a_log[0:1], dt_bias[0:1], e_mat, h0f)

    rev = lambda b_, c: (b_, nc - 1 - c, 0)
    y, htb = pl.pallas_call(
        functools.partial(_ssd_bwd_kernel, nc=nc),
        grid=(b, nc),
        in_specs=[pl.BlockSpec((None, q, SSD_XBC), rev),
                  pl.BlockSpec((None, q, LANE), rev),
                  pl.BlockSpec((None, q, SSD_INNER), rev),
                  pl.BlockSpec((1, LANE), const2),
                  pl.BlockSpec((1, LANE), const2),
                  pl.BlockSpec((1, SSD_INNER), const2),
                  pl.BlockSpec((LANE, SSD_INNER), const2),
                  state_spec],
        out_specs=[pl.BlockSpec((None, q, SSD_INNER), rev), state_spec],
        out_shape=[jax.ShapeDtypeStruct((b, t, SSD_INNER), F32),
                   jax.ShapeDtypeStruct((b,) + hs, F32)],
        scratch_shapes=[pltpu.VMEM(hs, F32), pltpu.VMEM((q, SSD_INNER), F32)],
        compiler_params=_cparams(("arbitrary", "arbitrary")),
        name="ssd_bwd",
    )(xc, dt, yf, a_log[1:2], dt_bias[1:2], dskip_x, e_mat, h0b)
    return y, htf, htb


def _merge_kernel(x_ref, mod_ref, u_ref, ya_ref, yb_ref, g_ref, yc_ref, z_ref, gates_ref,
                  d5_ref, wglu_ref, snw_ref, wa_ref, wb_ref, wc_ref, wo_ref, xo_ref):
    ya = jax.nn.gelu(ya_ref[...].astype(F32) + d5_ref[...] * u_ref[...].astype(F32))
    ya = ya * _sigmoid(jnp.dot(ya.astype(BF16), wglu_ref[...], preferred_element_type=F32))
    yb = yb_ref[...].astype(F32) * jax.nn.gelu(g_ref[...].astype(F32))
    yc = yc_ref[...] * _silu(z_ref[...].astype(F32))
    yc = yc * lax.rsqrt(jnp.mean(yc * yc, axis=-1, keepdims=True) + EPS) * snw_ref[...]
    mix = _sigmoid(gates_ref[:, 0:D_MODEL].astype(F32)) * jnp.dot(
        ya.astype(BF16), wa_ref[...], preferred_element_type=F32)
    mix = mix + _sigmoid(gates_ref[:, D_MODEL:2 * D_MODEL].astype(F32)) * jnp.dot(
        yb.astype(BF16), wb_ref[...], preferred_element_type=F32)
    mix = mix + _sigmoid(gates_ref[:, 2 * D_MODEL:3 * D_MODEL].astype(F32)) * jnp.dot(
        yc.astype(BF16), wc_ref[...], preferred_element_type=F32)
    y = jnp.dot(mix.astype(BF16), wo_ref[...], preferred_element_type=F32)
    xo_ref[...] = x_ref[...] + mod_ref[2:3, :] * y


def _merge(x, mod, p, ya, yb, yc, d5, wglu, snw, wa, wb, wc, wo):
    b, t, _ = x.shape
    tm = _tile(t, 512)
    tok = lambda width, blk: pl.BlockSpec((None, tm, width), lambda b_, i: (b_, i, blk))
    full = lambda r, c_: pl.BlockSpec((r, c_), lambda b_, i: (0, 0))
    return pl.pallas_call(
        _merge_kernel,
        grid=(b, t // tm),
        in_specs=[tok(D_MODEL, 0),
                  pl.BlockSpec((None, 8, D_MODEL), lambda b_, i: (b_, 0, 0)),
                  tok(S5_WIDTH, P_U // S5_WIDTH),
                  tok(S5_WIDTH, 0),
                  tok(LRU_WIDTH, 0),
                  tok(LRU_WIDTH, P_LG // LRU_WIDTH),
                  tok(SSD_INNER, 0),
                  tok(SSD_INNER, P_Z // SSD_INNER),
                  tok(3 * D_MODEL, P_GATE // (3 * D_MODEL)),
                  full(1, S5_WIDTH), full(S5_WIDTH, S5_WIDTH), full(1, SSD_INNER),
                  full(S5_WIDTH, D_MODEL), full(LRU_WIDTH, D_MODEL), full(SSD_INNER, D_MODEL),
                  full(D_MODEL, D_MODEL)],
        out_specs=tok(D_MODEL, 0),
        out_shape=jax.ShapeDtypeStruct((b, t, D_MODEL), F32),
        compiler_params=_cparams(("arbitrary", "arbitrary")),
        name="merge",
    )(x, mod, p, ya, yb, p, yc, p, p, d5, wglu, snw, wa, wb, wc, wo)


def _ffn_kernel(x_ref, mod_ref, nw_ref, w1_ref, w3_ref, w2_ref, fnw_ref, xo_ref, h_scr, acc_scr, *, nf, final_norm):
    f = pl.program_id(2)

    @pl.when(f == 0)
    def _():
        h_scr[...] = _rms_mod(x_ref[...], nw_ref[...], mod_ref[3:4, :], mod_ref[4:5, :]).astype(BF16)
        acc_scr[...] = jnp.zeros_like(acc_scr)

    h = h_scr[...]
    a = jnp.dot(h, w1_ref[...], preferred_element_type=F32)
    g = jnp.dot(h, w3_ref[...], preferred_element_type=F32)
    acc_scr[...] += jnp.dot((_silu(a) * g).astype(BF16), w2_ref[...], preferred_element_type=F32)

    @pl.when(f == nf - 1)
    def _():
        xo = x_ref[...] + mod_ref[5:6, :] * acc_scr[...]
        if final_norm:
            xo = xo * lax.rsqrt(jnp.mean(xo * xo, axis=-1, keepdims=True) + EPS) * fnw_ref[...]
        xo_ref[...] = xo


def _ffn(x, mod, nw, w1, w3, w2, fnw, final_norm):
    b, t, _ = x.shape
    dff = w1.shape[1]
    tm = _tile(t, 512)
    tf = dff // 2
    nf = dff // tf
    return pl.pallas_call(
        functools.partial(_ffn_kernel, nf=nf, final_norm=final_norm),
        grid=(b, t // tm, nf),
        in_specs=[pl.BlockSpec((None, tm, D_MODEL), lambda b_, i, f: (b_, i, 0)),
                  pl.BlockSpec((None, 8, D_MODEL), lambda b_, i, f: (b_, 0, 0)),
                  pl.BlockSpec((1, D_MODEL), lambda b_, i, f: (0, 0)),
                  pl.BlockSpec((D_MODEL, tf), lambda b_, i, f: (0, f)),
                  pl.BlockSpec((D_MODEL, tf), lambda b_, i, f: (0, f)),
                  pl.BlockSpec((tf, D_MODEL), lambda b_, i, f: (f, 0)),
                  pl.BlockSpec((1, D_MODEL), lambda b_, i, f: (0, 0))],
        out_specs=pl.BlockSpec((None, tm, D_MODEL), lambda b_, i, f: (b_, i, 0)),
        out_shape=jax.ShapeDtypeStruct((b, t, D_MODEL), F32),
        scratch_shapes=[pltpu.VMEM((tm, D_MODEL), BF16), pltpu.VMEM((tm, D_MODEL), F32)],
        compiler_params=_cparams(("arbitrary", "arbitrary", "arbitrary")),
        name="ffn",
    )(x, mod, nw, w1, w3, w2, fnw)


def _moe_kernel(x_ref, mod_ref, nw_ref, wr_ref, br_ref, w1_ref, w3_ref, w2_ref, fnw_ref, xo_ref,
                h_scr, gate_scr, acc_scr, *, nf, final_norm):
    e = pl.program_id(2)
    f = pl.program_id(3)

    @pl.when((e == 0) & (f == 0))
    def _():
        h = _rms_mod(x_ref[...], nw_ref[...], mod_ref[3:4, :], mod_ref[4:5, :])
        h_scr[...] = h.astype(BF16)
        acc_scr[...] = jnp.zeros_like(acc_scr)
        logits = jnp.dot(h, wr_ref[...], preferred_element_type=F32, precision=HIGHEST) + br_ref[...]
        lane = lax.broadcasted_iota(jnp.int32, logits.shape, 1)
        logits = jnp.where(lane < N_EXPERTS, logits, -jnp.inf)
        t1 = jnp.max(logits, axis=-1, keepdims=True)
        i1 = jnp.min(jnp.where(logits == t1, lane, LANE), axis=-1, keepdims=True)
        rest = jnp.where(lane == i1, -jnp.inf, logits)
        t2 = jnp.max(rest, axis=-1, keepdims=True)
        i2 = jnp.min(jnp.where(rest == t2, lane, LANE), axis=-1, keepdims=True)
        e2 = jnp.exp(t2 - t1)
        den = 1.0 + e2
        gate_scr[...] = jnp.where(lane == i1, 1.0 / den, 0.0) + jnp.where(lane == i2, e2 / den, 0.0)

    h = h_scr[...]
    a = jnp.dot(h, w1_ref[...], preferred_element_type=F32)
    g = jnp.dot(h, w3_ref[...], preferred_element_type=F32)
    y = jnp.dot((_silu(a) * g).astype(BF16), w2_ref[...], preferred_element_type=F32)
    lane = lax.broadcasted_iota(jnp.int32, gate_scr.shape, 1)
    gate_e = jnp.sum(jnp.where(lane == e, gate_scr[...], 0.0), axis=-1, keepdims=True)
    acc_scr[...] += gate_e * y

    @pl.when((e == N_EXPERTS - 1) & (f == nf - 1))
    def _():
        xo = x_ref[...] + mod_ref[5:6, :] * acc_scr[...]
        if final_norm:
            xo = xo * lax.rsqrt(jnp.mean(xo * xo, axis=-1, keepdims=True) + EPS) * fnw_ref[...]
        xo_ref[...] = xo


def _moe(x, mod, nw, wr, br, w1, w3, w2, fnw, final_norm):
    b, t, _ = x.shape
    dff = w1.shape[2]
    tm = _tile(t, 1024)
    tf = dff // 4
    nf = dff // tf
    return pl.pallas_call(
        functools.partial(_moe_kernel, nf=nf, final_norm=final_norm),
        grid=(b, t // tm, N_EXPERTS, nf),
        in_specs=[pl.BlockSpec((None, tm, D_MODEL), lambda b_, i, e, f: (b_, i, 0)),
                  pl.BlockSpec((None, 8, D_MODEL), lambda b_, i, e, f: (b_, 0, 0)),
                  pl.BlockSpec((1, D_MODEL), lambda b_, i, e, f: (0, 0)),
                  pl.BlockSpec((D_MODEL, LANE), lambda b_, i, e, f: (0, 0)),
                  pl.BlockSpec((1, LANE), lambda b_, i, e, f: (0, 0)),
                  pl.BlockSpec((None, D_MODEL, tf), lambda b_, i, e, f: (e, 0, f)),
                  pl.BlockSpec((None, D_MODEL, tf), lambda b_, i, e, f: (e, 0, f)),
                  pl.BlockSpec((None, tf, D_MODEL), lambda b_, i, e, f: (e, f, 0)),
                  pl.BlockSpec((1, D_MODEL), lambda b_, i, e, f: (0, 0))],
        out_specs=pl.BlockSpec((None, tm, D_MODEL), lambda b_, i, e, f: (b_, i, 0)),
        out_shape=jax.ShapeDtypeStruct((b, t, D_MODEL), F32),
        scratch_shapes=[pltpu.VMEM((tm, D_MODEL), BF16), pltpu.VMEM((tm, LANE), F32),
                        pltpu.VMEM((tm, D_MODEL), F32)],
        compiler_params=_cparams(("arbitrary", "arbitrary", "arbitrary", "arbitrary")),
        name="moe",
    )(x, mod, nw, wr, br, w1, w3, w2, fnw)


def _pad_lanes(v, fill=0.0):
    return jnp.pad(v.astype(F32), (0, LANE - v.shape[0]), constant_values=fill)[None, :]


def kernel(x, c, ctx, c_ctx, w_mod, b_mod, norm_w, w_in, s5_lam_re, s5_lam_im, s5_log_dt, s5_b_re, s5_b_im, s5_c_re, s5_c_im, s5_d, s5_w_glu, lru_conv_w, lru_conv_b, lru_w_a, lru_b_a, lru_w_x, lru_b_x, lru_lam, ssd_conv_w, ssd_conv_b, ssd_a_log, ssd_dt_bias, ssd_d, ssd_norm_w, w_br_a, w_br_b, w_br_c, w_out, ffn_w1, ffn_w3, ffn_w2, moe_w_router, moe_b_router, moe_w1, moe_w3, moe_w2, final_norm_w):
    depth = w_mod.shape[0]
    nb, t_lat, _ = x.shape
    t_ctx = ctx.shape[1]
    assert t_lat % GRID_W == 0 and t_lat % SSD_CHUNK == 0 and t_ctx % SSD_CHUNK == 0

    cond_in = jnp.zeros((8, D_MODEL), F32).at[:nb].set(c).at[nb].set(c_ctx)
    mods = _modulation(cond_in, w_mod, b_mod).reshape(depth, 8, N_MOD, D_MODEL)
    mods = jnp.pad(mods, ((0, 0), (0, 0), (0, 8 - N_MOD), (0, 0)))

    head_of = jnp.arange(SSD_INNER) // SSD_HEAD_DIM
    e_mat = (jnp.arange(LANE)[:, None] == head_of[None, :]).astype(BF16)
    n_lev_lat = max(1, math.ceil(math.log2(t_lat // S5_CHUNK)))

    x_lat, x_ctx = x, ctx
    for l in range(depth):
        ctx_out = l < depth - 1
        last = l == depth - 1
        mod_lat = mods[l, :nb]
        mod_ctx = jnp.broadcast_to(mods[l, nb][None], (nb, 8, D_MODEL))
        nw0 = norm_w[l, 0][None, :]
        nw1 = norm_w[l, 1][None, :]

        wl = w_in[l]
        o_lx = S5_WIDTH
        o_lg = o_lx + LRU_WIDTH
        o_z = o_lg + LRU_WIDTH
        o_xbc = o_z + SSD_INNER
        o_dt = o_xbc + SSD_XBC
        o_g = o_dt + SSD_HEADS
        w_main = jnp.concatenate(
            [wl[:, :o_lx], jnp.zeros((D_MODEL, P_LX - S5_WIDTH), wl.dtype), wl[:, o_lx:o_dt], wl[:, o_g:]],
            axis=1).astype(BF16)
        w_dt = jnp.pad(wl[:, o_dt:o_g], ((0, 0), (0, LANE - SSD_HEADS))).astype(BF16)
        tz, pm, rm, atab = _s5_setup(s5_lam_re[l], s5_lam_im[l], s5_log_dt[l], s5_b_re[l], s5_b_im[l],
                                     s5_c_re[l], s5_c_im[l], n_lev_lat)
        wg, bg = _lru_gate_weights(lru_w_a[l], lru_b_a[l], lru_w_x[l], lru_b_x[l])
        lcw = lru_conv_w[l].astype(F32)
        lcb = lru_conv_b[l][None, :].astype(F32)
        scw = ssd_conv_w[l].astype(F32)
        scb = ssd_conv_b[l][None, :].astype(F32)
        a_log = jnp.concatenate([_pad_lanes(ssd_a_log[l, 0]), _pad_lanes(ssd_a_log[l, 1])], axis=0)
        dt_bias = jnp.concatenate([_pad_lanes(ssd_dt_bias[l, 0]), _pad_lanes(ssd_dt_bias[l, 1])], axis=0)
        dskip_x = jnp.repeat(ssd_d[l].astype(F32), SSD_HEAD_DIM)[None, :]

        def mixers(p, dt, t, w, s5_h0, lru_h0, ssd_h0f, ssd_h0b):
            y5, s5_ht = _s5_scan(_s5_chunks(p, nb, t), tz, pm, rm, atab, s5_h0, nb)
            ya = _s5_unchunk(y5, nb, t)
            yb, lru_ht = _lru_scan(p, lcw, lcb, wg, bg, lru_lam[l].astype(F32), lru_h0, w)
            yc, ssd_htf, ssd_htb = _ssd_scan(p, dt, scw, scb, a_log, dt_bias, dskip_x, e_mat, ssd_h0f, ssd_h0b)
            return ya, yb, yc, s5_ht, lru_ht, ssd_htf, ssd_htb

        p_ctx, dt_ctx = _inproj(x_ctx, mod_ctx, nw0, w_main, w_dt)
        zs5 = jnp.zeros((S5_GROUPS, nb, 4 * S5_STATE), F32)
        zlru = jnp.zeros((nb, 2, LRU_WIDTH), F32)
        zssd = jnp.zeros((nb, SSD_GROUPS * SSD_STATE, SSD_GW), F32)
        ya_c, yb_c, yc_c, s5_h, lru_h, ssd_hf, ssd_hb = mixers(p_ctx, dt_ctx, t_ctx, 1, zs5, zlru, zssd, zssd)

        p_lat, dt_lat = _inproj(x_lat, mod_lat, nw0, w_main, w_dt)
        ya, yb, yc, _, _, _, _ = mixers(p_lat, dt_lat, t_lat, GRID_W, s5_h, lru_h, ssd_hf, ssd_hb)

        mw = (s5_d[l][None, :].astype(F32), s5_w_glu[l].astype(BF16), ssd_norm_w[l][None, :].astype(F32),
              w_br_a[l].astype(BF16), w_br_b[l].astype(BF16), w_br_c[l].astype(BF16), w_out[l].astype(BF16))
        x_lat = _merge(x_lat, mod_lat, p_lat, ya, yb, yc, *mw)
        if ctx_out:
            x_ctx = _merge(x_ctx, mod_ctx, p_ctx, ya_c, yb_c, yc_c, *mw)

        fnw = final_norm_w[None, :].astype(F32)
        if l % 2 == 0:
            fw = (ffn_w1[l // 2].astype(BF16), ffn_w3[l // 2].astype(BF16), ffn_w2[l // 2].astype(BF16))
            x_lat = _ffn(x_lat, mod_lat, nw1, *fw, fnw, last)
            if ctx_out:
                x_ctx = _ffn(x_ctx, mod_ctx, nw1, *fw, fnw, False)
        else:
            wr = jnp.pad(moe_w_router[l // 2].astype(F32), ((0, 0), (0, LANE - N_EXPERTS)))
            br = _pad_lanes(moe_b_router[l // 2])
            ew = (moe_w1[l // 2].astype(BF16), moe_w3[l // 2].astype(BF16), moe_w2[l // 2].astype(BF16))
            x_lat = _moe(x_lat, mod_lat, nw1, wr, br, *ew, fnw, last)
            if ctx_out:
                x_ctx = _moe(x_ctx, mod_ctx, nw1, wr, br, *ew, fnw, False)
    return x_lat
```

```python
import functools
import math

import jax
import jax.numpy as jnp
from jax import lax
from jax.experimental import pallas as pl
from jax.experimental.pallas import tpu as pltpu

F32 = jnp.float32
BF16 = jnp.bfloat16
HIGHEST = lax.Precision.HIGHEST

D_MODEL = 1024
GRID_W = 64
N_MOD = 6
EPS = 1e-6

S5_WIDTH = 768
S5_GROUP = 16
S5_GROUPS = S5_WIDTH // S5_GROUP
S5_STATE = 64
S5_CHUNK = 32
S5_CW = S5_CHUNK * S5_GROUP

LRU_WIDTH = 1024
LRU_BLOCKS = 16
LRU_BLOCK = LRU_WIDTH // LRU_BLOCKS
LRU_CONV = 4
LRU_C = 8.0
LRU_SLAB = 128

SSD_INNER = 1024
SSD_HEAD_DIM = 64
SSD_HEADS = SSD_INNER // SSD_HEAD_DIM
SSD_GROUPS = 4
SSD_HPG = SSD_HEADS // SSD_GROUPS
SSD_STATE = 128
SSD_CONV = 4
SSD_CHUNK = 128
SSD_XBC = SSD_INNER + 2 * SSD_GROUPS * SSD_STATE
SSD_GW = SSD_HPG * SSD_HEAD_DIM

N_EXPERTS = 8
TOP_K = 2
LANE = 128

P_U, P_LX, P_LG, P_Z, P_XBC, P_GATE = 0, 1024, 2048, 3072, 4096, 6144
P_TOTAL = 9216

VMEM_LIMIT = 56 * 1024 * 1024


def _cparams(sem):
    return pltpu.CompilerParams(dimension_semantics=sem, vmem_limit_bytes=VMEM_LIMIT)


def _tile(n, pref):
    t = min(n, pref)
    while n % t:
        t //= 2
    return t


def _sigmoid(x):
    return jax.nn.sigmoid(x)


def _sigmoid_t(x):
    return 0.5 * jnp.tanh(0.5 * x) + 0.5


def _silu(x):
    return x * jax.nn.sigmoid(x)


def _softplus(x):
    return jnp.maximum(x, 0.0) + jnp.log(1.0 + jnp.exp(-jnp.abs(x)))


def _rms_mod(x, nw, shift, scale):
    y = x * lax.rsqrt(jnp.mean(x * x, axis=-1, keepdims=True) + EPS) * nw
    return y * (1.0 + scale) + shift


def _mod_kernel(c_ref, w_ref, b_ref, o_ref):
    cond = _silu(c_ref[...])
    o_ref[...] = jnp.dot(cond, w_ref[...], preferred_element_type=F32, precision=HIGHEST) + b_ref[...]


def _modulation(cond_in, w_mod, b_mod):
    depth = w_mod.shape[0]
    n = w_mod.shape[2]
    tn = 1024
    return pl.pallas_call(
        _mod_kernel,
        grid=(depth, n // tn),
        in_specs=[pl.BlockSpec((8, D_MODEL), lambda l, j: (0, 0)),
                  pl.BlockSpec((None, D_MODEL, tn), lambda l, j: (l, 0, j)),
                  pl.BlockSpec((None, 1, tn), lambda l, j: (l, 0, j))],
        out_specs=pl.BlockSpec((None, 8, tn), lambda l, j: (l, 0, j)),
        out_shape=jax.ShapeDtypeStruct((depth, 8, n), F32),
        compiler_params=_cparams(("arbitrary", "arbitrary")),
        name="modulation",
    )(cond_in, w_mod, b_mod.reshape(depth, 1, n))


def _inproj_kernel(x_ref, mod_ref, nw_ref, w_ref, wdt_ref, p_ref, dt_ref, h_scr):
    @pl.when(pl.program_id(2) == 0)
    def _():
        h = _rms_mod(x_ref[...], nw_ref[...], mod_ref[0:1, :], mod_ref[1:2, :]).astype(BF16)
        h_scr[...] = h
        dt_ref[...] = jnp.dot(h, wdt_ref[...], preferred_element_type=F32)

    p_ref[...] = jnp.dot(h_scr[...], w_ref[...], preferred_element_type=F32).astype(p_ref.dtype)


def _inproj(x, mod, nw, w, wdt):
    b, t, _ = x.shape
    tm = _tile(t, 1024)
    tn = 1536
    return pl.pallas_call(
        _inproj_kernel,
        grid=(b, t // tm, P_TOTAL // tn),
        in_specs=[pl.BlockSpec((None, tm, D_MODEL), lambda b_, i, j: (b_, i, 0)),
                  pl.BlockSpec((None, 8, D_MODEL), lambda b_, i, j: (b_, 0, 0)),
                  pl.BlockSpec((1, D_MODEL), lambda b_, i, j: (0, 0)),
                  pl.BlockSpec((D_MODEL, tn), lambda b_, i, j: (0, j)),
                  pl.BlockSpec((D_MODEL, LANE), lambda b_, i, j: (0, 0))],
        out_specs=[pl.BlockSpec((None, tm, tn), lambda b_, i, j: (b_, i, j)),
                   pl.BlockSpec((None, tm, LANE), lambda b_, i, j: (b_, i, 0))],
        out_shape=[jax.ShapeDtypeStruct((b, t, P_TOTAL), BF16),
                   jax.ShapeDtypeStruct((b, t, LANE), F32)],
        scratch_shapes=[pltpu.VMEM((tm, D_MODEL), BF16)],
        compiler_params=_cparams(("arbitrary", "arbitrary", "arbitrary")),
        name="inproj",
    )(x, mod, nw, w, wdt)


def _s5_setup(lam_re, lam_im, log_dt, b_re, b_im, c_re, c_im, n_levels):
    t = S5_CHUNK
    dt = jnp.exp(log_dt.astype(F32))[..., None]
    lre, lim = lam_re.astype(F32), lam_im.astype(F32)
    are, aim = lre * dt, lim * dt
    k = jnp.arange(t + 1, dtype=F32)[:, None]
    mag = jnp.exp(are[:, :, None, :] * k)
    pw_re = mag * jnp.cos(aim[:, :, None, :] * k)
    pw_im = mag * jnp.sin(aim[:, :, None, :] * k)
    a_re, a_im = pw_re[:, :, 1], pw_im[:, :, 1]
    den = lre * lre + lim * lim
    q_re = ((a_re - 1.0) * lre + a_im * lim) / den
    q_im = (a_im * lre - (a_re - 1.0) * lim) / den
    bb_re = q_re[..., None] * b_re - q_im[..., None] * b_im
    bb_im = q_re[..., None] * b_im + q_im[..., None] * b_re
    cr, ci = c_re.astype(F32), c_im.astype(F32)

    cb_re = cr[..., None] * bb_re[:, :, None] - ci[..., None] * bb_im[:, :, None]
    cb_im = cr[..., None] * bb_im[:, :, None] + ci[..., None] * bb_re[:, :, None]
    kern = (jnp.einsum('dgkp,dgjpi->dgkji', pw_re[:, :, :t], cb_re, precision=HIGHEST)
            - jnp.einsum('dgkp,dgjpi->dgkji', pw_im[:, :, :t], cb_im, precision=HIGHEST))
    kf, kb = kern[0], kern[1]
    s_idx = jnp.arange(t)[:, None]
    t_idx = jnp.arange(t)[None, :]
    lag = t_idx - s_idx
    kf_g = kf[:, jnp.abs(lag)]
    kb_g = kb[:, jnp.abs(lag)]
    lag5 = lag[None, :, :, None, None]
    tz = jnp.where(lag5 > 0, kf_g, jnp.where(lag5 < 0, kb_g, kf_g + kb_g))
    tz = tz.transpose(0, 1, 4, 2, 3).reshape(S5_GROUPS, S5_CW, S5_CW)

    pf_re, pf_im = pw_re[0, :, :t][:, ::-1], pw_im[0, :, :t][:, ::-1]
    pb_re, pb_im = pw_re[1, :, :t], pw_im[1, :, :t]

    def in_map(p_re, p_im, d):
        w_re = p_re[:, :, None, :] * bb_re[d].transpose(0, 2, 1)[:, None] - p_im[:, :, None, :] * bb_im[d].transpose(0, 2, 1)[:, None]
        w_im = p_re[:, :, None, :] * bb_im[d].transpose(0, 2, 1)[:, None] + p_im[:, :, None, :] * bb_re[d].transpose(0, 2, 1)[:, None]
        return w_re, w_im

    wf_re, wf_im = in_map(pf_re, pf_im, 0)
    wb_re, wb_im = in_map(pb_re, pb_im, 1)
    pm = jnp.concatenate([wf_re, wf_im, wb_re, wb_im], axis=-1).reshape(S5_GROUPS, S5_CW, 4 * S5_STATE)

    def out_map(p_re, p_im, d):
        c_r = cr[d].transpose(0, 2, 1)[:, :, None, :]
        c_i = ci[d].transpose(0, 2, 1)[:, :, None, :]
        e_r = p_re.transpose(0, 2, 1)[..., None]
        e_i = p_im.transpose(0, 2, 1)[..., None]
        m_re = c_r * e_r - c_i * e_i
        m_im = c_r * e_i + c_i * e_r
        return m_re, -m_im

    rf_re, rf_im = out_map(pw_re[0, :, 1:t + 1], pw_im[0, :, 1:t + 1], 0)
    rb_re, rb_im = out_map(pw_re[1, :, 1:t + 1][:, ::-1], pw_im[1, :, 1:t + 1][:, ::-1], 1)
    rm = jnp.concatenate([rf_re, rf_im, rb_re, rb_im], axis=1).reshape(S5_GROUPS, 4 * S5_STATE, S5_CW)

    lev = (t * 2.0 ** jnp.arange(n_levels, dtype=F32))[:, None]
    lmag = jnp.exp(are[:, :, None, :] * lev)
    l_re = lmag * jnp.cos(aim[:, :, None, :] * lev)
    l_im = lmag * jnp.sin(aim[:, :, None, :] * lev)
    row_a = jnp.concatenate([l_re[0], l_re[0], l_re[1], l_re[1]], axis=-1)
    row_b = jnp.concatenate([-l_im[0], l_im[0], -l_im[1], l_im[1]], axis=-1)
    atab = jnp.stack([row_a, row_b], axis=2)
    return tz.astype(BF16), pm.astype(BF16), rm.astype(BF16), atab


def _shift_rows(h, d, down):
    n = h.shape[0]
    if d >= n:
        return jnp.zeros_like(h)
    row = lax.broadcasted_iota(jnp.int32, h.shape, 0)
    if down:
        return jnp.where(row >= d, pltpu.roll(h, d, 0), 0.0)
    return jnp.where(row < n - d, pltpu.roll(h, n - d, 0), 0.0)


def _s5_kernel(x_ref, tz_ref, pm_ref, rm_ref, at_ref, h0_ref, y_ref, ht_ref, hin_scr, *, nb, nc, n_levels):
    half = 2 * S5_STATE
    x = x_ref[...]
    s_all = jnp.dot(x, pm_ref[...], preferred_element_type=F32)
    row = lax.broadcasted_iota(jnp.int32, (nc, half), 0)

    def cmul(h, k, lo):
        a = at_ref[k, 0:1, lo:lo + half]
        b = at_ref[k, 1:2, lo:lo + half]
        return a * h + b * pltpu.roll(h, S5_STATE, 1)

    for b in range(nb):
        for d in range(2):
            lo = d * half
            h = s_all[b * nc:(b + 1) * nc, lo:lo + half]
            h0 = h0_ref[b:b + 1, lo:lo + half]
            edge = 0 if d == 0 else nc - 1
            h = h + jnp.where(row == edge, cmul(jnp.broadcast_to(h0, (nc, half)), 0, lo), 0.0)
            for k in range(n_levels):
                if (1 << k) < nc:
                    h = h + cmul(_shift_rows(h, 1 << k, d == 0), k, lo)
            ht_ref[b:b + 1, lo:lo + half] = h[nc - 1 - edge:nc - edge]
            hin = _shift_rows(h, 1, d == 0)
            hin = jnp.where(row == edge, jnp.broadcast_to(h0, (nc, half)), hin)
            hin_scr[b * nc:(b + 1) * nc, lo:lo + half] = hin

    y = jnp.dot(x, tz_ref[...], preferred_element_type=F32)
    y = y + jnp.dot(hin_scr[...].astype(BF16), rm_ref[...], preferred_element_type=F32)
    y_ref[...] = y.astype(y_ref.dtype)


def _s5_scan(xg, tz, pm, rm, atab, h0, nb):
    g, rows, _ = xg.shape
    nc = rows // nb
    n_levels = atab.shape[1]
    kern = functools.partial(_s5_kernel, nb=nb, nc=nc, n_levels=n_levels)
    return pl.pallas_call(
        kern,
        grid=(g,),
        in_specs=[pl.BlockSpec((None, rows, S5_CW), lambda i: (i, 0, 0)),
                  pl.BlockSpec((None, S5_CW, S5_CW), lambda i: (i, 0, 0)),
                  pl.BlockSpec((None, S5_CW, 4 * S5_STATE), lambda i: (i, 0, 0)),
                  pl.BlockSpec((None, 4 * S5_STATE, S5_CW), lambda i: (i, 0, 0)),
                  pl.BlockSpec((None, n_levels, 2, 4 * S5_STATE), lambda i: (i, 0, 0, 0)),
                  pl.BlockSpec((None, nb, 4 * S5_STATE), lambda i: (i, 0, 0))],
        out_specs=[pl.BlockSpec((None, rows, S5_CW), lambda i: (i, 0, 0)),
                   pl.BlockSpec((None, nb, 4 * S5_STATE), lambda i: (i, 0, 0))],
        out_shape=[jax.ShapeDtypeStruct((g, rows, S5_CW), BF16),
                   jax.ShapeDtypeStruct((g, nb, 4 * S5_STATE), F32)],
        scratch_shapes=[pltpu.VMEM((rows, 4 * S5_STATE), F32)],
        compiler_params=_cparams(("arbitrary",)),
        name="s5_scan",
    )(xg, tz, pm, rm, atab, h0)


def _s5_chunks(p, nb, t):
    nc = t // S5_CHUNK
    u = p[:, :, P_U:P_U + S5_WIDTH].reshape(nb, nc, S5_CHUNK, S5_GROUPS, S5_GROUP)
    return u.transpose(3, 0, 1, 2, 4).reshape(S5_GROUPS, nb * nc, S5_CW)


def _s5_unchunk(y, nb, t):
    nc = t // S5_CHUNK
    y = y.reshape(S5_GROUPS, nb, nc, S5_CHUNK, S5_GROUP)
    return y.transpose(1, 2, 3, 0, 4).reshape(nb, t, S5_WIDTH)


def _lru_kernel(x_ref, cw_ref, cb_ref, wg_ref, bg_ref, lam_ref, h0_ref, out_ref, ht_ref,
                xp_scr, xc_scr, af_scr, bf_scr, ab_scr, bb_scr, cf_scr, cr_scr, *, w, s, pad_top, ch, ns):
    l = w * s
    n_ch = l // ch
    slabs = range(ns)
    lanes = [slice(k * LRU_SLAB, (k + 1) * LRU_SLAB) for k in slabs]
    zero_slab = jnp.zeros((w, LRU_SLAB), F32)
    one_slab = jnp.ones((w, LRU_SLAB), F32)

    for k in slabs:
        xp_scr[k, 0:pad_top, :] = jnp.zeros((pad_top, LRU_SLAB), F32)
        xp_scr[k, pad_top + l:pad_top + l + pad_top, :] = jnp.zeros((pad_top, LRU_SLAB), F32)

    def copy_body(i, _):
        r0 = pl.multiple_of(i * ch, ch)
        for k in slabs:
            xp_scr[k, pl.ds(pad_top + r0, ch), :] = x_ref[pl.ds(r0, ch), lanes[k]].astype(F32)
        return 0
    lax.fori_loop(0, n_ch, copy_body, 0)

    def conv_body(i, _):
        r0 = pl.multiple_of(i * ch, ch)
        for k in slabs:
            acc = cb_ref[:, lanes[k]] + cw_ref[2:3, lanes[k]] * xp_scr[k, pl.ds(pad_top + r0, ch), :]
            acc = acc + cw_ref[0:1, lanes[k]] * xp_scr[k, pl.ds(pad_top + r0 - 2 * w, ch), :]
            acc = acc + cw_ref[1:2, lanes[k]] * xp_scr[k, pl.ds(pad_top + r0 - w, ch), :]
            acc = acc + cw_ref[3:4, lanes[k]] * xp_scr[k, pl.ds(pad_top + r0 + w, ch), :]
            xc_scr[k, pl.ds(r0, ch), :] = acc
        return 0
    lax.fori_loop(0, n_ch, conv_body, 0)

    if w > 1:
        def prev_col(v):
            return _shift_rows(v, 1, True)

        def next_col(v):
            return _shift_rows(v, 1, False)

        for k in slabs:
            x_last = xp_scr[k, pad_top + (s - 1) * w:pad_top + s * w, :]
            x_last2 = xp_scr[k, pad_top + (s - 2) * w:pad_top + (s - 1) * w, :]
            x_first = xp_scr[k, pad_top:pad_top + w, :]
            w0, w1, w3 = cw_ref[0:1, lanes[k]], cw_ref[1:2, lanes[k]], cw_ref[3:4, lanes[k]]
            xc_scr[k, 0:w, :] = xc_scr[k, 0:w, :] + w0 * prev_col(x_last2) + w1 * prev_col(x_last)
            xc_scr[k, w:2 * w, :] = xc_scr[k, w:2 * w, :] + w0 * prev_col(x_last)
            xc_scr[k, (s - 1) * w:s * w, :] = xc_scr[k, (s - 1) * w:s * w, :] + w3 * next_col(x_first)

    log_sig = -_softplus(-lam_ref[...])

    def coef_body(i, _):
        r0 = pl.multiple_of(i * ch, ch)
        for k in slabs:
            xc = xc_scr[k, pl.ds(r0, ch), :]
            g = jnp.dot(xc.astype(BF16), wg_ref[k], preferred_element_type=F32) + bg_ref[k]
            for d, (a_scr, b_scr) in enumerate(((af_scr, bf_scr), (ab_scr, bb_scr))):
                rg = _sigmoid_t(g[:, (2 * d) * LRU_SLAB:(2 * d + 1) * LRU_SLAB])
                ig = _sigmoid_t(g[:, (2 * d + 1) * LRU_SLAB:(2 * d + 2) * LRU_SLAB])
                a = jnp.exp(LRU_C * rg * log_sig[d:d + 1, lanes[k]])
                a_scr[k, pl.ds(r0, ch), :] = a
                b_scr[k, pl.ds(r0, ch), :] = jnp.sqrt(1.0 - a * a) * (ig * xc)
        return 0
    lax.fori_loop(0, n_ch, coef_body, 0)

    def scan_body(r, carry):
        rf = pl.multiple_of(r * w, w)
        rb = pl.multiple_of((s - 1 - r) * w, w)
        out = []
        for k in slabs:
            hf, pf, hb, pb = carry[k]
            a = af_scr[k, pl.ds(rf, w), :]
            hf = a * hf + bf_scr[k, pl.ds(rf, w), :]
            pf = a * pf
            bf_scr[k, pl.ds(rf, w), :] = hf
            af_scr[k, pl.ds(rf, w), :] = pf
            a = ab_scr[k, pl.ds(rb, w), :]
            hb = a * hb + bb_scr[k, pl.ds(rb, w), :]
            pb = a * pb
            bb_scr[k, pl.ds(rb, w), :] = hb
            ab_scr[k, pl.ds(rb, w), :] = pb
            out.append((hf, pf, hb, pb))
        return tuple(out)
    lax.fori_loop(0, s, scan_body, tuple((zero_slab, one_slab, zero_slab, one_slab) for _ in slabs))

    def carry_f(c, carry):
        out = []
        for k in slabs:
            cf_scr[k, pl.ds(c, 1), :] = carry[k]
            out.append(bf_scr[k, pl.ds((s - 1) * w + c, 1), :] + af_scr[k, pl.ds((s - 1) * w + c, 1), :] * carry[k])
        return tuple(out)
    fin = lax.fori_loop(0, w, carry_f, tuple(h0_ref[0:1, lanes[k]] for k in slabs))
    for k in slabs:
        ht_ref[0:1, lanes[k]] = fin[k]

    def carry_b(i, carry):
        c = w - 1 - i
        out = []
        for k in slabs:
            cr_scr[k, pl.ds(c, 1), :] = carry[k]
            out.append(bb_scr[k, pl.ds(c, 1), :] + ab_scr[k, pl.ds(c, 1), :] * carry[k])
        return tuple(out)
    fin = lax.fori_loop(0, w, carry_b, tuple(h0_ref[1:2, lanes[k]] for k in slabs))
    for k in slabs:
        ht_ref[1:2, lanes[k]] = fin[k]

    def out_body(r, _):
        r0 = pl.multiple_of(r * w, w)
        for k in slabs:
            y = (bf_scr[k, pl.ds(r0, w), :] + af_scr[k, pl.ds(r0, w), :] * cf_scr[k]
                 + bb_scr[k, pl.ds(r0, w), :] + ab_scr[k, pl.ds(r0, w), :] * cr_scr[k])
            bf_scr[k, pl.ds(r0, w), :] = y
        return 0
    lax.fori_loop(0, s, out_body, 0)

    def store_body(i, _):
        r0 = pl.multiple_of(i * ch, ch)
        for k in slabs:
            out_ref[pl.ds(r0, ch), lanes[k]] = bf_scr[k, pl.ds(r0, ch), :].astype(out_ref.dtype)
        return 0
    lax.fori_loop(0, n_ch, store_body, 0)


def _lru_scan(p, cw, cb, wg, bg, lam, h0, w, ns):
    b, t, _ = p.shape
    s = t // w
    assert s * w == t and s >= 4
    ch = _tile(t, 256)
    pad_top = max(2 * w, 8)
    cwid = ns * LRU_SLAB
    col0 = P_LX // cwid
    kern = functools.partial(_lru_kernel, w=w, s=s, pad_top=pad_top, ch=ch, ns=ns)
    return pl.pallas_call(
        kern,
        grid=(b, LRU_WIDTH // cwid),
        in_specs=[pl.BlockSpec((None, t, cwid), lambda b_, k: (b_, 0, col0 + k)),
                  pl.BlockSpec((LRU_CONV, cwid), lambda b_, k: (0, k)),
                  pl.BlockSpec((1, cwid), lambda b_, k: (0, k)),
                  pl.BlockSpec((ns, LRU_SLAB, 4 * LRU_SLAB), lambda b_, k: (k, 0, 0)),
                  pl.BlockSpec((ns, 1, 4 * LRU_SLAB), lambda b_, k: (k, 0, 0)),
                  pl.BlockSpec((2, cwid), lambda b_, k: (0, k)),
                  pl.BlockSpec((None, 2, cwid), lambda b_, k: (b_, 0, k))],
        out_specs=[pl.BlockSpec((None, t, cwid), lambda b_, k: (b_, 0, k)),
                   pl.BlockSpec((None, 2, cwid), lambda b_, k: (b_, 0, k))],
        out_shape=[jax.ShapeDtypeStruct((b, t, LRU_WIDTH), BF16),
                   jax.ShapeDtypeStruct((b, 2, LRU_WIDTH), F32)],
        scratch_shapes=[pltpu.VMEM((ns, t + 2 * pad_top, LRU_SLAB), F32),
                        pltpu.VMEM((ns, t, LRU_SLAB), F32),
                        pltpu.VMEM((ns, t, LRU_SLAB), F32), pltpu.VMEM((ns, t, LRU_SLAB), F32),
                        pltpu.VMEM((ns, t, LRU_SLAB), F32), pltpu.VMEM((ns, t, LRU_SLAB), F32),
                        pltpu.VMEM((ns, w, LRU_SLAB), F32), pltpu.VMEM((ns, w, LRU_SLAB), F32)],
        compiler_params=_cparams(("arbitrary", "arbitrary")),
        name="lru_scan",
    )(p, cw, cb, wg, bg, lam, h0)


def _lru_gate_weights(w_a, b_a, w_x, b_x):
    n_slab = LRU_WIDTH // LRU_SLAB
    per = LRU_SLAB // LRU_BLOCK

    def slab_diag(wm):
        wm = wm.reshape(n_slab, per, LRU_BLOCK, LRU_BLOCK)
        eye = jnp.eye(per, dtype=wm.dtype)
        return jnp.einsum('spkj,pq->spkqj', wm, eye).reshape(n_slab, LRU_SLAB, LRU_SLAB)

    wg = jnp.concatenate([slab_diag(w_a[0]), slab_diag(w_x[0]), slab_diag(w_a[1]), slab_diag(w_x[1])], axis=2)
    bg = jnp.concatenate([b_a[0].reshape(n_slab, 1, LRU_SLAB), b_x[0].reshape(n_slab, 1, LRU_SLAB),
                          b_a[1].reshape(n_slab, 1, LRU_SLAB), b_x[1].reshape(n_slab, 1, LRU_SLAB)], axis=2)
    return wg.astype(BF16), bg.astype(F32)


def _ssd_core(xs, bm, cm, dtr, alog, dbias, h_scr, y_scr, reverse):
    q = SSD_CHUNK
    a = -jnp.exp(alog)
    dt = _softplus(dtr + dbias)
    adt = dt * a
    row = lax.broadcasted_iota(jnp.int32, (q, q), 0)
    col = lax.broadcasted_iota(jnp.int32, (q, q), 1)
    tri = (col >= row) if reverse else (col <= row)
    cs = jnp.dot(tri.astype(F32), adt, preferred_element_type=F32, precision=HIGHEST)
    cs_t = cs.T
    dt_t = dt.T
    end = 0 if reverse else q - 1
    tot = cs[end:end + 1, :]
    w_end = dt * jnp.exp(tot - cs)
    ecs = jnp.exp(cs)
    etot = ecs[end:end + 1, :]
    xsb = xs.astype(BF16)

    for g in range(SSD_GROUPS):
        cg = cm[:, g * SSD_STATE:(g + 1) * SSD_STATE]
        bg = bm[:, g * SSD_STATE:(g + 1) * SSD_STATE]
        rows = slice(g * SSD_STATE, (g + 1) * SSD_STATE)
        sc = lax.dot_general(cg.astype(BF16), bg.astype(BF16), (((1,), (1,)), ((), ())),
                             preferred_element_type=F32)
        for j in range(SSD_HPG):
            hd = g * SSD_HPG + j
            cols = slice(hd * SSD_HEAD_DIM, (hd + 1) * SSD_HEAD_DIM)
            hcols = slice(j * SSD_HEAD_DIM, (j + 1) * SSD_HEAD_DIM)
            diff = cs[:, hd:hd + 1] - cs_t[hd:hd + 1, :]
            lm = jnp.where(tri, jnp.exp(diff), 0.0) * dt_t[hd:hd + 1, :]
            y = jnp.dot((sc * lm).astype(BF16), xsb[:, cols], preferred_element_type=F32)
            h_old = h_scr[rows, hcols]
            c_s = (cg * ecs[:, hd:hd + 1]).astype(BF16)
            y_scr[:, cols] = y + jnp.dot(c_s, h_old.astype(BF16), preferred_element_type=F32)
            b_s = (bg * w_end[:, hd:hd + 1]).astype(BF16)
            st = lax.dot_general(b_s, xsb[:, cols], (((0,), (0,)), ((), ())), preferred_element_type=F32)
            h_scr[rows, hcols] = etot[:, hd:hd + 1] * h_old + st
    return y_scr[...]


def _ssd_fwd_kernel(x_ref, xprev_ref, xnext_ref, dt_ref, cw_ref, cb_ref, alog_ref, dbias_ref, h0_ref,
                    y_ref, xc_ref, ht_ref, h_scr, y_scr, *, nc):
    c = pl.program_id(1)

    @pl.when(c == 0)
    def _():
        h_scr[...] = h0_ref[...]

    q = SSD_CHUNK
    x = x_ref[...].astype(F32)
    prev = jnp.where(c > 0, xprev_ref[...].astype(F32), 0.0)
    nxt = jnp.where(c < nc - 1, xnext_ref[...].astype(F32), 0.0)
    row8 = lax.broadcasted_iota(jnp.int32, (8, SSD_XBC), 0)

    def behind(k):
        xr = pltpu.roll(x, k, 0)
        top = jnp.where(row8 < k, pltpu.roll(prev, k, 0)[0:8], xr[0:8])
        return jnp.concatenate([top, xr[8:]], axis=0)

    xm2, xm1 = behind(2), behind(1)
    xr = pltpu.roll(x, q - 1, 0)
    bottom = jnp.where(row8 == 7, pltpu.roll(nxt, 15, 0)[8:16], xr[q - 8:q])
    xp1 = jnp.concatenate([xr[:q - 8], bottom], axis=0)
    conv = (cb_ref[...] + cw_ref[0:1, :] * xm2 + cw_ref[1:2, :] * xm1
            + cw_ref[2:3, :] * x + cw_ref[3:4, :] * xp1)
    xc = _silu(conv)
    xc_ref[...] = xc.astype(xc_ref.dtype)
    xs = xc[:, :SSD_INNER]
    bm = xc[:, SSD_INNER:SSD_INNER + SSD_GROUPS * SSD_STATE]
    cm = xc[:, SSD_INNER + SSD_GROUPS * SSD_STATE:]
    y_ref[...] = _ssd_core(xs, bm, cm, dt_ref[...], alog_ref[...], dbias_ref[...], h_scr, y_scr, False)

    @pl.when(c == nc - 1)
    def _():
        ht_ref[...] = h_scr[...]


def _ssd_bwd_kernel(xc_ref, dt_ref, yf_ref, alog_ref, dbias_ref, dskip_ref, h0_ref,
                    y_ref, ht_ref, h_scr, y_scr, *, nc):
    c = pl.program_id(1)

    @pl.when(c == 0)
    def _():
        h_scr[...] = h0_ref[...]

    xc = xc_ref[...].astype(F32)
    xs = xc[:, :SSD_INNER]
    bm = xc[:, SSD_INNER:SSD_INNER + SSD_GROUPS * SSD_STATE]
    cm = xc[:, SSD_INNER + SSD_GROUPS * SSD_STATE:]
    y = _ssd_core(xs, bm, cm, dt_ref[...], alog_ref[...], dbias_ref[...], h_scr, y_scr, True)
    y_ref[...] = y + yf_ref[...] + dskip_ref[...] * xs

    @pl.when(c == nc - 1)
    def _():
        ht_ref[...] = h_scr[...]


def _ssd_scan(p, dt, cw, cb, a_log, dt_bias, dskip_x, h0f, h0b):
    b, t, _ = p.shape
    q = SSD_CHUNK
    nc = t // q
    assert nc * q == t
    xbc_blk = P_XBC // SSD_XBC
    n16 = t // 16
    hs = (SSD_GROUPS * SSD_STATE, SSD_GW)
    const2 = lambda b_, c: (0, 0)
    state_spec = pl.BlockSpec((None,) + hs, lambda b_, c: (b_, 0, 0))

    yf, xc, htf = pl.pallas_call(
        functools.partial(_ssd_fwd_kernel, nc=nc),
        grid=(b, nc),
        in_specs=[pl.BlockSpec((None, q, SSD_XBC), lambda b_, c: (b_, c, xbc_blk)),
                  pl.BlockSpec((None, 16, SSD_XBC),
                               lambda b_, c: (b_, jnp.maximum(c * (q // 16) - 1, 0), xbc_blk)),
                  pl.BlockSpec((None, 16, SSD_XBC),
                               lambda b_, c: (b_, jnp.minimum((c + 1) * (q // 16), n16 - 1), xbc_blk)),
                  pl.BlockSpec((None, q, LANE), lambda b_, c: (b_, c, 0)),
                  pl.BlockSpec((SSD_CONV, SSD_XBC), const2),
                  pl.BlockSpec((1, SSD_XBC), const2),
                  pl.BlockSpec((1, LANE), const2),
                  pl.BlockSpec((1, LANE), const2),
                  state_spec],
        out_specs=[pl.BlockSpec((None, q, SSD_INNER), lambda b_, c: (b_, c, 0)),
                   pl.BlockSpec((None, q, SSD_XBC), lambda b_, c: (b_, c, 0)),
                   state_spec],
        out_shape=[jax.ShapeDtypeStruct((b, t, SSD_INNER), F32),
                   jax.ShapeDtypeStruct((b, t, SSD_XBC), BF16),
                   jax.ShapeDtypeStruct((b,) + hs, F32)],
        scratch_shapes=[pltpu.VMEM(hs, F32), pltpu.VMEM((q, SSD_INNER), F32)],
        compiler_params=_cparams(("arbitrary", "arbitrary")),
        name="ssd_fwd",
    )(p, p, p, dt, cw, cb, a_log[0:1], dt_bias[0:1], h0f)

    rev = lambda b_, c: (b_, nc - 1 - c, 0)
    y, htb = pl.pallas_call(
        functools.partial(_ssd_bwd_kernel, nc=nc),
        grid=(b, nc),
        in_specs=[pl.BlockSpec((None, q, SSD_XBC), rev),
                  pl.BlockSpec((None, q, LANE), rev),
                  pl.BlockSpec((None, q, SSD_INNER), rev),
                  pl.BlockSpec((1, LANE), const2),
                  pl.BlockSpec((1, LANE), const2),
                  pl.BlockSpec((1, SSD_INNER), const2),
                  state_spec],
        out_specs=[pl.BlockSpec((None, q, SSD_INNER), rev), state_spec],
        out_shape=[jax.ShapeDtypeStruct((b, t, SSD_INNER), F32),
                   jax.ShapeDtypeStruct((b,) + hs, F32)],
        scratch_shapes=[pltpu.VMEM(hs, F32), pltpu.VMEM((q, SSD_INNER), F32)],
        compiler_params=_cparams(("arbitrary", "arbitrary")),
        name="ssd_bwd",
    )(xc, dt, yf, a_log[1:2], dt_bias[1:2], dskip_x, h0b)
    return y, htf, htb


def _merge_kernel(x_ref, mod_ref, u_ref, ya_ref, yb_ref, g_ref, yc_ref, z_ref, gates_ref,
                  d5_ref, wglu_ref, snw_ref, wa_ref, wb_ref, wc_ref, wo_ref, xo_ref):
    ya = jax.nn.gelu(ya_ref[...].astype(F32) + d5_ref[...] * u_ref[...].astype(F32))
    ya = ya * _sigmoid(jnp.dot(ya.astype(BF16), wglu_ref[...], preferred_element_type=F32))
    yb = yb_ref[...].astype(F32) * jax.nn.gelu(g_ref[...].astype(F32))
    yc = yc_ref[...] * _silu(z_ref[...].astype(F32))
    yc = yc * lax.rsqrt(jnp.mean(yc * yc, axis=-1, keepdims=True) + EPS) * snw_ref[...]
    mix = _sigmoid(gates_ref[:, 0:D_MODEL].astype(F32)) * jnp.dot(
        ya.astype(BF16), wa_ref[...], preferred_element_type=F32)
    mix = mix + _sigmoid(gates_ref[:, D_MODEL:2 * D_MODEL].astype(F32)) * jnp.dot(
        yb.astype(BF16), wb_ref[...], preferred_element_type=F32)
    mix = mix + _sigmoid(gates_ref[:, 2 * D_MODEL:3 * D_MODEL].astype(F32)) * jnp.dot(
        yc.astype(BF16), wc_ref[...], preferred_element_type=F32)
    y = jnp.dot(mix.astype(BF16), wo_ref[...], preferred_element_type=F32)
    xo_ref[...] = x_ref[...] + mod_ref[2:3, :] * y


def _merge(x, mod, p, ya, yb, yc, d5, wglu, snw, wa, wb, wc, wo):
    b, t, _ = x.shape
    tm = _tile(t, 512)
    tok = lambda width, blk: pl.BlockSpec((None, tm, width), lambda b_, i: (b_, i, blk))
    full = lambda r, c_: pl.BlockSpec((r, c_), lambda b_, i: (0, 0))
    return pl.pallas_call(
        _merge_kernel,
        grid=(b, t // tm),
        in_specs=[tok(D_MODEL, 0),
                  pl.BlockSpec((None, 8, D_MODEL), lambda b_, i: (b_, 0, 0)),
                  tok(S5_WIDTH, P_U // S5_WIDTH),
                  tok(S5_WIDTH, 0),
                  tok(LRU_WIDTH, 0),
                  tok(LRU_WIDTH, P_LG // LRU_WIDTH),
                  tok(SSD_INNER, 0),
                  tok(SSD_INNER, P_Z // SSD_INNER),
                  tok(3 * D_MODEL, P_GATE // (3 * D_MODEL)),
                  full(1, S5_WIDTH), full(S5_WIDTH, S5_WIDTH), full(1, SSD_INNER),
                  full(S5_WIDTH, D_MODEL), full(LRU_WIDTH, D_MODEL), full(SSD_INNER, D_MODEL),
                  full(D_MODEL, D_MODEL)],
        out_specs=tok(D_MODEL, 0),
        out_shape=jax.ShapeDtypeStruct((b, t, D_MODEL), F32),
        compiler_params=_cparams(("arbitrary", "arbitrary")),
        name="merge",
    )(x, mod, p, ya, yb, p, yc, p, p, d5, wglu, snw, wa, wb, wc, wo)


def _ffn_kernel(x_ref, mod_ref, nw_ref, w1_ref, w3_ref, w2_ref, fnw_ref, xo_ref, h_scr, acc_scr, *, nf, final_norm):
    f = pl.program_id(2)

    @pl.when(f == 0)
    def _():
        h_scr[...] = _rms_mod(x_ref[...], nw_ref[...], mod_ref[3:4, :], mod_ref[4:5, :]).astype(BF16)
        acc_scr[...] = jnp.zeros_like(acc_scr)

    h = h_scr[...]
    a = jnp.dot(h, w1_ref[...], preferred_element_type=F32)
    g = jnp.dot(h, w3_ref[...], preferred_element_type=F32)
    acc_scr[...] += jnp.dot((_silu(a) * g).astype(BF16), w2_ref[...], preferred_element_type=F32)

    @pl.when(f == nf - 1)
    def _():
        xo = x_ref[...] + mod_ref[5:6, :] * acc_scr[...]
        if final_norm:
            xo = xo * lax.rsqrt(jnp.mean(xo * xo, axis=-1, keepdims=True) + EPS) * fnw_ref[...]
        xo_ref[...] = xo


def _ffn(x, mod, nw, w1, w3, w2, fnw, final_norm):
    b, t, _ = x.shape
    dff = w1.shape[1]
    tm = _tile(t, 512)
    tf = dff // 2
    nf = dff // tf
    return pl.pallas_call(
        functools.partial(_ffn_kernel, nf=nf, final_norm=final_norm),
        grid=(b, t // tm, nf),
        in_specs=[pl.BlockSpec((None, tm, D_MODEL), lambda b_, i, f: (b_, i, 0)),
                  pl.BlockSpec((None, 8, D_MODEL), lambda b_, i, f: (b_, 0, 0)),
                  pl.BlockSpec((1, D_MODEL), lambda b_, i, f: (0, 0)),
                  pl.BlockSpec((D_MODEL, tf), lambda b_, i, f: (0, f)),
                  pl.BlockSpec((D_MODEL, tf), lambda b_, i, f: (0, f)),
                  pl.BlockSpec((tf, D_MODEL), lambda b_, i, f: (f, 0)),
                  pl.BlockSpec((1, D_MODEL), lambda b_, i, f: (0, 0))],
        out_specs=pl.BlockSpec((None, tm, D_MODEL), lambda b_, i, f: (b_, i, 0)),
        out_shape=jax.ShapeDtypeStruct((b, t, D_MODEL), F32),
        scratch_shapes=[pltpu.VMEM((tm, D_MODEL), BF16), pltpu.VMEM((tm, D_MODEL), F32)],
        compiler_params=_cparams(("arbitrary", "arbitrary", "arbitrary")),
        name="ffn",
    )(x, mod, nw, w1, w3, w2, fnw)


MOE_BLK = 512
SEL_LANE = 8


def _router_kernel(x_ref, mod_ref, nw_ref, wr_ref, br_ref, h_ref, g_ref):
    h = _rms_mod(x_ref[...], nw_ref[...], mod_ref[3:4, :], mod_ref[4:5, :])
    h_ref[...] = h.astype(BF16)
    logits = jnp.dot(h, wr_ref[...], preferred_element_type=F32, precision=HIGHEST) + br_ref[...]
    lane = lax.broadcasted_iota(jnp.int32, logits.shape, 1)
    logits = jnp.where(lane < N_EXPERTS, logits, -jnp.inf)
    t1 = jnp.max(logits, axis=-1, keepdims=True)
    i1 = jnp.min(jnp.where(logits == t1, lane, LANE), axis=-1, keepdims=True)
    rest = jnp.where(lane == i1, -jnp.inf, logits)
    t2 = jnp.max(rest, axis=-1, keepdims=True)
    i2 = jnp.min(jnp.where(rest == t2, lane, LANE), axis=-1, keepdims=True)
    e2 = jnp.exp(t2 - t1)
    den = 1.0 + e2
    gate = jnp.where(lane == i1, 1.0 / den, 0.0) + jnp.where(lane == i2, e2 / den, 0.0)
    flag = jnp.where((lane == i1 + SEL_LANE) | (lane == i2 + SEL_LANE), 1.0, 0.0)
    g_ref[...] = gate + flag


def _router(x, mod, nw, wr, br):
    b, t, _ = x.shape
    tm = _tile(t, 512)
    return pl.pallas_call(
        _router_kernel,
        grid=(b, t // tm),
        in_specs=[pl.BlockSpec((None, tm, D_MODEL), lambda b_, i: (b_, i, 0)),
                  pl.BlockSpec((None, 8, D_MODEL), lambda b_, i: (b_, 0, 0)),
                  pl.BlockSpec((1, D_MODEL), lambda b_, i: (0, 0)),
                  pl.BlockSpec((D_MODEL, LANE), lambda b_, i: (0, 0)),
                  pl.BlockSpec((1, LANE), lambda b_, i: (0, 0))],
        out_specs=[pl.BlockSpec((None, tm, D_MODEL), lambda b_, i: (b_, i, 0)),
                   pl.BlockSpec((None, tm, LANE), lambda b_, i: (b_, i, 0))],
        out_shape=[jax.ShapeDtypeStruct((b, t, D_MODEL), BF16),
                   jax.ShapeDtypeStruct((b, t, LANE), F32)],
        compiler_params=_cparams(("arbitrary", "arbitrary")),
        name="router",
    )(x, mod, nw, wr, br)


def _moe_plan(g, ts):
    n = g.shape[0]
    blk = MOE_BLK
    nt = n // ts
    n_blocks = TOP_K * n // blk + N_EXPERTS
    max_pairs = n_blocks + N_EXPERTS * nt
    sel = g[:, SEL_LANE:SEL_LANE + N_EXPERTS] > 0.5
    cum = jnp.cumsum(sel.astype(jnp.int32), axis=0)
    cnt = cum[-1]
    nblk = (cnt + blk - 1) // blk
    blk_end = jnp.cumsum(nblk)
    seg_off = (blk_end - nblk) * blk
    n_used = blk_end[-1]
    pos = jnp.where(sel, seg_off[None, :] + cum - 1, -1).astype(jnp.int32)
    bexp = jnp.minimum(jnp.searchsorted(blk_end, jnp.arange(n_blocks), side='right'), N_EXPERTS - 1)

    tile_end = cum[ts - 1::ts]
    tile_beg = jnp.concatenate([jnp.zeros((1, N_EXPERTS), jnp.int32), tile_end[:-1]], axis=0)
    d_lo = (seg_off[None, :] + tile_beg) // blk
    d_hi = (seg_off[None, :] + tile_end - 1) // blk
    npairs = jnp.where(tile_end > tile_beg, d_hi - d_lo + 1, 0)

    def expand(cnt_flat, cell_j, cell_e):
        cend = jnp.cumsum(cnt_flat)
        total = cend[-1]
        k = jnp.minimum(jnp.arange(max_pairs), total - 1)
        cell = jnp.searchsorted(cend, k, side='right')
        i = k - (cend[cell] - cnt_flat[cell])
        j, e = cell_j[cell], cell_e[cell]
        d = d_lo[j, e] + i
        valid = (jnp.arange(max_pairs) < total).astype(jnp.int32)
        return j.astype(jnp.int32), d.astype(jnp.int32), e.astype(jnp.int32), valid

    jj, ee = jnp.meshgrid(jnp.arange(nt), jnp.arange(N_EXPERTS), indexing='ij')
    by_tile = expand(npairs.reshape(-1), jj.reshape(-1), ee.reshape(-1))
    by_block = expand(npairs.T.reshape(-1), jj.T.reshape(-1), ee.T.reshape(-1))

    def edges(key, valid):
        prev = jnp.concatenate([jnp.full((1,), -1, jnp.int32), key[:-1]])
        nxt = jnp.concatenate([key[1:], jnp.full((1,), -1, jnp.int32)])
        nxt_valid = jnp.concatenate([valid[1:], jnp.zeros((1,), jnp.int32)])
        first = ((key != prev) & (valid == 1)).astype(jnp.int32)
        last = (((key != nxt) | (nxt_valid == 0)) & (valid == 1)).astype(jnp.int32)
        return first, last

    tj, td, te, tv = by_tile
    t_first, t_last = edges(tj, tv)
    bj, bd, be, bv = by_block
    b_first, _ = edges(bd, bv)
    return dict(pos=pos, pos_t=pos.T, gate=g[:, :N_EXPERTS], bexp=bexp.astype(jnp.int32),
                n_used=n_used.astype(jnp.int32).reshape(1), n_blocks=n_blocks, max_pairs=max_pairs,
                disp=(bj, bd, be, b_first, bv), comb=(tj, td, te, t_first, t_last, tv))


def _dispatch_kernel(pj, pd, pe, pfirst, pvalid, h_ref, post_ref, xs_ref, *, ts):
    k = pl.program_id(0)
    row = pd[k] * MOE_BLK + lax.broadcasted_iota(jnp.int32, (MOE_BLK, 1), 0)
    hit = post_ref[pl.ds(pe[k], 1), :] == row
    got = jnp.dot(jnp.where(hit, 1.0, 0.0).astype(BF16), h_ref[...], preferred_element_type=F32).astype(BF16)

    @pl.when(pfirst[k] == 1)
    def _():
        xs_ref[...] = got

    @pl.when((pfirst[k] == 0) & (pvalid[k] == 1))
    def _():
        xs_ref[...] = xs_ref[...] + got


def _dispatch(h, plan, ts):
    n = h.shape[0]
    rows = plan['n_blocks'] * MOE_BLK
    grid_spec = pltpu.PrefetchScalarGridSpec(
        num_scalar_prefetch=5,
        grid=(plan['max_pairs'],),
        in_specs=[pl.BlockSpec((ts, D_MODEL), lambda k, pj, pd, pe, pf, pv: (pj[k], 0)),
                  pl.BlockSpec((N_EXPERTS, ts), lambda k, pj, pd, pe, pf, pv: (0, pj[k]))],
        out_specs=pl.BlockSpec((MOE_BLK, D_MODEL), lambda k, pj, pd, pe, pf, pv: (pd[k], 0)),
    )
    return pl.pallas_call(
        functools.partial(_dispatch_kernel, ts=ts),
        grid_spec=grid_spec,
        out_shape=jax.ShapeDtypeStruct((rows, D_MODEL), BF16),
        compiler_params=_cparams(("arbitrary",)),
        name="moe_dispatch",
    )(*plan['disp'], h, plan['pos_t'])


def _experts_kernel(bexp, nused, x_ref, w1_ref, w3_ref, w2_ref, y_ref, acc_scr, *, nf):
    d = pl.program_id(0)
    f = pl.program_id(1)

    @pl.when(d < nused[0])
    def _():
        @pl.when(f == 0)
        def _():
            acc_scr[...] = jnp.zeros_like(acc_scr)

        x = x_ref[...]
        a = jnp.dot(x, w1_ref[...], preferred_element_type=F32)
        g = jnp.dot(x, w3_ref[...], preferred_element_type=F32)
        acc_scr[...] += jnp.dot((_silu(a) * g).astype(BF16), w2_ref[...], preferred_element_type=F32)

        @pl.when(f == nf - 1)
        def _():
            y_ref[...] = acc_scr[...].astype(y_ref.dtype)


def _experts(xs, plan, w1, w3, w2):
    dff = w1.shape[2]
    nf = 2
    tf = dff // nf
    n_blocks = plan['n_blocks']

    def blk(d, f, bexp, nused):
        return jnp.minimum(d, nused[0] - 1)

    def fidx(d, f, bexp, nused):
        return jnp.where(d < nused[0], f, nf - 1)

    grid_spec = pltpu.PrefetchScalarGridSpec(
        num_scalar_prefetch=2,
        grid=(n_blocks, nf),
        in_specs=[pl.BlockSpec((MOE_BLK, D_MODEL), lambda d, f, be, nu: (blk(d, f, be, nu), 0)),
                  pl.BlockSpec((None, D_MODEL, tf), lambda d, f, be, nu: (be[blk(d, f, be, nu)], 0, fidx(d, f, be, nu))),
                  pl.BlockSpec((None, D_MODEL, tf), lambda d, f, be, nu: (be[blk(d, f, be, nu)], 0, fidx(d, f, be, nu))),
                  pl.BlockSpec((None, tf, D_MODEL), lambda d, f, be, nu: (be[blk(d, f, be, nu)], fidx(d, f, be, nu), 0))],
        out_specs=pl.BlockSpec((MOE_BLK, D_MODEL), lambda d, f, be, nu: (blk(d, f, be, nu), 0)),
        scratch_shapes=[pltpu.VMEM((MOE_BLK, D_MODEL), F32)],
    )
    return pl.pallas_call(
        functools.partial(_experts_kernel, nf=nf),
        grid_spec=grid_spec,
        out_shape=jax.ShapeDtypeStruct(xs.shape, BF16),
        compiler_params=_cparams(("arbitrary", "arbitrary")),
        name="moe_experts",
    )(plan['bexp'], plan['n_used'], xs, w1, w3, w2)


def _combine_kernel(pj, pd, pe, pfirst, plast, pvalid, ys_ref, pos_ref, gate_ref, x_ref, mod_ref, fnw_ref,
                    xo_ref, acc_scr, *, ts, final_norm):
    k = pl.program_id(0)

    @pl.when(pfirst[k] == 1)
    def _():
        acc_scr[...] = jnp.zeros_like(acc_scr)

    @pl.when(pvalid[k] == 1)
    def _():
        e = pe[k]
        lane = lax.broadcasted_iota(jnp.int32, (ts, N_EXPERTS), 1)
        pos_e = jnp.sum(jnp.where(lane == e, pos_ref[...], 0), axis=1, keepdims=True)
        gate_e = jnp.sum(jnp.where(lane == e, gate_ref[...], 0.0), axis=1, keepdims=True)
        row = pd[k] * MOE_BLK + lax.broadcasted_iota(jnp.int32, (1, MOE_BLK), 1)
        hit = pos_e == row
        got = jnp.dot(jnp.where(hit, 1.0, 0.0).astype(BF16), ys_ref[...], preferred_element_type=F32)
        acc_scr[...] += gate_e * got

    @pl.when(plast[k] == 1)
    def _():
        xo = x_ref[...] + mod_ref[5:6, :] * acc_scr[...]
        if final_norm:
            xo = xo * lax.rsqrt(jnp.mean(xo * xo, axis=-1, keepdims=True) + EPS) * fnw_ref[...]
        xo_ref[...] = xo


def _combine(ys, plan, x, mod, fnw, ts, final_norm):
    n = x.shape[0]
    tiles_per_batch = n // (mod.shape[0] * ts)
    sp = lambda fn: (lambda k, pj, pd, pe, pf, pl_, pv: fn(k, pj, pd))
    grid_spec = pltpu.PrefetchScalarGridSpec(
        num_scalar_prefetch=6,
        grid=(plan['max_pairs'],),
        in_specs=[pl.BlockSpec((MOE_BLK, D_MODEL), sp(lambda k, pj, pd: (pd[k], 0))),
                  pl.BlockSpec((ts, N_EXPERTS), sp(lambda k, pj, pd: (pj[k], 0))),
                  pl.BlockSpec((ts, N_EXPERTS), sp(lambda k, pj, pd: (pj[k], 0))),
                  pl.BlockSpec((ts, D_MODEL), sp(lambda k, pj, pd: (pj[k], 0))),
                  pl.BlockSpec((None, 8, D_MODEL), sp(lambda k, pj, pd: (pj[k] // tiles_per_batch, 0, 0))),
                  pl.BlockSpec((1, D_MODEL), sp(lambda k, pj, pd: (0, 0)))],
        out_specs=pl.BlockSpec((ts, D_MODEL), sp(lambda k, pj, pd: (pj[k], 0))),
        scratch_shapes=[pltpu.VMEM((ts, D_MODEL), F32)],
    )
    return pl.pallas_call(
        functools.partial(_combine_kernel, ts=ts, final_norm=final_norm),
        grid_spec=grid_spec,
        out_shape=jax.ShapeDtypeStruct((n, D_MODEL), F32),
        compiler_params=_cparams(("arbitrary",)),
        name="moe_combine",
    )(*plan['comb'], ys, plan['pos'], plan['gate'], x, mod, fnw)


def _moe(x, mod, nw, wr, br, w1, w3, w2, fnw, final_norm):
    b, t, _ = x.shape
    ts = _tile(t, 512)
    h, g = _router(x, mod, nw, wr, br)
    plan = _moe_plan(g.reshape(b * t, LANE), ts)
    xs = _dispatch(h.reshape(b * t, D_MODEL), plan, ts)
    ys = _experts(xs, plan, w1, w3, w2)
    out = _combine(ys, plan, x.reshape(b * t, D_MODEL), mod, fnw, ts, final_norm)
    return out.reshape(b, t, D_MODEL)


def _pad_lanes(v, fill=0.0):
    return jnp.pad(v.astype(F32), (0, LANE - v.shape[0]), constant_values=fill)[None, :]


def kernel(x, c, ctx, c_ctx, w_mod, b_mod, norm_w, w_in, s5_lam_re, s5_lam_im, s5_log_dt, s5_b_re, s5_b_im, s5_c_re, s5_c_im, s5_d, s5_w_glu, lru_conv_w, lru_conv_b, lru_w_a, lru_b_a, lru_w_x, lru_b_x, lru_lam, ssd_conv_w, ssd_conv_b, ssd_a_log, ssd_dt_bias, ssd_d, ssd_norm_w, w_br_a, w_br_b, w_br_c, w_out, ffn_w1, ffn_w3, ffn_w2, moe_w_router, moe_b_router, moe_w1, moe_w3, moe_w2, final_norm_w):
    depth = w_mod.shape[0]
    nb, t_lat, _ = x.shape
    t_ctx = ctx.shape[1]
    assert t_lat % GRID_W == 0 and t_lat % SSD_CHUNK == 0 and t_ctx % SSD_CHUNK == 0

    cond_in = jnp.zeros((8, D_MODEL), F32).at[:nb].set(c).at[nb].set(c_ctx)
    mods = _modulation(cond_in, w_mod, b_mod).reshape(depth, 8, N_MOD, D_MODEL)
    mods = jnp.pad(mods, ((0, 0), (0, 0), (0, 8 - N_MOD), (0, 0)))

    n_lev_lat = max(1, math.ceil(math.log2(t_lat // S5_CHUNK)))

    x_lat, x_ctx = x, ctx
    for l in range(depth):
        ctx_out = l < depth - 1
        last = l == depth - 1
        mod_lat = mods[l, :nb]
        mod_ctx = jnp.broadcast_to(mods[l, nb][None], (nb, 8, D_MODEL))
        nw0 = norm_w[l, 0][None, :]
        nw1 = norm_w[l, 1][None, :]

        wl = w_in[l]
        o_lx = S5_WIDTH
        o_lg = o_lx + LRU_WIDTH
        o_z = o_lg + LRU_WIDTH
        o_xbc = o_z + SSD_INNER
        o_dt = o_xbc + SSD_XBC
        o_g = o_dt + SSD_HEADS
        w_main = jnp.concatenate(
            [wl[:, :o_lx], jnp.zeros((D_MODEL, P_LX - S5_WIDTH), wl.dtype), wl[:, o_lx:o_dt], wl[:, o_g:]],
            axis=1).astype(BF16)
        w_dt = jnp.pad(wl[:, o_dt:o_g], ((0, 0), (0, LANE - SSD_HEADS))).astype(BF16)
        tz, pm, rm, atab = _s5_setup(s5_lam_re[l], s5_lam_im[l], s5_log_dt[l], s5_b_re[l], s5_b_im[l],
                                     s5_c_re[l], s5_c_im[l], n_lev_lat)
        wg, bg = _lru_gate_weights(lru_w_a[l], lru_b_a[l], lru_w_x[l], lru_b_x[l])
        lcw = lru_conv_w[l].astype(F32)
        lcb = lru_conv_b[l][None, :].astype(F32)
        scw = ssd_conv_w[l].astype(F32)
        scb = ssd_conv_b[l][None, :].astype(F32)
        a_log = jnp.concatenate([_pad_lanes(ssd_a_log[l, 0]), _pad_lanes(ssd_a_log[l, 1])], axis=0)
        dt_bias = jnp.concatenate([_pad_lanes(ssd_dt_bias[l, 0]), _pad_lanes(ssd_dt_bias[l, 1])], axis=0)
        dskip_x = jnp.repeat(ssd_d[l].astype(F32), SSD_HEAD_DIM)[None, :]

        def mixers(p, dt, t, w, s5_h0, lru_h0, ssd_h0f, ssd_h0b):
            y5, s5_ht = _s5_scan(_s5_chunks(p, nb, t), tz, pm, rm, atab, s5_h0, nb)
            ya = _s5_unchunk(y5, nb, t)
            yb, lru_ht = _lru_scan(p, lcw, lcb, wg, bg, lru_lam[l].astype(F32), lru_h0, w,
                                   1 if w > 1 else LRU_WIDTH // LRU_SLAB)
            yc, ssd_htf, ssd_htb = _ssd_scan(p, dt, scw, scb, a_log, dt_bias, dskip_x, ssd_h0f, ssd_h0b)
            return ya, yb, yc, s5_ht, lru_ht, ssd_htf, ssd_htb

        p_ctx, dt_ctx = _inproj(x_ctx, mod_ctx, nw0, w_main, w_dt)
        zs5 = jnp.zeros((S5_GROUPS, nb, 4 * S5_STATE), F32)
        zlru = jnp.zeros((nb, 2, LRU_WIDTH), F32)
        zssd = jnp.zeros((nb, SSD_GROUPS * SSD_STATE, SSD_GW), F32)
        ya_c, yb_c, yc_c, s5_h, lru_h, ssd_hf, ssd_hb = mixers(p_ctx, dt_ctx, t_ctx, 1, zs5, zlru, zssd, zssd)

        p_lat, dt_lat = _inproj(x_lat, mod_lat, nw0, w_main, w_dt)
        ya, yb, yc, _, _, _, _ = mixers(p_lat, dt_lat, t_lat, GRID_W, s5_h, lru_h, ssd_hf, ssd_hb)

        mw = (s5_d[l][None, :].astype(F32), s5_w_glu[l].astype(BF16), ssd_norm_w[l][None, :].astype(F32),
              w_br_a[l].astype(BF16), w_br_b[l].astype(BF16), w_br_c[l].astype(BF16), w_out[l].astype(BF16))
        x_lat = _merge(x_lat, mod_lat, p_lat, ya, yb, yc, *mw)
        if ctx_out:
            x_ctx = _merge(x_ctx, mod_ctx, p_ctx, ya_c, yb_c, yc_c, *mw)

        fnw = final_norm_w[None, :].astype(F32)
        if l % 2 == 0:
            fw = (ffn_w1[l // 2].astype(BF16), ffn_w3[l // 2].astype(BF16), ffn_w2[l // 2].astype(BF16))
            x_lat = _ffn(x_lat, mod_lat, nw1, *fw, fnw, last)
            if ctx_out:
                x_ctx = _ffn(x_ctx, mod_ctx, nw1, *fw, fnw, False)
        else:
            wr = jnp.pad(moe_w_router[l // 2].astype(F32), ((0, 0), (0, LANE - N_EXPERTS)))
            br = _pad_lanes(moe_b_router[l // 2])
            ew = (moe_w1[l // 2].astype(BF16), moe_w3[l // 2].astype(BF16), moe_w2[l // 2].astype(BF16))
            x_lat = _moe(x_lat, mod_lat, nw1, wr, br, *ew, fnw, last)
            if ctx_out:
                x_ctx = _moe(x_ctx, mod_ctx, nw1, wr, br, *ew, fnw, False)
    return x_lat
```

```python
import functools
import math

import jax
import jax.numpy as jnp
from jax import lax
from jax.experimental import pallas as pl
from jax.experimental.pallas import tpu as pltpu

F32 = jnp.float32
BF16 = jnp.bfloat16
HIGHEST = lax.Precision.HIGHEST

D_MODEL = 1024
GRID_W = 64
N_MOD = 6
EPS = 1e-6

S5_WIDTH = 768
S5_GROUP = 16
S5_GROUPS = S5_WIDTH // S5_GROUP
S5_STATE = 64
S5_CHUNK = 32
S5_CW = S5_CHUNK * S5_GROUP

LRU_WIDTH = 1024
LRU_BLOCKS = 16
LRU_BLOCK = LRU_WIDTH // LRU_BLOCKS
LRU_CONV = 4
LRU_C = 8.0
LRU_SLAB = 128

SSD_INNER = 1024
SSD_HEAD_DIM = 64
SSD_HEADS = SSD_INNER // SSD_HEAD_DIM
SSD_GROUPS = 4
SSD_HPG = SSD_HEADS // SSD_GROUPS
SSD_STATE = 128
SSD_CONV = 4
SSD_CHUNK = 128
SSD_XBC = SSD_INNER + 2 * SSD_GROUPS * SSD_STATE
SSD_GW = SSD_HPG * SSD_HEAD_DIM

N_EXPERTS = 8
TOP_K = 2
LANE = 128

P_U, P_LX, P_LG, P_Z, P_XBC, P_GATE = 0, 1024, 2048, 3072, 4096, 6144
P_TOTAL = 9216

VMEM_LIMIT = 56 * 1024 * 1024


def _cparams(sem):
    return pltpu.CompilerParams(dimension_semantics=sem, vmem_limit_bytes=VMEM_LIMIT)


def _tile(n, pref):
    t = min(n, pref)
    while n % t:
        t //= 2
    return t


def _sigmoid(x):
    return jax.nn.sigmoid(x)


def _sigmoid_t(x):
    return 0.5 * jnp.tanh(0.5 * x) + 0.5


def _silu(x):
    return x * jax.nn.sigmoid(x)


def _softplus(x):
    return jnp.maximum(x, 0.0) + jnp.log(1.0 + jnp.exp(-jnp.abs(x)))


def _split_dot(x, w_bf16):
    hi = x.astype(BF16)
    lo = (x - hi.astype(F32)).astype(BF16)
    return (jnp.dot(hi, w_bf16, preferred_element_type=F32)
            + jnp.dot(lo, w_bf16, preferred_element_type=F32))


def _rms_mod(x, nw, shift, scale):
    y = x * lax.rsqrt(jnp.mean(x * x, axis=-1, keepdims=True) + EPS) * nw
    return y * (1.0 + scale) + shift


def _mod_kernel(c_ref, w_ref, b_ref, o_ref):
    cond = _silu(c_ref[...])
    o_ref[...] = jnp.dot(cond, w_ref[...], preferred_element_type=F32, precision=HIGHEST) + b_ref[...]


def _modulation(cond_in, w_mod, b_mod):
    depth = w_mod.shape[0]
    n = w_mod.shape[2]
    tn = 1024
    return pl.pallas_call(
        _mod_kernel,
        grid=(depth, n // tn),
        in_specs=[pl.BlockSpec((8, D_MODEL), lambda l, j: (0, 0)),
                  pl.BlockSpec((None, D_MODEL, tn), lambda l, j: (l, 0, j)),
                  pl.BlockSpec((None, 1, tn), lambda l, j: (l, 0, j))],
        out_specs=pl.BlockSpec((None, 8, tn), lambda l, j: (l, 0, j)),
        out_shape=jax.ShapeDtypeStruct((depth, 8, n), F32),
        compiler_params=_cparams(("arbitrary", "arbitrary")),
        name="modulation",
    )(cond_in, w_mod, b_mod.reshape(depth, 1, n))


def _inproj_kernel(x_ref, mod_ref, nw_ref, w_ref, wdt_ref, p_ref, dt_ref, h_scr):
    @pl.when(pl.program_id(2) == 0)
    def _():
        h = _rms_mod(x_ref[...], nw_ref[...], mod_ref[0:1, :], mod_ref[1:2, :]).astype(BF16)
        h_scr[...] = h
        dt_ref[...] = jnp.dot(h, wdt_ref[...], preferred_element_type=F32)

    p_ref[...] = jnp.dot(h_scr[...], w_ref[...], preferred_element_type=F32).astype(p_ref.dtype)


def _inproj(x, mod, nw, w, wdt):
    b, t, _ = x.shape
    tm = _tile(t, 1024)
    tn = 1536
    return pl.pallas_call(
        _inproj_kernel,
        grid=(b, t // tm, P_TOTAL // tn),
        in_specs=[pl.BlockSpec((None, tm, D_MODEL), lambda b_, i, j: (b_, i, 0)),
                  pl.BlockSpec((None, 8, D_MODEL), lambda b_, i, j: (b_, 0, 0)),
                  pl.BlockSpec((1, D_MODEL), lambda b_, i, j: (0, 0)),
                  pl.BlockSpec((D_MODEL, tn), lambda b_, i, j: (0, j)),
                  pl.BlockSpec((D_MODEL, LANE), lambda b_, i, j: (0, 0))],
        out_specs=[pl.BlockSpec((None, tm, tn), lambda b_, i, j: (b_, i, j)),
                   pl.BlockSpec((None, tm, LANE), lambda b_, i, j: (b_, i, 0))],
        out_shape=[jax.ShapeDtypeStruct((b, t, P_TOTAL), BF16),
                   jax.ShapeDtypeStruct((b, t, LANE), F32)],
        scratch_shapes=[pltpu.VMEM((tm, D_MODEL), BF16)],
        compiler_params=_cparams(("arbitrary", "arbitrary", "arbitrary")),
        name="inproj",
    )(x, mod, nw, w, wdt)


def _s5_setup(lam_re, lam_im, log_dt, b_re, b_im, c_re, c_im, n_levels):
    t = S5_CHUNK
    dt = jnp.exp(log_dt.astype(F32))[..., None]
    lre, lim = lam_re.astype(F32), lam_im.astype(F32)
    are, aim = lre * dt, lim * dt
    k = jnp.arange(t + 1, dtype=F32)[:, None]
    mag = jnp.exp(are[:, :, None, :] * k)
    pw_re = mag * jnp.cos(aim[:, :, None, :] * k)
    pw_im = mag * jnp.sin(aim[:, :, None, :] * k)
    a_re, a_im = pw_re[:, :, 1], pw_im[:, :, 1]
    den = lre * lre + lim * lim
    q_re = ((a_re - 1.0) * lre + a_im * lim) / den
    q_im = (a_im * lre - (a_re - 1.0) * lim) / den
    bb_re = q_re[..., None] * b_re - q_im[..., None] * b_im
    bb_im = q_re[..., None] * b_im + q_im[..., None] * b_re
    cr, ci = c_re.astype(F32), c_im.astype(F32)

    cb_re = cr[..., None] * bb_re[:, :, None] - ci[..., None] * bb_im[:, :, None]
    cb_im = cr[..., None] * bb_im[:, :, None] + ci[..., None] * bb_re[:, :, None]
    kern = (jnp.einsum('dgkp,dgjpi->dgkji', pw_re[:, :, :t], cb_re, precision=HIGHEST)
            - jnp.einsum('dgkp,dgjpi->dgkji', pw_im[:, :, :t], cb_im, precision=HIGHEST))
    kf, kb = kern[0], kern[1]
    taps = jnp.concatenate([kb[:, 1:][:, ::-1], (kf[:, 0] + kb[:, 0])[:, None], kf[:, 1:]], axis=1)
    kcat = taps.transpose(0, 3, 1, 2).reshape(S5_GROUPS, S5_GROUP, (2 * t - 1) * S5_GROUP)
    kcat = jnp.pad(kcat, ((0, 0), (0, 0), (0, S5_GROUP)))

    pf_re, pf_im = pw_re[0, :, :t][:, ::-1], pw_im[0, :, :t][:, ::-1]
    pb_re, pb_im = pw_re[1, :, :t], pw_im[1, :, :t]

    def in_map(p_re, p_im, d):
        w_re = p_re[:, :, None, :] * bb_re[d].transpose(0, 2, 1)[:, None] - p_im[:, :, None, :] * bb_im[d].transpose(0, 2, 1)[:, None]
        w_im = p_re[:, :, None, :] * bb_im[d].transpose(0, 2, 1)[:, None] + p_im[:, :, None, :] * bb_re[d].transpose(0, 2, 1)[:, None]
        return w_re, w_im

    wf_re, wf_im = in_map(pf_re, pf_im, 0)
    wb_re, wb_im = in_map(pb_re, pb_im, 1)
    pm = jnp.concatenate([wf_re, wf_im, wb_re, wb_im], axis=-1).reshape(S5_GROUPS, S5_CW, 4 * S5_STATE)

    def out_map(p_re, p_im, d):
        c_r = cr[d].transpose(0, 2, 1)[:, :, None, :]
        c_i = ci[d].transpose(0, 2, 1)[:, :, None, :]
        e_r = p_re.transpose(0, 2, 1)[..., None]
        e_i = p_im.transpose(0, 2, 1)[..., None]
        m_re = c_r * e_r - c_i * e_i
        m_im = c_r * e_i + c_i * e_r
        return m_re, -m_im

    rf_re, rf_im = out_map(pw_re[0, :, 1:t + 1], pw_im[0, :, 1:t + 1], 0)
    rb_re, rb_im = out_map(pw_re[1, :, 1:t + 1][:, ::-1], pw_im[1, :, 1:t + 1][:, ::-1], 1)
    rm = jnp.concatenate([rf_re, rf_im, rb_re, rb_im], axis=1).reshape(S5_GROUPS, 4 * S5_STATE, S5_CW)

    lev = (t * 2.0 ** jnp.arange(n_levels, dtype=F32))[:, None]
    lmag = jnp.exp(are[:, :, None, :] * lev)
    l_re = lmag * jnp.cos(aim[:, :, None, :] * lev)
    l_im = lmag * jnp.sin(aim[:, :, None, :] * lev)
    row_a = jnp.concatenate([l_re[0], l_re[0], l_re[1], l_re[1]], axis=-1)
    row_b = jnp.concatenate([-l_im[0], l_im[0], -l_im[1], l_im[1]], axis=-1)
    atab = jnp.stack([row_a, row_b], axis=2)
    return kcat, pm.astype(BF16), rm.astype(BF16), atab


def _shift_rows(h, d, down):
    n = h.shape[0]
    if d >= n:
        return jnp.zeros_like(h)
    row = lax.broadcasted_iota(jnp.int32, h.shape, 0)
    if down:
        return jnp.where(row >= d, pltpu.roll(h, d, 0), 0.0)
    return jnp.where(row < n - d, pltpu.roll(h, n - d, 0), 0.0)


S5_SLAB_GROUPS = LANE // S5_GROUP
S5_PIECES = S5_CW // LANE
S5_PER_PIECE = LANE // S5_GROUP


def _s5_kernel(u_ref, perm_ref, kc_ref, pm_ref, rm_ref, at_ref, h0_ref, y_ref, ht_ref,
               x_scr, yg_scr, tz_scr, hin_scr, *, nb, nc, n_levels):
    half = 2 * S5_STATE
    row = lax.broadcasted_iota(jnp.int32, (nc, half), 0)

    for v in range(S5_PIECES):
        a = jnp.concatenate(
            [jnp.concatenate([u_ref[b, S5_PER_PIECE * v + sl] for b in range(nb)], axis=0)
             for sl in range(S5_PER_PIECE)], axis=1)
        xp = jnp.dot(a.astype(BF16), perm_ref[...], preferred_element_type=F32).astype(BF16)
        for gl in range(S5_SLAB_GROUPS):
            x_scr[gl, :, v * LANE:(v + 1) * LANE] = xp[:, gl * LANE:(gl + 1) * LANE]

    def group(g, _):
        kc = kc_ref[g]
        for s in range(S5_CHUNK):
            off = (S5_CHUNK - 1 - s) * S5_GROUP
            tz_scr[s * S5_GROUP:(s + 1) * S5_GROUP, :] = kc[:, off:off + S5_CW].astype(BF16)
        x = x_scr[g]
        s_all = jnp.dot(x, pm_ref[g], preferred_element_type=F32)

        def cmul(h, k, lo):
            a = at_ref[g, k, 0:1, lo:lo + half]
            b = at_ref[g, k, 1:2, lo:lo + half]
            return a * h + b * pltpu.roll(h, S5_STATE, 1)

        for b in range(nb):
            for d in range(2):
                lo = d * half
                h = s_all[b * nc:(b + 1) * nc, lo:lo + half]
                h0 = h0_ref[g, b:b + 1, lo:lo + half]
                edge = 0 if d == 0 else nc - 1
                h = h + jnp.where(row == edge, cmul(jnp.broadcast_to(h0, (nc, half)), 0, lo), 0.0)
                for k in range(n_levels):
                    if (1 << k) < nc:
                        h = h + cmul(_shift_rows(h, 1 << k, d == 0), k, lo)
                ht_ref[g, b:b + 1, lo:lo + half] = h[nc - 1 - edge:nc - edge]
                hin = _shift_rows(h, 1, d == 0)
                hin = jnp.where(row == edge, jnp.broadcast_to(h0, (nc, half)), hin)
                hin_scr[b * nc:(b + 1) * nc, lo:lo + half] = hin

        y = jnp.dot(x, tz_scr[...], preferred_element_type=F32)
        y = y + jnp.dot(hin_scr[...].astype(BF16), rm_ref[g], preferred_element_type=F32)
        yg_scr[g] = y.astype(BF16)
        return 0
    lax.fori_loop(0, S5_SLAB_GROUPS, group, 0)

    for v in range(S5_PIECES):
        cat = jnp.concatenate([yg_scr[gl, :, v * LANE:(v + 1) * LANE] for gl in range(S5_SLAB_GROUPS)], axis=1)
        yp = jnp.dot(cat, perm_ref[...], preferred_element_type=F32)
        for tl in range(S5_PER_PIECE):
            for b in range(nb):
                y_ref[b, S5_PER_PIECE * v + tl] = yp[b * nc:(b + 1) * nc, tl * LANE:(tl + 1) * LANE].astype(y_ref.dtype)


def _s5_scan(u, perm, kcat, pm, rm, atab, h0):
    nb, _, nc, _ = u.shape
    rows = nb * nc
    n_levels = atab.shape[1]
    gs = S5_SLAB_GROUPS
    kern = functools.partial(_s5_kernel, nb=nb, nc=nc, n_levels=n_levels)
    tok_spec = pl.BlockSpec((nb, S5_CHUNK, nc, LANE), lambda i: (0, 0, 0, i))
    return pl.pallas_call(
        kern,
        grid=(S5_GROUPS // gs,),
        in_specs=[tok_spec,
                  pl.BlockSpec((gs * LANE, gs * LANE), lambda i: (0, 0)),
                  pl.BlockSpec((gs, S5_GROUP, 2 * S5_CW), lambda i: (i, 0, 0)),
                  pl.BlockSpec((gs, S5_CW, 4 * S5_STATE), lambda i: (i, 0, 0)),
                  pl.BlockSpec((gs, 4 * S5_STATE, S5_CW), lambda i: (i, 0, 0)),
                  pl.BlockSpec((gs, n_levels, 2, 4 * S5_STATE), lambda i: (i, 0, 0, 0)),
                  pl.BlockSpec((gs, nb, 4 * S5_STATE), lambda i: (i, 0, 0))],
        out_specs=[tok_spec,
                   pl.BlockSpec((gs, nb, 4 * S5_STATE), lambda i: (i, 0, 0))],
        out_shape=[jax.ShapeDtypeStruct(u.shape, u.dtype),
                   jax.ShapeDtypeStruct((S5_GROUPS, nb, 4 * S5_STATE), F32)],
        scratch_shapes=[pltpu.VMEM((gs, rows, S5_CW), BF16), pltpu.VMEM((gs, rows, S5_CW), BF16),
                        pltpu.VMEM((S5_CW, S5_CW), BF16), pltpu.VMEM((rows, 4 * S5_STATE), F32)],
        compiler_params=_cparams(("arbitrary",)),
        name="s5_scan",
    )(u, perm, kcat, pm, rm, atab, h0)


def _s5_chunks(p, nb, t):
    nc = t // S5_CHUNK
    u = p[:, :, P_U:P_U + S5_WIDTH].reshape(nb, nc, S5_CHUNK, S5_WIDTH).transpose(0, 2, 1, 3)
    return u if nc % 16 == 0 else u.astype(F32)


def _s5_unchunk(y, nb, t):
    return y.transpose(0, 2, 1, 3).reshape(nb, t, S5_WIDTH)


def _s5_perm():
    a = jnp.arange(S5_SLAB_GROUPS * LANE)
    s, g, i = a // LANE, (a % LANE) // S5_GROUP, a % S5_GROUP
    dst = g * LANE + s * S5_GROUP + i
    return (dst[:, None] == a[None, :]).astype(BF16)


def _lru_kernel(x_ref, cw_ref, cb_ref, wg_ref, bg_ref, lam_ref, h0_ref, out_ref, ht_ref,
                xp_scr, xc_scr, af_scr, bf_scr, ab_scr, bb_scr, cf_scr, cr_scr, *, w, s, pad_top, ch, ns):
    l = w * s
    n_ch = l // ch
    slabs = range(ns)
    lanes = [slice(k * LRU_SLAB, (k + 1) * LRU_SLAB) for k in slabs]
    zero_slab = jnp.zeros((w, LRU_SLAB), F32)
    one_slab = jnp.ones((w, LRU_SLAB), F32)

    for k in slabs:
        xp_scr[k, 0:pad_top, :] = jnp.zeros((pad_top, LRU_SLAB), F32)
        xp_scr[k, pad_top + l:pad_top + l + pad_top, :] = jnp.zeros((pad_top, LRU_SLAB), F32)

    def copy_body(i, _):
        r0 = pl.multiple_of(i * ch, ch)
        for k in slabs:
            xp_scr[k, pl.ds(pad_top + r0, ch), :] = x_ref[pl.ds(r0, ch), lanes[k]].astype(F32)
        return 0
    lax.fori_loop(0, n_ch, copy_body, 0)

    def conv_body(i, _):
        r0 = pl.multiple_of(i * ch, ch)
        for k in slabs:
            acc = cb_ref[:, lanes[k]] + cw_ref[2:3, lanes[k]] * xp_scr[k, pl.ds(pad_top + r0, ch), :]
            acc = acc + cw_ref[0:1, lanes[k]] * xp_scr[k, pl.ds(pad_top + r0 - 2 * w, ch), :]
            acc = acc + cw_ref[1:2, lanes[k]] * xp_scr[k, pl.ds(pad_top + r0 - w, ch), :]
            acc = acc + cw_ref[3:4, lanes[k]] * xp_scr[k, pl.ds(pad_top + r0 + w, ch), :]
            xc_scr[k, pl.ds(r0, ch), :] = acc
        return 0
    lax.fori_loop(0, n_ch, conv_body, 0)

    if w > 1:
        def prev_col(v):
            return _shift_rows(v, 1, True)

        def next_col(v):
            return _shift_rows(v, 1, False)

        for k in slabs:
            x_last = xp_scr[k, pad_top + (s - 1) * w:pad_top + s * w, :]
            x_last2 = xp_scr[k, pad_top + (s - 2) * w:pad_top + (s - 1) * w, :]
            x_first = xp_scr[k, pad_top:pad_top + w, :]
            w0, w1, w3 = cw_ref[0:1, lanes[k]], cw_ref[1:2, lanes[k]], cw_ref[3:4, lanes[k]]
            xc_scr[k, 0:w, :] = xc_scr[k, 0:w, :] + w0 * prev_col(x_last2) + w1 * prev_col(x_last)
            xc_scr[k, w:2 * w, :] = xc_scr[k, w:2 * w, :] + w0 * prev_col(x_last)
            xc_scr[k, (s - 1) * w:s * w, :] = xc_scr[k, (s - 1) * w:s * w, :] + w3 * next_col(x_first)

    log_sig = -_softplus(-lam_ref[...])

    def coef_body(i, _):
        r0 = pl.multiple_of(i * ch, ch)
        for k in slabs:
            xc = xc_scr[k, pl.ds(r0, ch), :]
            g = jnp.dot(xc.astype(BF16), wg_ref[k], preferred_element_type=F32) + bg_ref[k]
            for d, (a_scr, b_scr) in enumerate(((af_scr, bf_scr), (ab_scr, bb_scr))):
                rg = _sigmoid_t(g[:, (2 * d) * LRU_SLAB:(2 * d + 1) * LRU_SLAB])
                ig = _sigmoid_t(g[:, (2 * d + 1) * LRU_SLAB:(2 * d + 2) * LRU_SLAB])
                a = jnp.exp(LRU_C * rg * log_sig[d:d + 1, lanes[k]])
                a_scr[k, pl.ds(r0, ch), :] = a
                b_scr[k, pl.ds(r0, ch), :] = jnp.sqrt(1.0 - a * a) * (ig * xc)
        return 0
    lax.fori_loop(0, n_ch, coef_body, 0)

    def scan_body(r, carry):
        rf = pl.multiple_of(r * w, w)
        rb = pl.multiple_of((s - 1 - r) * w, w)
        out = []
        for k in slabs:
            hf, pf, hb, pb = carry[k]
            a = af_scr[k, pl.ds(rf, w), :]
            hf = a * hf + bf_scr[k, pl.ds(rf, w), :]
            pf = a * pf
            bf_scr[k, pl.ds(rf, w), :] = hf
            af_scr[k, pl.ds(rf, w), :] = pf
            a = ab_scr[k, pl.ds(rb, w), :]
            hb = a * hb + bb_scr[k, pl.ds(rb, w), :]
            pb = a * pb
            bb_scr[k, pl.ds(rb, w), :] = hb
            ab_scr[k, pl.ds(rb, w), :] = pb
            out.append((hf, pf, hb, pb))
        return tuple(out)
    lax.fori_loop(0, s, scan_body, tuple((zero_slab, one_slab, zero_slab, one_slab) for _ in slabs))

    def carry_f(c, carry):
        out = []
        for k in slabs:
            cf_scr[k, pl.ds(c, 1), :] = carry[k]
            out.append(bf_scr[k, pl.ds((s - 1) * w + c, 1), :] + af_scr[k, pl.ds((s - 1) * w + c, 1), :] * carry[k])
        return tuple(out)
    fin = lax.fori_loop(0, w, carry_f, tuple(h0_ref[0:1, lanes[k]] for k in slabs))
    for k in slabs:
        ht_ref[0:1, lanes[k]] = fin[k]

    def carry_b(i, carry):
        c = w - 1 - i
        out = []
        for k in slabs:
            cr_scr[k, pl.ds(c, 1), :] = carry[k]
            out.append(bb_scr[k, pl.ds(c, 1), :] + ab_scr[k, pl.ds(c, 1), :] * carry[k])
        return tuple(out)
    fin = lax.fori_loop(0, w, carry_b, tuple(h0_ref[1:2, lanes[k]] for k in slabs))
    for k in slabs:
        ht_ref[1:2, lanes[k]] = fin[k]

    def out_body(r, _):
        r0 = pl.multiple_of(r * w, w)
        for k in slabs:
            y = (bf_scr[k, pl.ds(r0, w), :] + af_scr[k, pl.ds(r0, w), :] * cf_scr[k]
                 + bb_scr[k, pl.ds(r0, w), :] + ab_scr[k, pl.ds(r0, w), :] * cr_scr[k])
            bf_scr[k, pl.ds(r0, w), :] = y
        return 0
    lax.fori_loop(0, s, out_body, 0)

    def store_body(i, _):
        r0 = pl.multiple_of(i * ch, ch)
        for k in slabs:
            out_ref[pl.ds(r0, ch), lanes[k]] = bf_scr[k, pl.ds(r0, ch), :].astype(out_ref.dtype)
        return 0
    lax.fori_loop(0, n_ch, store_body, 0)


def _lru_scan(p, cw, cb, wg, bg, lam, h0, w, ns):
    b, t, _ = p.shape
    s = t // w
    assert s * w == t and s >= 4
    ch = _tile(t, 256)
    pad_top = max(2 * w, 8)
    cwid = ns * LRU_SLAB
    col0 = P_LX // cwid
    kern = functools.partial(_lru_kernel, w=w, s=s, pad_top=pad_top, ch=ch, ns=ns)
    return pl.pallas_call(
        kern,
        grid=(b, LRU_WIDTH // cwid),
        in_specs=[pl.BlockSpec((None, t, cwid), lambda b_, k: (b_, 0, col0 + k)),
                  pl.BlockSpec((LRU_CONV, cwid), lambda b_, k: (0, k)),
                  pl.BlockSpec((1, cwid), lambda b_, k: (0, k)),
                  pl.BlockSpec((ns, LRU_SLAB, 4 * LRU_SLAB), lambda b_, k: (k, 0, 0)),
                  pl.BlockSpec((ns, 1, 4 * LRU_SLAB), lambda b_, k: (k, 0, 0)),
                  pl.BlockSpec((2, cwid), lambda b_, k: (0, k)),
                  pl.BlockSpec((None, 2, cwid), lambda b_, k: (b_, 0, k))],
        out_specs=[pl.BlockSpec((None, t, cwid), lambda b_, k: (b_, 0, k)),
                   pl.BlockSpec((None, 2, cwid), lambda b_, k: (b_, 0, k))],
        out_shape=[jax.ShapeDtypeStruct((b, t, LRU_WIDTH), BF16),
                   jax.ShapeDtypeStruct((b, 2, LRU_WIDTH), F32)],
        scratch_shapes=[pltpu.VMEM((ns, t + 2 * pad_top, LRU_SLAB), F32),
                        pltpu.VMEM((ns, t, LRU_SLAB), F32),
                        pltpu.VMEM((ns, t, LRU_SLAB), F32), pltpu.VMEM((ns, t, LRU_SLAB), F32),
                        pltpu.VMEM((ns, t, LRU_SLAB), F32), pltpu.VMEM((ns, t, LRU_SLAB), F32),
                        pltpu.VMEM((ns, w, LRU_SLAB), F32), pltpu.VMEM((ns, w, LRU_SLAB), F32)],
        compiler_params=_cparams(("arbitrary", "arbitrary")),
        name="lru_scan",
    )(p, cw, cb, wg, bg, lam, h0)


def _lru_gate_weights(w_a, b_a, w_x, b_x):
    n_slab = LRU_WIDTH // LRU_SLAB
    per = LRU_SLAB // LRU_BLOCK

    def slab_diag(wm):
        wm = wm.reshape(n_slab, per, LRU_BLOCK, LRU_BLOCK)
        eye = jnp.eye(per, dtype=wm.dtype)
        return jnp.einsum('spkj,pq->spkqj', wm, eye).reshape(n_slab, LRU_SLAB, LRU_SLAB)

    wg = jnp.concatenate([slab_diag(w_a[0]), slab_diag(w_x[0]), slab_diag(w_a[1]), slab_diag(w_x[1])], axis=2)
    bg = jnp.concatenate([b_a[0].reshape(n_slab, 1, LRU_SLAB), b_x[0].reshape(n_slab, 1, LRU_SLAB),
                          b_a[1].reshape(n_slab, 1, LRU_SLAB), b_x[1].reshape(n_slab, 1, LRU_SLAB)], axis=2)
    return wg.astype(BF16), bg.astype(F32)


def _ssd_core(xs, bm, cm, dtr, alog, dbias, e_ref, h_scr, y_scr, reverse):
    q = SSD_CHUNK
    a = -jnp.exp(alog)
    dt = _softplus(dtr + dbias)
    adt = dt * a
    row = lax.broadcasted_iota(jnp.int32, (q, q), 0)
    col = lax.broadcasted_iota(jnp.int32, (q, q), 1)
    tri = (col >= row) if reverse else (col <= row)
    cs = jnp.dot(tri.astype(F32), adt, preferred_element_type=F32, precision=HIGHEST)
    cs_t = cs.T
    dt_t = dt.T
    end = 0 if reverse else q - 1
    tot = cs[end:end + 1, :]
    e = e_ref[...]
    w_x = _split_dot(dt * jnp.exp(tot - cs), e)
    ecs_x = _split_dot(jnp.exp(cs), e)
    xsb = xs.astype(BF16)
    xw = (xs * w_x).astype(BF16)
    decay_row = ecs_x[end:end + 1, :]

    for g in range(SSD_GROUPS):
        cg = cm[:, g * SSD_STATE:(g + 1) * SSD_STATE]
        bg = bm[:, g * SSD_STATE:(g + 1) * SSD_STATE]
        sc = lax.dot_general(cg, bg, (((1,), (1,)), ((), ())), preferred_element_type=F32)
        for j in range(SSD_HPG):
            hd = g * SSD_HPG + j
            cols = slice(hd * SSD_HEAD_DIM, (hd + 1) * SSD_HEAD_DIM)
            diff = cs[:, hd:hd + 1] - cs_t[hd:hd + 1, :]
            lm = jnp.where(tri, jnp.exp(diff), 0.0) * dt_t[hd:hd + 1, :]
            y_scr[:, cols] = jnp.dot((sc * lm).astype(BF16), xsb[:, cols], preferred_element_type=F32)
        ch = slice(g * SSD_GW, (g + 1) * SSD_GW)
        rows = slice(g * SSD_STATE, (g + 1) * SSD_STATE)
        h_old = h_scr[rows, :]
        y_off = jnp.dot(cg, h_old.astype(BF16), preferred_element_type=F32) * ecs_x[:, ch]
        y_scr[:, ch] = y_scr[:, ch] + y_off
        st = lax.dot_general(bg, xw[:, ch], (((0,), (0,)), ((), ())), preferred_element_type=F32)
        h_scr[rows, :] = decay_row[:, ch] * h_old + st
    return y_scr[...]


def _ssd_fwd_kernel(x_ref, xprev_ref, xnext_ref, dt_ref, cw_ref, cb_ref, alog_ref, dbias_ref, e_ref, h0_ref,
                    y_ref, xc_ref, ht_ref, h_scr, y_scr, *, nc):
    c = pl.program_id(1)

    @pl.when(c == 0)
    def _():
        h_scr[...] = h0_ref[...]

    q = SSD_CHUNK
    x = x_ref[...].astype(F32)
    prev = jnp.where(c > 0, xprev_ref[...].astype(F32), 0.0)
    nxt = jnp.where(c < nc - 1, xnext_ref[...].astype(F32), 0.0)
    row8 = lax.broadcasted_iota(jnp.int32, (8, SSD_XBC), 0)

    def behind(k):
        xr = pltpu.roll(x, k, 0)
        top = jnp.where(row8 < k, pltpu.roll(prev, k, 0)[0:8], xr[0:8])
        return jnp.concatenate([top, xr[8:]], axis=0)

    xm2, xm1 = behind(2), behind(1)
    xr = pltpu.roll(x, q - 1, 0)
    bottom = jnp.where(row8 == 7, pltpu.roll(nxt, 15, 0)[8:16], xr[q - 8:q])
    xp1 = jnp.concatenate([xr[:q - 8], bottom], axis=0)
    conv = (cb_ref[...] + cw_ref[0:1, :] * xm2 + cw_ref[1:2, :] * xm1
            + cw_ref[2:3, :] * x + cw_ref[3:4, :] * xp1)
    xc = _silu(conv)
    xc_ref[...] = xc.astype(xc_ref.dtype)
    xs = xc[:, :SSD_INNER]
    bm = xc[:, SSD_INNER:SSD_INNER + SSD_GROUPS * SSD_STATE].astype(BF16)
    cm = xc[:, SSD_INNER + SSD_GROUPS * SSD_STATE:].astype(BF16)
    y_ref[...] = _ssd_core(xs, bm, cm, dt_ref[...], alog_ref[...], dbias_ref[...], e_ref, h_scr, y_scr, False)

    @pl.when(c == nc - 1)
    def _():
        ht_ref[...] = h_scr[...]


def _ssd_bwd_kernel(xc_ref, dt_ref, yf_ref, alog_ref, dbias_ref, dskip_ref, e_ref, h0_ref,
                    y_ref, ht_ref, h_scr, y_scr, *, nc):
    c = pl.program_id(1)

    @pl.when(c == 0)
    def _():
        h_scr[...] = h0_ref[...]

    xc = xc_ref[...]
    xs = xc[:, :SSD_INNER].astype(F32)
    bm = xc[:, SSD_INNER:SSD_INNER + SSD_GROUPS * SSD_STATE]
    cm = xc[:, SSD_INNER + SSD_GROUPS * SSD_STATE:]
    y = _ssd_core(xs, bm, cm, dt_ref[...], alog_ref[...], dbias_ref[...], e_ref, h_scr, y_scr, True)
    y_ref[...] = y + yf_ref[...] + dskip_ref[...] * xs

    @pl.when(c == nc - 1)
    def _():
        ht_ref[...] = h_scr[...]


def _ssd_scan(p, dt, cw, cb, a_log, dt_bias, dskip_x, e_mat, h0f, h0b):
    b, t, _ = p.shape
    q = SSD_CHUNK
    nc = t // q
    assert nc * q == t
    xbc_blk = P_XBC // SSD_XBC
    n16 = t // 16
    hs = (SSD_GROUPS * SSD_STATE, SSD_GW)
    const2 = lambda b_, c: (0, 0)
    state_spec = pl.BlockSpec((None,) + hs, lambda b_, c: (b_, 0, 0))

    yf, xc, htf = pl.pallas_call(
        functools.partial(_ssd_fwd_kernel, nc=nc),
        grid=(b, nc),
        in_specs=[pl.BlockSpec((None, q, SSD_XBC), lambda b_, c: (b_, c, xbc_blk)),
                  pl.BlockSpec((None, 16, SSD_XBC),
                               lambda b_, c: (b_, jnp.maximum(c * (q // 16) - 1, 0), xbc_blk)),
                  pl.BlockSpec((None, 16, SSD_XBC),
                               lambda b_, c: (b_, jnp.minimum((c + 1) * (q // 16), n16 - 1), xbc_blk)),
                  pl.BlockSpec((None, q, LANE), lambda b_, c: (b_, c, 0)),
                  pl.BlockSpec((SSD_CONV, SSD_XBC), const2),
                  pl.BlockSpec((1, SSD_XBC), const2),
                  pl.BlockSpec((1, LANE), const2),
                  pl.BlockSpec((1, LANE), const2),
                  pl.BlockSpec((LANE, SSD_INNER), const2),
                  state_spec],
        out_specs=[pl.BlockSpec((None, q, SSD_INNER), lambda b_, c: (b_, c, 0)),
                   pl.BlockSpec((None, q, SSD_XBC), lambda b_, c: (b_, c, 0)),
                   state_spec],
        out_shape=[jax.ShapeDtypeStruct((b, t, SSD_INNER), F32),
                   jax.ShapeDtypeStruct((b, t, SSD_XBC), BF16),
                   jax.ShapeDtypeStruct((b,) + hs, F32)],
        scratch_shapes=[pltpu.VMEM(hs, F32), pltpu.VMEM((q, SSD_INNER), F32)],
        compiler_params=_cparams(("arbitrary", "arbitrary")),
        name="ssd_fwd",
    )(p, p, p, dt, cw, cb, a_log[0:1], dt_bias[0:1], e_mat, h0f)

    rev = lambda b_, c: (b_, nc - 1 - c, 0)
    y, htb = pl.pallas_call(
        functools.partial(_ssd_bwd_kernel, nc=nc),
        grid=(b, nc),
        in_specs=[pl.BlockSpec((None, q, SSD_XBC), rev),
                  pl.BlockSpec((None, q, LANE), rev),
                  pl.BlockSpec((None, q, SSD_INNER), rev),
                  pl.BlockSpec((1, LANE), const2),
                  pl.BlockSpec((1, LANE), const2),
                  pl.BlockSpec((1, SSD_INNER), const2),
                  pl.BlockSpec((LANE, SSD_INNER), const2),
                  state_spec],
        out_specs=[pl.BlockSpec((None, q, SSD_INNER), rev), state_spec],
        out_shape=[jax.ShapeDtypeStruct((b, t, SSD_INNER), F32),
                   jax.ShapeDtypeStruct((b,) + hs, F32)],
        scratch_shapes=[pltpu.VMEM(hs, F32), pltpu.VMEM((q, SSD_INNER), F32)],
        compiler_params=_cparams(("arbitrary", "arbitrary")),
        name="ssd_bwd",
    )(xc, dt, yf, a_log[1:2], dt_bias[1:2], dskip_x, e_mat, h0b)
    return y, htf, htb


def _merge_kernel(x_ref, mod_ref, u_ref, ya_ref, yb_ref, g_ref, yc_ref, z_ref, gates_ref,
                  d5_ref, wglu_ref, snw_ref, wa_ref, wb_ref, wc_ref, wo_ref, xo_ref):
    ya = jax.nn.gelu(ya_ref[...].astype(F32) + d5_ref[...] * u_ref[...].astype(F32))
    ya = ya * _sigmoid(jnp.dot(ya.astype(BF16), wglu_ref[...], preferred_element_type=F32))
    yb = yb_ref[...].astype(F32) * jax.nn.gelu(g_ref[...].astype(F32))
    yc = yc_ref[...] * _silu(z_ref[...].astype(F32))
    yc = yc * lax.rsqrt(jnp.mean(yc * yc, axis=-1, keepdims=True) + EPS) * snw_ref[...]
    mix = _sigmoid(gates_ref[:, 0:D_MODEL].astype(F32)) * jnp.dot(
        ya.astype(BF16), wa_ref[...], preferred_element_type=F32)
    mix = mix + _sigmoid(gates_ref[:, D_MODEL:2 * D_MODEL].astype(F32)) * jnp.dot(
        yb.astype(BF16), wb_ref[...], preferred_element_type=F32)
    mix = mix + _sigmoid(gates_ref[:, 2 * D_MODEL:3 * D_MODEL].astype(F32)) * jnp.dot(
        yc.astype(BF16), wc_ref[...], preferred_element_type=F32)
    y = jnp.dot(mix.astype(BF16), wo_ref[...], preferred_element_type=F32)
    xo_ref[...] = x_ref[...] + mod_ref[2:3, :] * y


def _merge(x, mod, p, ya, yb, yc, d5, wglu, snw, wa, wb, wc, wo):
    b, t, _ = x.shape
    tm = _tile(t, 512)
    tok = lambda width, blk: pl.BlockSpec((None, tm, width), lambda b_, i: (b_, i, blk))
    full = lambda r, c_: pl.BlockSpec((r, c_), lambda b_, i: (0, 0))
    return pl.pallas_call(
        _merge_kernel,
        grid=(b, t // tm),
        in_specs=[tok(D_MODEL, 0),
                  pl.BlockSpec((None, 8, D_MODEL), lambda b_, i: (b_, 0, 0)),
                  tok(S5_WIDTH, P_U // S5_WIDTH),
                  tok(S5_WIDTH, 0),
                  tok(LRU_WIDTH, 0),
                  tok(LRU_WIDTH, P_LG // LRU_WIDTH),
                  tok(SSD_INNER, 0),
                  tok(SSD_INNER, P_Z // SSD_INNER),
                  tok(3 * D_MODEL, P_GATE // (3 * D_MODEL)),
                  full(1, S5_WIDTH), full(S5_WIDTH, S5_WIDTH), full(1, SSD_INNER),
                  full(S5_WIDTH, D_MODEL), full(LRU_WIDTH, D_MODEL), full(SSD_INNER, D_MODEL),
                  full(D_MODEL, D_MODEL)],
        out_specs=tok(D_MODEL, 0),
        out_shape=jax.ShapeDtypeStruct((b, t, D_MODEL), F32),
        compiler_params=_cparams(("arbitrary", "arbitrary")),
        name="merge",
    )(x, mod, p, ya, yb, p, yc, p, p, d5, wglu, snw, wa, wb, wc, wo)


def _ffn_kernel(x_ref, mod_ref, nw_ref, w1_ref, w3_ref, w2_ref, fnw_ref, xo_ref, h_scr, acc_scr, *, nf, final_norm):
    f = pl.program_id(2)

    @pl.when(f == 0)
    def _():
        h_scr[...] = _rms_mod(x_ref[...], nw_ref[...], mod_ref[3:4, :], mod_ref[4:5, :]).astype(BF16)
        acc_scr[...] = jnp.zeros_like(acc_scr)

    h = h_scr[...]
    a = jnp.dot(h, w1_ref[...], preferred_element_type=F32)
    g = jnp.dot(h, w3_ref[...], preferred_element_type=F32)
    acc_scr[...] += jnp.dot((_silu(a) * g).astype(BF16), w2_ref[...], preferred_element_type=F32)

    @pl.when(f == nf - 1)
    def _():
        xo = x_ref[...] + mod_ref[5:6, :] * acc_scr[...]
        if final_norm:
            xo = xo * lax.rsqrt(jnp.mean(xo * xo, axis=-1, keepdims=True) + EPS) * fnw_ref[...]
        xo_ref[...] = xo


def _ffn(x, mod, nw, w1, w3, w2, fnw, final_norm):
    b, t, _ = x.shape
    dff = w1.shape[1]
    tm = _tile(t, 512)
    tf = dff // 2
    nf = dff // tf
    return pl.pallas_call(
        functools.partial(_ffn_kernel, nf=nf, final_norm=final_norm),
        grid=(b, t // tm, nf),
        in_specs=[pl.BlockSpec((None, tm, D_MODEL), lambda b_, i, f: (b_, i, 0)),
                  pl.BlockSpec((None, 8, D_MODEL), lambda b_, i, f: (b_, 0, 0)),
                  pl.BlockSpec((1, D_MODEL), lambda b_, i, f: (0, 0)),
                  pl.BlockSpec((D_MODEL, tf), lambda b_, i, f: (0, f)),
                  pl.BlockSpec((D_MODEL, tf), lambda b_, i, f: (0, f)),
                  pl.BlockSpec((tf, D_MODEL), lambda b_, i, f: (f, 0)),
                  pl.BlockSpec((1, D_MODEL), lambda b_, i, f: (0, 0))],
        out_specs=pl.BlockSpec((None, tm, D_MODEL), lambda b_, i, f: (b_, i, 0)),
        out_shape=jax.ShapeDtypeStruct((b, t, D_MODEL), F32),
        scratch_shapes=[pltpu.VMEM((tm, D_MODEL), BF16), pltpu.VMEM((tm, D_MODEL), F32)],
        compiler_params=_cparams(("arbitrary", "arbitrary", "arbitrary")),
        name="ffn",
    )(x, mod, nw, w1, w3, w2, fnw)


MOE_BLK = 512
SEL_LANE = 8


def _router_kernel(x_ref, mod_ref, nw_ref, wr_ref, br_ref, h_ref, g_ref):
    h = _rms_mod(x_ref[...], nw_ref[...], mod_ref[3:4, :], mod_ref[4:5, :])
    h_ref[...] = h.astype(BF16)
    logits = jnp.dot(h, wr_ref[...], preferred_element_type=F32, precision=HIGHEST) + br_ref[...]
    lane = lax.broadcasted_iota(jnp.int32, logits.shape, 1)
    logits = jnp.where(lane < N_EXPERTS, logits, -jnp.inf)
    t1 = jnp.max(logits, axis=-1, keepdims=True)
    i1 = jnp.min(jnp.where(logits == t1, lane, LANE), axis=-1, keepdims=True)
    rest = jnp.where(lane == i1, -jnp.inf, logits)
    t2 = jnp.max(rest, axis=-1, keepdims=True)
    i2 = jnp.min(jnp.where(rest == t2, lane, LANE), axis=-1, keepdims=True)
    e2 = jnp.exp(t2 - t1)
    den = 1.0 + e2
    gate = jnp.where(lane == i1, 1.0 / den, 0.0) + jnp.where(lane == i2, e2 / den, 0.0)
    flag = jnp.where((lane == i1 + SEL_LANE) | (lane == i2 + SEL_LANE), 1.0, 0.0)
    g_ref[...] = gate + flag


def _router(x, mod, nw, wr, br):
    b, t, _ = x.shape
    tm = _tile(t, 512)
    return pl.pallas_call(
        _router_kernel,
        grid=(b, t // tm),
        in_specs=[pl.BlockSpec((None, tm, D_MODEL), lambda b_, i: (b_, i, 0)),
                  pl.BlockSpec((None, 8, D_MODEL), lambda b_, i: (b_, 0, 0)),
                  pl.BlockSpec((1, D_MODEL), lambda b_, i: (0, 0)),
                  pl.BlockSpec((D_MODEL, LANE), lambda b_, i: (0, 0)),
                  pl.BlockSpec((1, LANE), lambda b_, i: (0, 0))],
        out_specs=[pl.BlockSpec((None, tm, D_MODEL), lambda b_, i: (b_, i, 0)),
                   pl.BlockSpec((None, tm, LANE), lambda b_, i: (b_, i, 0))],
        out_shape=[jax.ShapeDtypeStruct((b, t, D_MODEL), BF16),
                   jax.ShapeDtypeStruct((b, t, LANE), F32)],
        compiler_params=_cparams(("arbitrary", "arbitrary")),
        name="router",
    )(x, mod, nw, wr, br)


def _moe_plan(g, ts):
    n = g.shape[0]
    blk = MOE_BLK
    nt = n // ts
    n_blocks = TOP_K * n // blk + N_EXPERTS
    max_pairs = n_blocks + N_EXPERTS * nt
    sel = g[:, SEL_LANE:SEL_LANE + N_EXPERTS] > 0.5
    cum = jnp.cumsum(sel.astype(jnp.int32), axis=0)
    cnt = cum[-1]
    nblk = (cnt + blk - 1) // blk
    blk_end = jnp.cumsum(nblk)
    seg_off = (blk_end - nblk) * blk
    n_used = blk_end[-1]
    pos = jnp.where(sel, seg_off[None, :] + cum - 1, -1).astype(jnp.int32)
    bexp = jnp.minimum(jnp.sum(blk_end[None, :] <= jnp.arange(n_blocks)[:, None], axis=1), N_EXPERTS - 1)

    tile_end = cum[ts - 1::ts]
    tile_beg = jnp.concatenate([jnp.zeros((1, N_EXPERTS), jnp.int32), tile_end[:-1]], axis=0)
    d_lo = (seg_off[None, :] + tile_beg) // blk
    d_hi = (seg_off[None, :] + tile_end - 1) // blk
    npairs = jnp.where(tile_end > tile_beg, d_hi - d_lo + 1, 0)

    def expand(cnt_flat, cell_j, cell_e):
        cend = jnp.cumsum(cnt_flat)
        total = cend[-1]
        k = jnp.minimum(jnp.arange(max_pairs), total - 1)
        cell = jnp.sum(cend[None, :] <= k[:, None], axis=1)
        i = k - (cend[cell] - cnt_flat[cell])
        j, e = cell_j[cell], cell_e[cell]
        d = d_lo[j, e] + i
        valid = (jnp.arange(max_pairs) < total).astype(jnp.int32)
        return j.astype(jnp.int32), d.astype(jnp.int32), e.astype(jnp.int32), valid

    jj, ee = jnp.meshgrid(jnp.arange(nt), jnp.arange(N_EXPERTS), indexing='ij')
    by_tile = expand(npairs.reshape(-1), jj.reshape(-1), ee.reshape(-1))
    by_block = expand(npairs.T.reshape(-1), jj.T.reshape(-1), ee.T.reshape(-1))

    def edges(key, valid):
        prev = jnp.concatenate([jnp.full((1,), -1, jnp.int32), key[:-1]])
        nxt = jnp.concatenate([key[1:], jnp.full((1,), -1, jnp.int32)])
        nxt_valid = jnp.concatenate([valid[1:], jnp.zeros((1,), jnp.int32)])
        first = ((key != prev) & (valid == 1)).astype(jnp.int32)
        last = (((key != nxt) | (nxt_valid == 0)) & (valid == 1)).astype(jnp.int32)
        return first, last

    tj, td, te, tv = by_tile
    t_first, t_last = edges(tj, tv)
    bj, bd, be, bv = by_block
    b_first, _ = edges(bd, bv)
    return dict(pos=pos, pos_t=pos.T, gate=g[:, :N_EXPERTS], bexp=bexp.astype(jnp.int32),
                n_used=n_used.astype(jnp.int32).reshape(1), n_blocks=n_blocks, max_pairs=max_pairs,
                disp=(bj, bd, be, b_first, bv), comb=(tj, td, te, t_first, t_last, tv))


def _dispatch_kernel(pj, pd, pe, pfirst, pvalid, h_ref, post_ref, xs_ref, *, ts):
    k = pl.program_id(0)
    row = pd[k] * MOE_BLK + lax.broadcasted_iota(jnp.int32, (MOE_BLK, 1), 0)
    hit = post_ref[pl.ds(pe[k], 1), :] == row
    got = jnp.dot(jnp.where(hit, 1.0, 0.0).astype(BF16), h_ref[...], preferred_element_type=F32).astype(BF16)

    @pl.when(pfirst[k] == 1)
    def _():
        xs_ref[...] = got

    @pl.when((pfirst[k] == 0) & (pvalid[k] == 1))
    def _():
        xs_ref[...] = xs_ref[...] + got


def _dispatch(h, plan, ts):
    n = h.shape[0]
    rows = plan['n_blocks'] * MOE_BLK
    grid_spec = pltpu.PrefetchScalarGridSpec(
        num_scalar_prefetch=5,
        grid=(plan['max_pairs'],),
        in_specs=[pl.BlockSpec((ts, D_MODEL), lambda k, pj, pd, pe, pf, pv: (pj[k], 0)),
                  pl.BlockSpec((N_EXPERTS, ts), lambda k, pj, pd, pe, pf, pv: (0, pj[k]))],
        out_specs=pl.BlockSpec((MOE_BLK, D_MODEL), lambda k, pj, pd, pe, pf, pv: (pd[k], 0)),
    )
    return pl.pallas_call(
        functools.partial(_dispatch_kernel, ts=ts),
        grid_spec=grid_spec,
        out_shape=jax.ShapeDtypeStruct((rows, D_MODEL), BF16),
        compiler_params=_cparams(("arbitrary",)),
        name="moe_dispatch",
    )(*plan['disp'], h, plan['pos_t'])


def _experts_kernel(bexp, nused, x_ref, w1_ref, w3_ref, w2_ref, y_ref, acc_scr, *, nf):
    d = pl.program_id(0)
    f = pl.program_id(1)

    @pl.when(d < nused[0])
    def _():
        @pl.when(f == 0)
        def _():
            acc_scr[...] = jnp.zeros_like(acc_scr)

        x = x_ref[...]
        a = jnp.dot(x, w1_ref[...], preferred_element_type=F32)
        g = jnp.dot(x, w3_ref[...], preferred_element_type=F32)
        acc_scr[...] += jnp.dot((_silu(a) * g).astype(BF16), w2_ref[...], preferred_element_type=F32)

        @pl.when(f == nf - 1)
        def _():
            y_ref[...] = acc_scr[...].astype(y_ref.dtype)


def _experts(xs, plan, w1, w3, w2):
    dff = w1.shape[2]
    nf = 2
    tf = dff // nf
    n_blocks = plan['n_blocks']

    def blk(d, f, bexp, nused):
        return jnp.minimum(d, nused[0] - 1)

    def fidx(d, f, bexp, nused):
        return jnp.where(d < nused[0], f, nf - 1)

    grid_spec = pltpu.PrefetchScalarGridSpec(
        num_scalar_prefetch=2,
        grid=(n_blocks, nf),
        in_specs=[pl.BlockSpec((MOE_BLK, D_MODEL), lambda d, f, be, nu: (blk(d, f, be, nu), 0)),
                  pl.BlockSpec((None, D_MODEL, tf), lambda d, f, be, nu: (be[blk(d, f, be, nu)], 0, fidx(d, f, be, nu))),
                  pl.BlockSpec((None, D_MODEL, tf), lambda d, f, be, nu: (be[blk(d, f, be, nu)], 0, fidx(d, f, be, nu))),
                  pl.BlockSpec((None, tf, D_MODEL), lambda d, f, be, nu: (be[blk(d, f, be, nu)], fidx(d, f, be, nu), 0))],
        out_specs=pl.BlockSpec((MOE_BLK, D_MODEL), lambda d, f, be, nu: (blk(d, f, be, nu), 0)),
        scratch_shapes=[pltpu.VMEM((MOE_BLK, D_MODEL), F32)],
    )
    return pl.pallas_call(
        functools.partial(_experts_kernel, nf=nf),
        grid_spec=grid_spec,
        out_shape=jax.ShapeDtypeStruct(xs.shape, BF16),
        compiler_params=_cparams(("arbitrary", "arbitrary")),
        name="moe_experts",
    )(plan['bexp'], plan['n_used'], xs, w1, w3, w2)


def _combine_kernel(pj, pd, pe, pfirst, plast, pvalid, ys_ref, pos_ref, gate_ref, x_ref, mod_ref, fnw_ref,
                    xo_ref, acc_scr, *, ts, final_norm):
    k = pl.program_id(0)

    @pl.when(pfirst[k] == 1)
    def _():
        acc_scr[...] = jnp.zeros_like(acc_scr)

    @pl.when(pvalid[k] == 1)
    def _():
        e = pe[k]
        lane = lax.broadcasted_iota(jnp.int32, (ts, N_EXPERTS), 1)
        pos_e = jnp.sum(jnp.where(lane == e, pos_ref[...], 0), axis=1, keepdims=True)
        gate_e = jnp.sum(jnp.where(lane == e, gate_ref[...], 0.0), axis=1, keepdims=True)
        row = pd[k] * MOE_BLK + lax.broadcasted_iota(jnp.int32, (1, MOE_BLK), 1)
        hit = pos_e == row
        got = jnp.dot(jnp.where(hit, 1.0, 0.0).astype(BF16), ys_ref[...], preferred_element_type=F32)
        acc_scr[...] += gate_e * got

    @pl.when(plast[k] == 1)
    def _():
        xo = x_ref[...] + mod_ref[5:6, :] * acc_scr[...]
        if final_norm:
            xo = xo * lax.rsqrt(jnp.mean(xo * xo, axis=-1, keepdims=True) + EPS) * fnw_ref[...]
        xo_ref[...] = xo


def _combine(ys, plan, x, mod, fnw, ts, final_norm):
    n = x.shape[0]
    tiles_per_batch = n // (mod.shape[0] * ts)
    sp = lambda fn: (lambda k, pj, pd, pe, pf, pl_, pv: fn(k, pj, pd))
    grid_spec = pltpu.PrefetchScalarGridSpec(
        num_scalar_prefetch=6,
        grid=(plan['max_pairs'],),
        in_specs=[pl.BlockSpec((MOE_BLK, D_MODEL), sp(lambda k, pj, pd: (pd[k], 0))),
                  pl.BlockSpec((ts, N_EXPERTS), sp(lambda k, pj, pd: (pj[k], 0))),
                  pl.BlockSpec((ts, N_EXPERTS), sp(lambda k, pj, pd: (pj[k], 0))),
                  pl.BlockSpec((ts, D_MODEL), sp(lambda k, pj, pd: (pj[k], 0))),
                  pl.BlockSpec((None, 8, D_MODEL), sp(lambda k, pj, pd: (pj[k] // tiles_per_batch, 0, 0))),
                  pl.BlockSpec((1, D_MODEL), sp(lambda k, pj, pd: (0, 0)))],
        out_specs=pl.BlockSpec((ts, D_MODEL), sp(lambda k, pj, pd: (pj[k], 0))),
        scratch_shapes=[pltpu.VMEM((ts, D_MODEL), F32)],
    )
    return pl.pallas_call(
        functools.partial(_combine_kernel, ts=ts, final_norm=final_norm),
        grid_spec=grid_spec,
        out_shape=jax.ShapeDtypeStruct((n, D_MODEL), F32),
        compiler_params=_cparams(("arbitrary",)),
        name="moe_combine",
    )(*plan['comb'], ys, plan['pos'], plan['gate'], x, mod, fnw)


def _moe(x, mod, nw, wr, br, w1, w3, w2, fnw, final_norm):
    b, t, _ = x.shape
    ts = _tile(t, 512)
    h, g = _router(x, mod, nw, wr, br)
    plan = _moe_plan(g.reshape(b * t, LANE), ts)
    xs = _dispatch(h.reshape(b * t, D_MODEL), plan, ts)
    ys = _experts(xs, plan, w1, w3, w2)
    out = _combine(ys, plan, x.reshape(b * t, D_MODEL), mod, fnw, ts, final_norm)
    return out.reshape(b, t, D_MODEL)


def _pad_lanes(v, fill=0.0):
    return jnp.pad(v.astype(F32), (0, LANE - v.shape[0]), constant_values=fill)[None, :]


def kernel(x, c, ctx, c_ctx, w_mod, b_mod, norm_w, w_in, s5_lam_re, s5_lam_im, s5_log_dt, s5_b_re, s5_b_im, s5_c_re, s5_c_im, s5_d, s5_w_glu, lru_conv_w, lru_conv_b, lru_w_a, lru_b_a, lru_w_x, lru_b_x, lru_lam, ssd_conv_w, ssd_conv_b, ssd_a_log, ssd_dt_bias, ssd_d, ssd_norm_w, w_br_a, w_br_b, w_br_c, w_out, ffn_w1, ffn_w3, ffn_w2, moe_w_router, moe_b_router, moe_w1, moe_w3, moe_w2, final_norm_w):
    depth = w_mod.shape[0]
    nb, t_lat, _ = x.shape
    t_ctx = ctx.shape[1]
    assert t_lat % GRID_W == 0 and t_lat % SSD_CHUNK == 0 and t_ctx % SSD_CHUNK == 0

    cond_in = jnp.zeros((8, D_MODEL), F32).at[:nb].set(c).at[nb].set(c_ctx)
    mods = _modulation(cond_in, w_mod, b_mod).reshape(depth, 8, N_MOD, D_MODEL)
    mods = jnp.pad(mods, ((0, 0), (0, 0), (0, 8 - N_MOD), (0, 0)))

    head_of = jnp.arange(SSD_INNER) // SSD_HEAD_DIM
    e_mat = (jnp.arange(LANE)[:, None] == head_of[None, :]).astype(BF16)
    s5_perm = _s5_perm()
    n_lev_lat = max(1, math.ceil(math.log2(t_lat // S5_CHUNK)))

    x_lat, x_ctx = x, ctx
    for l in range(depth):
        ctx_out = l < depth - 1
        last = l == depth - 1
        mod_lat = mods[l, :nb]
        mod_ctx = jnp.broadcast_to(mods[l, nb][None], (nb, 8, D_MODEL))
        nw0 = norm_w[l, 0][None, :]
        nw1 = norm_w[l, 1][None, :]

        wl = w_in[l]
        o_lx = S5_WIDTH
        o_lg = o_lx + LRU_WIDTH
        o_z = o_lg + LRU_WIDTH
        o_xbc = o_z + SSD_INNER
        o_dt = o_xbc + SSD_XBC
        o_g = o_dt + SSD_HEADS
        w_main = jnp.concatenate(
            [wl[:, :o_lx], jnp.zeros((D_MODEL, P_LX - S5_WIDTH), wl.dtype), wl[:, o_lx:o_dt], wl[:, o_g:]],
            axis=1).astype(BF16)
        w_dt = jnp.pad(wl[:, o_dt:o_g], ((0, 0), (0, LANE - SSD_HEADS))).astype(BF16)
        kcat, pm, rm, atab = _s5_setup(s5_lam_re[l], s5_lam_im[l], s5_log_dt[l], s5_b_re[l], s5_b_im[l],
                                     s5_c_re[l], s5_c_im[l], n_lev_lat)
        wg, bg = _lru_gate_weights(lru_w_a[l], lru_b_a[l], lru_w_x[l], lru_b_x[l])
        lcw = lru_conv_w[l].astype(F32)
        lcb = lru_conv_b[l][None, :].astype(F32)
        scw = ssd_conv_w[l].astype(F32)
        scb = ssd_conv_b[l][None, :].astype(F32)
        a_log = jnp.concatenate([_pad_lanes(ssd_a_log[l, 0]), _pad_lanes(ssd_a_log[l, 1])], axis=0)
        dt_bias = jnp.concatenate([_pad_lanes(ssd_dt_bias[l, 0]), _pad_lanes(ssd_dt_bias[l, 1])], axis=0)
        dskip_x = jnp.repeat(ssd_d[l].astype(F32), SSD_HEAD_DIM)[None, :]

        def mixers(p, dt, t, w, s5_h0, lru_h0, ssd_h0f, ssd_h0b):
            y5, s5_ht = _s5_scan(_s5_chunks(p, nb, t), s5_perm, kcat, pm, rm, atab, s5_h0)
            ya = _s5_unchunk(y5, nb, t)
            yb, lru_ht = _lru_scan(p, lcw, lcb, wg, bg, lru_lam[l].astype(F32), lru_h0, w,
                                   1 if w > 1 else LRU_WIDTH // LRU_SLAB)
            yc, ssd_htf, ssd_htb = _ssd_scan(p, dt, scw, scb, a_log, dt_bias, dskip_x, e_mat, ssd_h0f, ssd_h0b)
            return ya, yb, yc, s5_ht, lru_ht, ssd_htf, ssd_htb

        p_ctx, dt_ctx = _inproj(x_ctx, mod_ctx, nw0, w_main, w_dt)
        zs5 = jnp.zeros((S5_GROUPS, nb, 4 * S5_STATE), F32)
        zlru = jnp.zeros((nb, 2, LRU_WIDTH), F32)
        zssd = jnp.zeros((nb, SSD_GROUPS * SSD_STATE, SSD_GW), F32)
        ya_c, yb_c, yc_c, s5_h, lru_h, ssd_hf, ssd_hb = mixers(p_ctx, dt_ctx, t_ctx, 1, zs5, zlru, zssd, zssd)

        p_lat, dt_lat = _inproj(x_lat, mod_lat, nw0, w_main, w_dt)
        ya, yb, yc, _, _, _, _ = mixers(p_lat, dt_lat, t_lat, GRID_W, s5_h, lru_h, ssd_hf, ssd_hb)

        mw = (s5_d[l][None, :].astype(F32), s5_w_glu[l].astype(BF16), ssd_norm_w[l][None, :].astype(F32),
              w_br_a[l].astype(BF16), w_br_b[l].astype(BF16), w_br_c[l].astype(BF16), w_out[l].astype(BF16))
        x_lat = _merge(x_lat, mod_lat, p_lat, ya, yb, yc, *mw)
        if ctx_out:
            x_ctx = _merge(x_ctx, mod_ctx, p_ctx, ya_c, yb_c, yc_c, *mw)

        fnw = final_norm_w[None, :].astype(F32)
        if l % 2 == 0:
            fw = (ffn_w1[l // 2].astype(BF16), ffn_w3[l // 2].astype(BF16), ffn_w2[l // 2].astype(BF16))
            x_lat = _ffn(x_lat, mod_lat, nw1, *fw, fnw, last)
            if ctx_out:
                x_ctx = _ffn(x_ctx, mod_ctx, nw1, *fw, fnw, False)
        else:
            wr = jnp.pad(moe_w_router[l // 2].astype(F32), ((0, 0), (0, LANE - N_EXPERTS)))
            br = _pad_lanes(moe_b_router[l // 2])
            ew = (moe_w1[l // 2].astype(BF16), moe_w3[l // 2].astype(BF16), moe_w2[l // 2].astype(BF16))
            x_lat = _moe(x_lat, mod_lat, nw1, wr, br, *ew, fnw, last)
            if ctx_out:
                x_ctx = _moe(x_ctx, mod_ctx, nw1, wr, br, *ew, fnw, False)
    return x_lat
```

```python
import functools
import math

import jax
import jax.numpy as jnp
from jax import lax
from jax.experimental import pallas as pl
from jax.experimental.pallas import tpu as pltpu

F32 = jnp.float32
BF16 = jnp.bfloat16
HIGHEST = lax.Precision.HIGHEST

D_MODEL = 1024
GRID_W = 64
N_MOD = 6
EPS = 1e-6

S5_WIDTH = 768
S5_GROUP = 16
S5_GROUPS = S5_WIDTH // S5_GROUP
S5_STATE = 64
S5_CHUNK = 32
S5_CW = S5_CHUNK * S5_GROUP

LRU_WIDTH = 1024
LRU_BLOCKS = 16
LRU_BLOCK = LRU_WIDTH // LRU_BLOCKS
LRU_CONV = 4
LRU_C = 8.0
LRU_SLAB = 128

SSD_INNER = 1024
SSD_HEAD_DIM = 64
SSD_HEADS = SSD_INNER // SSD_HEAD_DIM
SSD_GROUPS = 4
SSD_HPG = SSD_HEADS // SSD_GROUPS
SSD_STATE = 128
SSD_CONV = 4
SSD_CHUNK = 128
SSD_STEP_CHUNKS = 2
SSD_XBC = SSD_INNER + 2 * SSD_GROUPS * SSD_STATE
SSD_GW = SSD_HPG * SSD_HEAD_DIM

N_EXPERTS = 8
TOP_K = 2
LANE = 128

P_U, P_LX, P_LG, P_Z, P_XBC, P_GATE = 0, 1024, 2048, 3072, 4096, 6144
P_TOTAL = 9216

VMEM_LIMIT = 56 * 1024 * 1024


def _cparams(sem):
    return pltpu.CompilerParams(dimension_semantics=sem, vmem_limit_bytes=VMEM_LIMIT)


def _tile(n, pref):
    t = min(n, pref)
    while n % t:
        t //= 2
    return t


def _sigmoid(x):
    return jax.nn.sigmoid(x)


def _silu(x):
    return x * jax.nn.sigmoid(x)


def _softplus(x):
    return jnp.maximum(x, 0.0) + jnp.log(1.0 + jnp.exp(-jnp.abs(x)))


def _split_dot(x, w_bf16):
    hi = x.astype(BF16)
    lo = (x - hi.astype(F32)).astype(BF16)
    return (jnp.dot(hi, w_bf16, preferred_element_type=F32)
            + jnp.dot(lo, w_bf16, preferred_element_type=F32))


def _rms_mod(x, nw, shift, scale):
    y = x * lax.rsqrt(jnp.mean(x * x, axis=-1, keepdims=True) + EPS) * nw
    return y * (1.0 + scale) + shift


def _mod_kernel(c_ref, w_ref, b_ref, o_ref):
    cond = _silu(c_ref[...])
    o_ref[...] = jnp.dot(cond, w_ref[...], preferred_element_type=F32, precision=HIGHEST) + b_ref[...]


def _modulation(cond_in, w_mod, b_mod):
    depth = w_mod.shape[0]
    n = w_mod.shape[2]
    tn = 1024
    return pl.pallas_call(
        _mod_kernel,
        grid=(depth, n // tn),
        in_specs=[pl.BlockSpec((8, D_MODEL), lambda l, j: (0, 0)),
                  pl.BlockSpec((None, D_MODEL, tn), lambda l, j: (l, 0, j)),
                  pl.BlockSpec((None, 1, tn), lambda l, j: (l, 0, j))],
        out_specs=pl.BlockSpec((None, 8, tn), lambda l, j: (l, 0, j)),
        out_shape=jax.ShapeDtypeStruct((depth, 8, n), F32),
        compiler_params=_cparams(("arbitrary", "arbitrary")),
        name="modulation",
    )(cond_in, w_mod, b_mod.reshape(depth, 1, n))


def _inproj_kernel(x_ref, mod_ref, nw_ref, w_ref, wdt_ref, p_ref, dt_ref, h_scr):
    @pl.when(pl.program_id(2) == 0)
    def _():
        h = _rms_mod(x_ref[...], nw_ref[...], mod_ref[0:1, :], mod_ref[1:2, :]).astype(BF16)
        h_scr[...] = h
        dt_ref[...] = jnp.dot(h, wdt_ref[...], preferred_element_type=F32)

    p_ref[...] = jnp.dot(h_scr[...], w_ref[...], preferred_element_type=F32).astype(p_ref.dtype)


def _inproj(x, mod, nw, w, wdt):
    b, t, _ = x.shape
    tm = _tile(t, 1024)
    tn = 2304
    return pl.pallas_call(
        _inproj_kernel,
        grid=(b, t // tm, P_TOTAL // tn),
        in_specs=[pl.BlockSpec((None, tm, D_MODEL), lambda b_, i, j: (b_, i, 0)),
                  pl.BlockSpec((None, 8, D_MODEL), lambda b_, i, j: (b_, 0, 0)),
                  pl.BlockSpec((1, D_MODEL), lambda b_, i, j: (0, 0)),
                  pl.BlockSpec((D_MODEL, tn), lambda b_, i, j: (0, j)),
                  pl.BlockSpec((D_MODEL, LANE), lambda b_, i, j: (0, 0))],
        out_specs=[pl.BlockSpec((None, tm, tn), lambda b_, i, j: (b_, i, j)),
                   pl.BlockSpec((None, tm, LANE), lambda b_, i, j: (b_, i, 0))],
        out_shape=[jax.ShapeDtypeStruct((b, t, P_TOTAL), BF16),
                   jax.ShapeDtypeStruct((b, t, LANE), F32)],
        scratch_shapes=[pltpu.VMEM((tm, D_MODEL), BF16)],
        compiler_params=_cparams(("arbitrary", "arbitrary", "arbitrary")),
        name="inproj",
    )(x, mod, nw, w, wdt)


def _s5_setup(lam_re, lam_im, log_dt, b_re, b_im, c_re, c_im, n_levels):
    t = S5_CHUNK
    dt = jnp.exp(log_dt.astype(F32))[..., None]
    lre, lim = lam_re.astype(F32), lam_im.astype(F32)
    are, aim = lre * dt, lim * dt
    k = jnp.arange(t + 1, dtype=F32)[:, None]
    mag = jnp.exp(are[:, :, None, :] * k)
    pw_re = mag * jnp.cos(aim[:, :, None, :] * k)
    pw_im = mag * jnp.sin(aim[:, :, None, :] * k)
    a_re, a_im = pw_re[:, :, 1], pw_im[:, :, 1]
    den = lre * lre + lim * lim
    q_re = ((a_re - 1.0) * lre + a_im * lim) / den
    q_im = (a_im * lre - (a_re - 1.0) * lim) / den
    bb_re = q_re[..., None] * b_re - q_im[..., None] * b_im
    bb_im = q_re[..., None] * b_im + q_im[..., None] * b_re
    cr, ci = c_re.astype(F32), c_im.astype(F32)

    cb_re = cr[..., None] * bb_re[:, :, None] - ci[..., None] * bb_im[:, :, None]
    cb_im = cr[..., None] * bb_im[:, :, None] + ci[..., None] * bb_re[:, :, None]
    kern = (jnp.einsum('dgkp,dgjpi->dgkji', pw_re[:, :, :t], cb_re, precision=HIGHEST)
            - jnp.einsum('dgkp,dgjpi->dgkji', pw_im[:, :, :t], cb_im, precision=HIGHEST))
    kf, kb = kern[0], kern[1]
    taps = jnp.concatenate([kb[:, 1:][:, ::-1], (kf[:, 0] + kb[:, 0])[:, None], kf[:, 1:]], axis=1)
    kcat = taps.transpose(0, 3, 1, 2).reshape(S5_GROUPS, S5_GROUP, (2 * t - 1) * S5_GROUP)
    kcat = jnp.pad(kcat, ((0, 0), (0, 0), (0, S5_GROUP)))

    pf_re, pf_im = pw_re[0, :, :t][:, ::-1], pw_im[0, :, :t][:, ::-1]
    pb_re, pb_im = pw_re[1, :, :t], pw_im[1, :, :t]

    def in_map(p_re, p_im, d):
        w_re = p_re[:, :, None, :] * bb_re[d].transpose(0, 2, 1)[:, None] - p_im[:, :, None, :] * bb_im[d].transpose(0, 2, 1)[:, None]
        w_im = p_re[:, :, None, :] * bb_im[d].transpose(0, 2, 1)[:, None] + p_im[:, :, None, :] * bb_re[d].transpose(0, 2, 1)[:, None]
        return w_re, w_im

    wf_re, wf_im = in_map(pf_re, pf_im, 0)
    wb_re, wb_im = in_map(pb_re, pb_im, 1)
    pm = jnp.concatenate([wf_re, wf_im, wb_re, wb_im], axis=-1).reshape(S5_GROUPS, S5_CW, 4 * S5_STATE)

    def out_map(p_re, p_im, d):
        c_r = cr[d].transpose(0, 2, 1)[:, :, None, :]
        c_i = ci[d].transpose(0, 2, 1)[:, :, None, :]
        e_r = p_re.transpose(0, 2, 1)[..., None]
        e_i = p_im.transpose(0, 2, 1)[..., None]
        m_re = c_r * e_r - c_i * e_i
        m_im = c_r * e_i + c_i * e_r
        return m_re, -m_im

    rf_re, rf_im = out_map(pw_re[0, :, 1:t + 1], pw_im[0, :, 1:t + 1], 0)
    rb_re, rb_im = out_map(pw_re[1, :, 1:t + 1][:, ::-1], pw_im[1, :, 1:t + 1][:, ::-1], 1)
    rm = jnp.concatenate([rf_re, rf_im, rb_re, rb_im], axis=1).reshape(S5_GROUPS, 4 * S5_STATE, S5_CW)

    lev = (t * 2.0 ** jnp.arange(n_levels, dtype=F32))[:, None]
    lmag = jnp.exp(are[:, :, None, :] * lev)
    l_re = lmag * jnp.cos(aim[:, :, None, :] * lev)
    l_im = lmag * jnp.sin(aim[:, :, None, :] * lev)
    row_a = jnp.concatenate([l_re[0], l_re[0], l_re[1], l_re[1]], axis=-1)
    row_b = jnp.concatenate([-l_im[0], l_im[0], -l_im[1], l_im[1]], axis=-1)
    atab = jnp.stack([row_a, row_b], axis=2)
    return kcat, pm.astype(BF16), rm.astype(BF16), atab


def _shift_rows(h, d, down):
    n = h.shape[0]
    if d >= n:
        return jnp.zeros_like(h)
    row = lax.broadcasted_iota(jnp.int32, h.shape, 0)
    if down:
        return jnp.where(row >= d, pltpu.roll(h, d, 0), 0.0)
    return jnp.where(row < n - d, pltpu.roll(h, n - d, 0), 0.0)


S5_SLAB_GROUPS = LANE // S5_GROUP
S5_PIECES = S5_CW // LANE
S5_PER_PIECE = LANE // S5_GROUP


def _s5_kernel(u_ref, perm_ref, kc_ref, pm_ref, rm_ref, at_ref, h0_ref, y_ref, ht_ref,
               x_scr, yg_scr, tz_scr, hin_scr, *, nb, nc, n_levels):
    half = 2 * S5_STATE
    row = lax.broadcasted_iota(jnp.int32, (nc, half), 0)

    for v in range(S5_PIECES):
        a = jnp.concatenate(
            [jnp.concatenate([u_ref[b, S5_PER_PIECE * v + sl] for b in range(nb)], axis=0)
             for sl in range(S5_PER_PIECE)], axis=1)
        xp = jnp.dot(a.astype(BF16), perm_ref[...], preferred_element_type=F32).astype(BF16)
        for gl in range(S5_SLAB_GROUPS):
            x_scr[gl, :, v * LANE:(v + 1) * LANE] = xp[:, gl * LANE:(gl + 1) * LANE]

    def group(g, _):
        kc = kc_ref[g]
        for s in range(S5_CHUNK):
            off = (S5_CHUNK - 1 - s) * S5_GROUP
            tz_scr[s * S5_GROUP:(s + 1) * S5_GROUP, :] = kc[:, off:off + S5_CW].astype(BF16)
        x = x_scr[g]
        s_all = jnp.dot(x, pm_ref[g], preferred_element_type=F32)

        def cmul(h, k, lo):
            a = at_ref[g, k, 0:1, lo:lo + half]
            b = at_ref[g, k, 1:2, lo:lo + half]
            return a * h + b * pltpu.roll(h, S5_STATE, 1)

        for b in range(nb):
            for d in range(2):
                lo = d * half
                h = s_all[b * nc:(b + 1) * nc, lo:lo + half]
                h0 = h0_ref[g, b:b + 1, lo:lo + half]
                edge = 0 if d == 0 else nc - 1
                h = h + jnp.where(row == edge, cmul(jnp.broadcast_to(h0, (nc, half)), 0, lo), 0.0)
                for k in range(n_levels):
                    if (1 << k) < nc:
                        h = h + cmul(_shift_rows(h, 1 << k, d == 0), k, lo)
                ht_ref[g, b:b + 1, lo:lo + half] = h[nc - 1 - edge:nc - edge]
                hin = _shift_rows(h, 1, d == 0)
                hin = jnp.where(row == edge, jnp.broadcast_to(h0, (nc, half)), hin)
                hin_scr[b * nc:(b + 1) * nc, lo:lo + half] = hin

        y = jnp.dot(x, tz_scr[...], preferred_element_type=F32)
        y = y + jnp.dot(hin_scr[...].astype(BF16), rm_ref[g], preferred_element_type=F32)
        yg_scr[g] = y.astype(BF16)
        return 0
    lax.fori_loop(0, S5_SLAB_GROUPS, group, 0)

    for v in range(S5_PIECES):
        cat = jnp.concatenate([yg_scr[gl, :, v * LANE:(v + 1) * LANE] for gl in range(S5_SLAB_GROUPS)], axis=1)
        yp = jnp.dot(cat, perm_ref[...], preferred_element_type=F32)
        for tl in range(S5_PER_PIECE):
            for b in range(nb):
                y_ref[b, S5_PER_PIECE * v + tl] = yp[b * nc:(b + 1) * nc, tl * LANE:(tl + 1) * LANE].astype(y_ref.dtype)


def _s5_scan(u, perm, kcat, pm, rm, atab, h0):
    nb, _, nc, _ = u.shape
    rows = nb * nc
    n_levels = atab.shape[1]
    gs = S5_SLAB_GROUPS
    kern = functools.partial(_s5_kernel, nb=nb, nc=nc, n_levels=n_levels)
    tok_spec = pl.BlockSpec((nb, S5_CHUNK, nc, LANE), lambda i: (0, 0, 0, i))
    return pl.pallas_call(
        kern,
        grid=(S5_GROUPS // gs,),
        in_specs=[tok_spec,
                  pl.BlockSpec((gs * LANE, gs * LANE), lambda i: (0, 0)),
                  pl.BlockSpec((gs, S5_GROUP, 2 * S5_CW), lambda i: (i, 0, 0)),
                  pl.BlockSpec((gs, S5_CW, 4 * S5_STATE), lambda i: (i, 0, 0)),
                  pl.BlockSpec((gs, 4 * S5_STATE, S5_CW), lambda i: (i, 0, 0)),
                  pl.BlockSpec((gs, n_levels, 2, 4 * S5_STATE), lambda i: (i, 0, 0, 0)),
                  pl.BlockSpec((gs, nb, 4 * S5_STATE), lambda i: (i, 0, 0))],
        out_specs=[tok_spec,
                   pl.BlockSpec((gs, nb, 4 * S5_STATE), lambda i: (i, 0, 0))],
        out_shape=[jax.ShapeDtypeStruct(u.shape, u.dtype),
                   jax.ShapeDtypeStruct((S5_GROUPS, nb, 4 * S5_STATE), F32)],
        scratch_shapes=[pltpu.VMEM((gs, rows, S5_CW), BF16), pltpu.VMEM((gs, rows, S5_CW), BF16),
                        pltpu.VMEM((S5_CW, S5_CW), BF16), pltpu.VMEM((rows, 4 * S5_STATE), F32)],
        compiler_params=_cparams(("arbitrary",)),
        name="s5_scan",
    )(u, perm, kcat, pm, rm, atab, h0)


def _s5_chunks(p, nb, t):
    nc = t // S5_CHUNK
    u = p[:, :, P_U:P_U + S5_WIDTH].reshape(nb, nc, S5_CHUNK, S5_WIDTH).transpose(0, 2, 1, 3)
    return u if nc % 16 == 0 else u.astype(F32)


def _s5_unchunk(y, nb, t):
    return y.transpose(0, 2, 1, 3).reshape(nb, t, S5_WIDTH)


def _s5_perm():
    a = jnp.arange(S5_SLAB_GROUPS * LANE)
    s, g, i = a // LANE, (a % LANE) // S5_GROUP, a % S5_GROUP
    dst = g * LANE + s * S5_GROUP + i
    return (dst[:, None] == a[None, :]).astype(BF16)


def _lru_kernel(x_ref, cw_ref, cb_ref, wg_ref, bg_ref, lam_ref, h0_ref, out_ref, ht_ref,
                xp_scr, xc_scr, af_scr, bf_scr, ab_scr, bb_scr, cf_scr, cr_scr, *, w, s, pad_top, ch, ns):
    l = w * s
    n_ch = l // ch
    slabs = range(ns)
    lanes = [slice(k * LRU_SLAB, (k + 1) * LRU_SLAB) for k in slabs]
    zero_slab = jnp.zeros((w, LRU_SLAB), F32)
    one_slab = jnp.ones((w, LRU_SLAB), F32)

    for k in slabs:
        xp_scr[k, 0:pad_top, :] = jnp.zeros((pad_top, LRU_SLAB), F32)
        xp_scr[k, pad_top + l:pad_top + l + pad_top, :] = jnp.zeros((pad_top, LRU_SLAB), F32)

    def copy_body(i, _):
        r0 = pl.multiple_of(i * ch, ch)
        for k in slabs:
            xp_scr[k, pl.ds(pad_top + r0, ch), :] = x_ref[pl.ds(r0, ch), lanes[k]].astype(F32)
        return 0
    lax.fori_loop(0, n_ch, copy_body, 0)

    def conv_body(i, _):
        r0 = pl.multiple_of(i * ch, ch)
        for k in slabs:
            acc = cb_ref[:, lanes[k]] + cw_ref[2:3, lanes[k]] * xp_scr[k, pl.ds(pad_top + r0, ch), :]
            acc = acc + cw_ref[0:1, lanes[k]] * xp_scr[k, pl.ds(pad_top + r0 - 2 * w, ch), :]
            acc = acc + cw_ref[1:2, lanes[k]] * xp_scr[k, pl.ds(pad_top + r0 - w, ch), :]
            acc = acc + cw_ref[3:4, lanes[k]] * xp_scr[k, pl.ds(pad_top + r0 + w, ch), :]
            xc_scr[k, pl.ds(r0, ch), :] = acc
        return 0
    lax.fori_loop(0, n_ch, conv_body, 0)

    if w > 1:
        def prev_col(v):
            return _shift_rows(v, 1, True)

        def next_col(v):
            return _shift_rows(v, 1, False)

        for k in slabs:
            x_last = xp_scr[k, pad_top + (s - 1) * w:pad_top + s * w, :]
            x_last2 = xp_scr[k, pad_top + (s - 2) * w:pad_top + (s - 1) * w, :]
            x_first = xp_scr[k, pad_top:pad_top + w, :]
            w0, w1, w3 = cw_ref[0:1, lanes[k]], cw_ref[1:2, lanes[k]], cw_ref[3:4, lanes[k]]
            xc_scr[k, 0:w, :] = xc_scr[k, 0:w, :] + w0 * prev_col(x_last2) + w1 * prev_col(x_last)
            xc_scr[k, w:2 * w, :] = xc_scr[k, w:2 * w, :] + w0 * prev_col(x_last)
            xc_scr[k, (s - 1) * w:s * w, :] = xc_scr[k, (s - 1) * w:s * w, :] + w3 * next_col(x_first)

    c_half = (0.5 * LRU_C) * -_softplus(-lam_ref[...])

    def coef_body(i, _):
        r0 = pl.multiple_of(i * ch, ch)
        for k in slabs:
            xc = xc_scr[k, pl.ds(r0, ch), :]
            hx = 0.5 * xc
            g = jnp.dot(xc.astype(BF16), wg_ref[k], preferred_element_type=F32) + bg_ref[k]
            for d, (a_scr, b_scr) in enumerate(((af_scr, bf_scr), (ab_scr, bb_scr))):
                t_r = jnp.tanh(g[:, (2 * d) * LRU_SLAB:(2 * d + 1) * LRU_SLAB])
                t_i = jnp.tanh(g[:, (2 * d + 1) * LRU_SLAB:(2 * d + 2) * LRU_SLAB])
                c = c_half[d:d + 1, lanes[k]]
                a = jnp.exp(c * t_r + c)
                a_scr[k, pl.ds(r0, ch), :] = a
                b_scr[k, pl.ds(r0, ch), :] = jnp.sqrt(1.0 - a * a) * (hx * (t_i + 1.0))
        return 0
    lax.fori_loop(0, n_ch, coef_body, 0)

    def scan_body(r, carry):
        rf = pl.multiple_of(r * w, w)
        rb = pl.multiple_of((s - 1 - r) * w, w)
        out = []
        for k in slabs:
            hf, pf, hb, pb = carry[k]
            a = af_scr[k, pl.ds(rf, w), :]
            hf = a * hf + bf_scr[k, pl.ds(rf, w), :]
            pf = a * pf
            bf_scr[k, pl.ds(rf, w), :] = hf
            af_scr[k, pl.ds(rf, w), :] = pf
            a = ab_scr[k, pl.ds(rb, w), :]
            hb = a * hb + bb_scr[k, pl.ds(rb, w), :]
            pb = a * pb
            bb_scr[k, pl.ds(rb, w), :] = hb
            ab_scr[k, pl.ds(rb, w), :] = pb
            out.append((hf, pf, hb, pb))
        return tuple(out)
    lax.fori_loop(0, s, scan_body, tuple((zero_slab, one_slab, zero_slab, one_slab) for _ in slabs))

    def carry_f(c, carry):
        out = []
        for k in slabs:
            cf_scr[k, pl.ds(c, 1), :] = carry[k]
            out.append(bf_scr[k, pl.ds((s - 1) * w + c, 1), :] + af_scr[k, pl.ds((s - 1) * w + c, 1), :] * carry[k])
        return tuple(out)
    fin = lax.fori_loop(0, w, carry_f, tuple(h0_ref[0:1, lanes[k]] for k in slabs))
    for k in slabs:
        ht_ref[0:1, lanes[k]] = fin[k]

    def carry_b(i, carry):
        c = w - 1 - i
        out = []
        for k in slabs:
            cr_scr[k, pl.ds(c, 1), :] = carry[k]
            out.append(bb_scr[k, pl.ds(c, 1), :] + ab_scr[k, pl.ds(c, 1), :] * carry[k])
        return tuple(out)
    fin = lax.fori_loop(0, w, carry_b, tuple(h0_ref[1:2, lanes[k]] for k in slabs))
    for k in slabs:
        ht_ref[1:2, lanes[k]] = fin[k]

    def out_body(r, _):
        r0 = pl.multiple_of(r * w, w)
        for k in slabs:
            y = (bf_scr[k, pl.ds(r0, w), :] + af_scr[k, pl.ds(r0, w), :] * cf_scr[k]
                 + bb_scr[k, pl.ds(r0, w), :] + ab_scr[k, pl.ds(r0, w), :] * cr_scr[k])
            bf_scr[k, pl.ds(r0, w), :] = y
        return 0
    lax.fori_loop(0, s, out_body, 0)

    def store_body(i, _):
        r0 = pl.multiple_of(i * ch, ch)
        for k in slabs:
            out_ref[pl.ds(r0, ch), lanes[k]] = bf_scr[k, pl.ds(r0, ch), :].astype(out_ref.dtype)
        return 0
    lax.fori_loop(0, n_ch, store_body, 0)


def _lru_scan(p, cw, cb, wg, bg, lam, h0, w, ns):
    b, t, _ = p.shape
    s = t // w
    assert s * w == t and s >= 4
    ch = _tile(t, 256)
    pad_top = max(2 * w, 8)
    cwid = ns * LRU_SLAB
    col0 = P_LX // cwid
    kern = functools.partial(_lru_kernel, w=w, s=s, pad_top=pad_top, ch=ch, ns=ns)
    return pl.pallas_call(
        kern,
        grid=(b, LRU_WIDTH // cwid),
        in_specs=[pl.BlockSpec((None, t, cwid), lambda b_, k: (b_, 0, col0 + k)),
                  pl.BlockSpec((LRU_CONV, cwid), lambda b_, k: (0, k)),
                  pl.BlockSpec((1, cwid), lambda b_, k: (0, k)),
                  pl.BlockSpec((ns, LRU_SLAB, 4 * LRU_SLAB), lambda b_, k: (k, 0, 0)),
                  pl.BlockSpec((ns, 1, 4 * LRU_SLAB), lambda b_, k: (k, 0, 0)),
                  pl.BlockSpec((2, cwid), lambda b_, k: (0, k)),
                  pl.BlockSpec((None, 2, cwid), lambda b_, k: (b_, 0, k))],
        out_specs=[pl.BlockSpec((None, t, cwid), lambda b_, k: (b_, 0, k)),
                   pl.BlockSpec((None, 2, cwid), lambda b_, k: (b_, 0, k))],
        out_shape=[jax.ShapeDtypeStruct((b, t, LRU_WIDTH), BF16),
                   jax.ShapeDtypeStruct((b, 2, LRU_WIDTH), F32)],
        scratch_shapes=[pltpu.VMEM((ns, t + 2 * pad_top, LRU_SLAB), F32),
                        pltpu.VMEM((ns, t, LRU_SLAB), F32),
                        pltpu.VMEM((ns, t, LRU_SLAB), F32), pltpu.VMEM((ns, t, LRU_SLAB), F32),
                        pltpu.VMEM((ns, t, LRU_SLAB), F32), pltpu.VMEM((ns, t, LRU_SLAB), F32),
                        pltpu.VMEM((ns, w, LRU_SLAB), F32), pltpu.VMEM((ns, w, LRU_SLAB), F32)],
        compiler_params=_cparams(("arbitrary", "arbitrary")),
        name="lru_scan",
    )(p, cw, cb, wg, bg, lam, h0)


def _lru_gate_weights(w_a, b_a, w_x, b_x):
    n_slab = LRU_WIDTH // LRU_SLAB
    per = LRU_SLAB // LRU_BLOCK

    def slab_diag(wm):
        wm = wm.reshape(n_slab, per, LRU_BLOCK, LRU_BLOCK)
        eye = jnp.eye(per, dtype=wm.dtype)
        return jnp.einsum('spkj,pq->spkqj', wm, eye).reshape(n_slab, LRU_SLAB, LRU_SLAB)

    wg = jnp.concatenate([slab_diag(w_a[0]), slab_diag(w_x[0]), slab_diag(w_a[1]), slab_diag(w_x[1])], axis=2)
    bg = jnp.concatenate([b_a[0].reshape(n_slab, 1, LRU_SLAB), b_x[0].reshape(n_slab, 1, LRU_SLAB),
                          b_a[1].reshape(n_slab, 1, LRU_SLAB), b_x[1].reshape(n_slab, 1, LRU_SLAB)], axis=2)
    return (0.5 * wg).astype(BF16), (0.5 * bg).astype(F32)


def _ssd_core(xs, bm, cm, dtr, alog, dbias, e_ref, h_scr, y_scr, reverse):
    q = SSD_CHUNK
    a = -jnp.exp(alog)
    dt = _softplus(dtr + dbias)
    adt = dt * a
    row = lax.broadcasted_iota(jnp.int32, (q, q), 0)
    col = lax.broadcasted_iota(jnp.int32, (q, q), 1)
    tri = (col >= row) if reverse else (col <= row)
    cs = jnp.dot(tri.astype(F32), adt, preferred_element_type=F32, precision=HIGHEST)
    cs_t = cs.T
    dt_t = dt.T
    end = 0 if reverse else q - 1
    tot = cs[end:end + 1, :]
    e = e_ref[...]
    w_x = _split_dot(dt * jnp.exp(tot - cs), e)
    ecs_x = _split_dot(jnp.exp(cs), e)
    xsb = xs.astype(BF16)
    xw = (xs * w_x).astype(BF16)
    decay_row = ecs_x[end:end + 1, :]

    for g in range(SSD_GROUPS):
        cg = cm[:, g * SSD_STATE:(g + 1) * SSD_STATE]
        bg = bm[:, g * SSD_STATE:(g + 1) * SSD_STATE]
        sc = lax.dot_general(cg, bg, (((1,), (1,)), ((), ())), preferred_element_type=F32)
        for j in range(SSD_HPG):
            hd = g * SSD_HPG + j
            cols = slice(hd * SSD_HEAD_DIM, (hd + 1) * SSD_HEAD_DIM)
            diff = cs[:, hd:hd + 1] - cs_t[hd:hd + 1, :]
            lm = jnp.where(tri, jnp.exp(diff), 0.0) * dt_t[hd:hd + 1, :]
            y_scr[:, cols] = jnp.dot((sc * lm).astype(BF16), xsb[:, cols], preferred_element_type=F32)
        ch = slice(g * SSD_GW, (g + 1) * SSD_GW)
        rows = slice(g * SSD_STATE, (g + 1) * SSD_STATE)
        h_old = h_scr[rows, :]
        y_off = jnp.dot(cg, h_old.astype(BF16), preferred_element_type=F32) * ecs_x[:, ch]
        y_scr[:, ch] = y_scr[:, ch] + y_off
        st = lax.dot_general(bg, xw[:, ch], (((0,), (0,)), ((), ())), preferred_element_type=F32)
        h_scr[rows, :] = decay_row[:, ch] * h_old + st
    return y_scr[...]


def _ssd_fwd_kernel(x_ref, xprev_ref, xnext_ref, dt_ref, cw_ref, cb_ref, alog_ref, dbias_ref, e_ref, h0_ref,
                    y_ref, xc_ref, ht_ref, h_scr, y_scr, xcf_scr, *, nc):
    c = pl.program_id(1)

    @pl.when(c == 0)
    def _():
        h_scr[...] = h0_ref[...]

    q = x_ref.shape[0]
    x = x_ref[...].astype(F32)
    prev = jnp.where(c > 0, xprev_ref[...].astype(F32), 0.0)
    nxt = jnp.where(c < nc - 1, xnext_ref[...].astype(F32), 0.0)
    row8 = lax.broadcasted_iota(jnp.int32, (8, SSD_XBC), 0)

    def behind(k):
        xr = pltpu.roll(x, k, 0)
        top = jnp.where(row8 < k, pltpu.roll(prev, k, 0)[0:8], xr[0:8])
        return jnp.concatenate([top, xr[8:]], axis=0)

    xm2, xm1 = behind(2), behind(1)
    xr = pltpu.roll(x, q - 1, 0)
    bottom = jnp.where(row8 == 7, pltpu.roll(nxt, 15, 0)[8:16], xr[q - 8:q])
    xp1 = jnp.concatenate([xr[:q - 8], bottom], axis=0)
    conv = (cb_ref[...] + cw_ref[0:1, :] * xm2 + cw_ref[1:2, :] * xm1
            + cw_ref[2:3, :] * x + cw_ref[3:4, :] * xp1)
    xc = _silu(conv)
    xc_ref[...] = xc.astype(xc_ref.dtype)
    xcf_scr[...] = xc
    for k in range(q // SSD_CHUNK):
        r = slice(k * SSD_CHUNK, (k + 1) * SSD_CHUNK)
        xs = xcf_scr[r, :SSD_INNER]
        bm = xcf_scr[r, SSD_INNER:SSD_INNER + SSD_GROUPS * SSD_STATE].astype(BF16)
        cm = xcf_scr[r, SSD_INNER + SSD_GROUPS * SSD_STATE:].astype(BF16)
        y_ref[r, :] = _ssd_core(xs, bm, cm, dt_ref[r, :], alog_ref[...], dbias_ref[...], e_ref, h_scr, y_scr, False)

    @pl.when(c == nc - 1)
    def _():
        ht_ref[...] = h_scr[...]


def _ssd_bwd_kernel(xc_ref, dt_ref, yf_ref, alog_ref, dbias_ref, dskip_ref, e_ref, h0_ref,
                    y_ref, ht_ref, h_scr, y_scr, *, nc):
    c = pl.program_id(1)

    @pl.when(c == 0)
    def _():
        h_scr[...] = h0_ref[...]

    for k in reversed(range(xc_ref.shape[0] // SSD_CHUNK)):
        r = slice(k * SSD_CHUNK, (k + 1) * SSD_CHUNK)
        xs = xc_ref[r, :SSD_INNER].astype(F32)
        bm = xc_ref[r, SSD_INNER:SSD_INNER + SSD_GROUPS * SSD_STATE]
        cm = xc_ref[r, SSD_INNER + SSD_GROUPS * SSD_STATE:]
        y = _ssd_core(xs, bm, cm, dt_ref[r, :], alog_ref[...], dbias_ref[...], e_ref, h_scr, y_scr, True)
        y_ref[r, :] = y + yf_ref[r, :] + dskip_ref[...] * xs

    @pl.when(c == nc - 1)
    def _():
        ht_ref[...] = h_scr[...]


def _ssd_scan(p, dt, cw, cb, a_log, dt_bias, dskip_x, e_mat, h0f, h0b):
    b, t, _ = p.shape
    q = SSD_STEP_CHUNKS * SSD_CHUNK
    nc = t // q
    assert nc * q == t
    xbc_blk = P_XBC // SSD_XBC
    n16 = t // 16
    hs = (SSD_GROUPS * SSD_STATE, SSD_GW)
    const2 = lambda b_, c: (0, 0)
    state_spec = pl.BlockSpec((None,) + hs, lambda b_, c: (b_, 0, 0))

    yf, xc, htf = pl.pallas_call(
        functools.partial(_ssd_fwd_kernel, nc=nc),
        grid=(b, nc),
        in_specs=[pl.BlockSpec((None, q, SSD_XBC), lambda b_, c: (b_, c, xbc_blk)),
                  pl.BlockSpec((None, 16, SSD_XBC),
                               lambda b_, c: (b_, jnp.maximum(c * (q // 16) - 1, 0), xbc_blk)),
                  pl.BlockSpec((None, 16, SSD_XBC),
                               lambda b_, c: (b_, jnp.minimum((c + 1) * (q // 16), n16 - 1), xbc_blk)),
                  pl.BlockSpec((None, q, LANE), lambda b_, c: (b_, c, 0)),
                  pl.BlockSpec((SSD_CONV, SSD_XBC), const2),
                  pl.BlockSpec((1, SSD_XBC), const2),
                  pl.BlockSpec((1, LANE), const2),
                  pl.BlockSpec((1, LANE), const2),
                  pl.BlockSpec((LANE, SSD_INNER), const2),
                  state_spec],
        out_specs=[pl.BlockSpec((None, q, SSD_INNER), lambda b_, c: (b_, c, 0)),
                   pl.BlockSpec((None, q, SSD_XBC), lambda b_, c: (b_, c, 0)),
                   state_spec],
        out_shape=[jax.ShapeDtypeStruct((b, t, SSD_INNER), F32),
                   jax.ShapeDtypeStruct((b, t, SSD_XBC), BF16),
                   jax.ShapeDtypeStruct((b,) + hs, F32)],
        scratch_shapes=[pltpu.VMEM(hs, F32), pltpu.VMEM((SSD_CHUNK, SSD_INNER), F32),
                        pltpu.VMEM((q, SSD_XBC), F32)],
        compiler_params=_cparams(("arbitrary", "arbitrary")),
        name="ssd_fwd",
    )(p, p, p, dt, cw, cb, a_log[0:1], dt_bias[0:1], e_mat, h0f)

    rev = lambda b_, c: (b_, nc - 1 - c, 0)
    y, htb = pl.pallas_call(
        functools.partial(_ssd_bwd_kernel, nc=nc),
        grid=(b, nc),
        in_specs=[pl.BlockSpec((None, q, SSD_XBC), rev),
                  pl.BlockSpec((None, q, LANE), rev),
                  pl.BlockSpec((None, q, SSD_INNER), rev),
                  pl.BlockSpec((1, LANE), const2),
                  pl.BlockSpec((1, LANE), const2),
                  pl.BlockSpec((1, SSD_INNER), const2),
                  pl.BlockSpec((LANE, SSD_INNER), const2),
                  state_spec],
        out_specs=[pl.BlockSpec((None, q, SSD_INNER), rev), state_spec],
        out_shape=[jax.ShapeDtypeStruct((b, t, SSD_INNER), F32),
                   jax.ShapeDtypeStruct((b,) + hs, F32)],
        scratch_shapes=[pltpu.VMEM(hs, F32), pltpu.VMEM((SSD_CHUNK, SSD_INNER), F32)],
        compiler_params=_cparams(("arbitrary", "arbitrary")),
        name="ssd_bwd",
    )(xc, dt, yf, a_log[1:2], dt_bias[1:2], dskip_x, e_mat, h0b)
    return y, htf, htb


def _merge_kernel(x_ref, mod_ref, u_ref, ya_ref, yb_ref, g_ref, yc_ref, z_ref, gates_ref,
                  d5_ref, wglu_ref, snw_ref, wa_ref, wb_ref, wc_ref, wo_ref, xo_ref):
    ya = jax.nn.gelu(ya_ref[...].astype(F32) + d5_ref[...] * u_ref[...].astype(F32))
    ya = ya * _sigmoid(jnp.dot(ya.astype(BF16), wglu_ref[...], preferred_element_type=F32))
    yb = yb_ref[...].astype(F32) * jax.nn.gelu(g_ref[...].astype(F32))
    yc = yc_ref[...] * _silu(z_ref[...].astype(F32))
    yc = yc * lax.rsqrt(jnp.mean(yc * yc, axis=-1, keepdims=True) + EPS) * snw_ref[...]
    mix = _sigmoid(gates_ref[:, 0:D_MODEL].astype(F32)) * jnp.dot(
        ya.astype(BF16), wa_ref[...], preferred_element_type=F32)
    mix = mix + _sigmoid(gates_ref[:, D_MODEL:2 * D_MODEL].astype(F32)) * jnp.dot(
        yb.astype(BF16), wb_ref[...], preferred_element_type=F32)
    mix = mix + _sigmoid(gates_ref[:, 2 * D_MODEL:3 * D_MODEL].astype(F32)) * jnp.dot(
        yc.astype(BF16), wc_ref[...], preferred_element_type=F32)
    y = jnp.dot(mix.astype(BF16), wo_ref[...], preferred_element_type=F32)
    xo_ref[...] = x_ref[...] + mod_ref[2:3, :] * y


def _merge(x, mod, p, ya, yb, yc, d5, wglu, snw, wa, wb, wc, wo):
    b, t, _ = x.shape
    tm = _tile(t, 512)
    tok = lambda width, blk: pl.BlockSpec((None, tm, width), lambda b_, i: (b_, i, blk))
    full = lambda r, c_: pl.BlockSpec((r, c_), lambda b_, i: (0, 0))
    return pl.pallas_call(
        _merge_kernel,
        grid=(b, t // tm),
        in_specs=[tok(D_MODEL, 0),
                  pl.BlockSpec((None, 8, D_MODEL), lambda b_, i: (b_, 0, 0)),
                  tok(S5_WIDTH, P_U // S5_WIDTH),
                  tok(S5_WIDTH, 0),
                  tok(LRU_WIDTH, 0),
                  tok(LRU_WIDTH, P_LG // LRU_WIDTH),
                  tok(SSD_INNER, 0),
                  tok(SSD_INNER, P_Z // SSD_INNER),
                  tok(3 * D_MODEL, P_GATE // (3 * D_MODEL)),
                  full(1, S5_WIDTH), full(S5_WIDTH, S5_WIDTH), full(1, SSD_INNER),
                  full(S5_WIDTH, D_MODEL), full(LRU_WIDTH, D_MODEL), full(SSD_INNER, D_MODEL),
                  full(D_MODEL, D_MODEL)],
        out_specs=tok(D_MODEL, 0),
        out_shape=jax.ShapeDtypeStruct((b, t, D_MODEL), F32),
        compiler_params=_cparams(("arbitrary", "arbitrary")),
        name="merge",
    )(x, mod, p, ya, yb, p, yc, p, p, d5, wglu, snw, wa, wb, wc, wo)


def _ffn_kernel(x_ref, mod_ref, nw_ref, w1_ref, w3_ref, w2_ref, fnw_ref, xo_ref, h_scr, acc_scr, *, nf, final_norm):
    f = pl.program_id(2)

    @pl.when(f == 0)
    def _():
        h_scr[...] = _rms_mod(x_ref[...], nw_ref[...], mod_ref[3:4, :], mod_ref[4:5, :]).astype(BF16)
        acc_scr[...] = jnp.zeros_like(acc_scr)

    h = h_scr[...]
    a = jnp.dot(h, w1_ref[...], preferred_element_type=F32)
    g = jnp.dot(h, w3_ref[...], preferred_element_type=F32)
    acc_scr[...] += jnp.dot((_silu(a) * g).astype(BF16), w2_ref[...], preferred_element_type=F32)

    @pl.when(f == nf - 1)
    def _():
        xo = x_ref[...] + mod_ref[5:6, :] * acc_scr[...]
        if final_norm:
            xo = xo * lax.rsqrt(jnp.mean(xo * xo, axis=-1, keepdims=True) + EPS) * fnw_ref[...]
        xo_ref[...] = xo


def _ffn(x, mod, nw, w1, w3, w2, fnw, final_norm):
    b, t, _ = x.shape
    dff = w1.shape[1]
    tm = _tile(t, 512)
    tf = dff // 2
    nf = dff // tf
    return pl.pallas_call(
        functools.partial(_ffn_kernel, nf=nf, final_norm=final_norm),
        grid=(b, t // tm, nf),
        in_specs=[pl.BlockSpec((None, tm, D_MODEL), lambda b_, i, f: (b_, i, 0)),
                  pl.BlockSpec((None, 8, D_MODEL), lambda b_, i, f: (b_, 0, 0)),
                  pl.BlockSpec((1, D_MODEL), lambda b_, i, f: (0, 0)),
                  pl.BlockSpec((D_MODEL, tf), lambda b_, i, f: (0, f)),
                  pl.BlockSpec((D_MODEL, tf), lambda b_, i, f: (0, f)),
                  pl.BlockSpec((tf, D_MODEL), lambda b_, i, f: (f, 0)),
                  pl.BlockSpec((1, D_MODEL), lambda b_, i, f: (0, 0))],
        out_specs=pl.BlockSpec((None, tm, D_MODEL), lambda b_, i, f: (b_, i, 0)),
        out_shape=jax.ShapeDtypeStruct((b, t, D_MODEL), F32),
        scratch_shapes=[pltpu.VMEM((tm, D_MODEL), BF16), pltpu.VMEM((tm, D_MODEL), F32)],
        compiler_params=_cparams(("arbitrary", "arbitrary", "arbitrary")),
        name="ffn",
    )(x, mod, nw, w1, w3, w2, fnw)


MOE_BLK = 512
MOE_PER_EBLK = 2
SEL_LANE = 8


def _router_kernel(x_ref, mod_ref, nw_ref, wr_ref, br_ref, h_ref, g_ref):
    h = _rms_mod(x_ref[...], nw_ref[...], mod_ref[3:4, :], mod_ref[4:5, :])
    h_ref[...] = h.astype(BF16)
    logits = jnp.dot(h, wr_ref[...], preferred_element_type=F32, precision=HIGHEST) + br_ref[...]
    lane = lax.broadcasted_iota(jnp.int32, logits.shape, 1)
    logits = jnp.where(lane < N_EXPERTS, logits, -jnp.inf)
    t1 = jnp.max(logits, axis=-1, keepdims=True)
    i1 = jnp.min(jnp.where(logits == t1, lane, LANE), axis=-1, keepdims=True)
    rest = jnp.where(lane == i1, -jnp.inf, logits)
    t2 = jnp.max(rest, axis=-1, keepdims=True)
    i2 = jnp.min(jnp.where(rest == t2, lane, LANE), axis=-1, keepdims=True)
    e2 = jnp.exp(t2 - t1)
    den = 1.0 + e2
    gate = jnp.where(lane == i1, 1.0 / den, 0.0) + jnp.where(lane == i2, e2 / den, 0.0)
    flag = jnp.where((lane == i1 + SEL_LANE) | (lane == i2 + SEL_LANE), 1.0, 0.0)
    g_ref[...] = gate + flag


def _router(x, mod, nw, wr, br):
    b, t, _ = x.shape
    tm = _tile(t, 512)
    return pl.pallas_call(
        _router_kernel,
        grid=(b, t // tm),
        in_specs=[pl.BlockSpec((None, tm, D_MODEL), lambda b_, i: (b_, i, 0)),
                  pl.BlockSpec((None, 8, D_MODEL), lambda b_, i: (b_, 0, 0)),
                  pl.BlockSpec((1, D_MODEL), lambda b_, i: (0, 0)),
                  pl.BlockSpec((D_MODEL, LANE), lambda b_, i: (0, 0)),
                  pl.BlockSpec((1, LANE), lambda b_, i: (0, 0))],
        out_specs=[pl.BlockSpec((None, tm, D_MODEL), lambda b_, i: (b_, i, 0)),
                   pl.BlockSpec((None, tm, LANE), lambda b_, i: (b_, i, 0))],
        out_shape=[jax.ShapeDtypeStruct((b, t, D_MODEL), BF16),
                   jax.ShapeDtypeStruct((b, t, LANE), F32)],
        compiler_params=_cparams(("arbitrary", "arbitrary")),
        name="router",
    )(x, mod, nw, wr, br)


def _moe_plan(g, ts):
    n = g.shape[0]
    blk = MOE_BLK
    eblk = MOE_PER_EBLK * MOE_BLK
    nt = n // ts
    n_eblocks = TOP_K * n // eblk + N_EXPERTS
    n_blocks = n_eblocks * MOE_PER_EBLK
    max_pairs = n_blocks + N_EXPERTS * nt
    sel = g[:, SEL_LANE:SEL_LANE + N_EXPERTS] > 0.5
    cum = jnp.cumsum(sel.astype(jnp.int32), axis=0)
    cnt = cum[-1]
    neb = (cnt + eblk - 1) // eblk
    eb_end = jnp.cumsum(neb)
    eb_beg = eb_end - neb
    seg_off = eb_beg * eblk
    n_used = eb_end[-1]
    pos = jnp.where(sel, seg_off[None, :] + cum - 1, -1).astype(jnp.int32)
    eb = jnp.arange(n_eblocks)
    bexp = jnp.minimum(jnp.sum(eb_end[None, :] <= eb[:, None], axis=1), N_EXPERTS - 1)
    hi_ok = (cnt[bexp] - (eb - eb_beg[bexp]) * eblk > blk).astype(jnp.int32)

    tile_end = cum[ts - 1::ts]
    tile_beg = jnp.concatenate([jnp.zeros((1, N_EXPERTS), jnp.int32), tile_end[:-1]], axis=0)
    d_lo = (seg_off[None, :] + tile_beg) // blk
    d_hi = (seg_off[None, :] + tile_end - 1) // blk
    npairs = jnp.where(tile_end > tile_beg, d_hi - d_lo + 1, 0)

    def expand(cnt_flat, cell_j, cell_e):
        cend = jnp.cumsum(cnt_flat)
        total = cend[-1]
        k = jnp.minimum(jnp.arange(max_pairs), total - 1)
        cell = jnp.sum(cend[None, :] <= k[:, None], axis=1)
        i = k - (cend[cell] - cnt_flat[cell])
        j, e = cell_j[cell], cell_e[cell]
        d = d_lo[j, e] + i
        valid = (jnp.arange(max_pairs) < total).astype(jnp.int32)
        return j.astype(jnp.int32), d.astype(jnp.int32), e.astype(jnp.int32), valid

    jj, ee = jnp.meshgrid(jnp.arange(nt), jnp.arange(N_EXPERTS), indexing='ij')
    by_tile = expand(npairs.reshape(-1), jj.reshape(-1), ee.reshape(-1))
    by_block = expand(npairs.T.reshape(-1), jj.T.reshape(-1), ee.T.reshape(-1))

    def edges(key, valid):
        prev = jnp.concatenate([jnp.full((1,), -1, jnp.int32), key[:-1]])
        nxt = jnp.concatenate([key[1:], jnp.full((1,), -1, jnp.int32)])
        nxt_valid = jnp.concatenate([valid[1:], jnp.zeros((1,), jnp.int32)])
        first = ((key != prev) & (valid == 1)).astype(jnp.int32)
        last = (((key != nxt) | (nxt_valid == 0)) & (valid == 1)).astype(jnp.int32)
        return first, last

    tj, td, te, tv = by_tile
    t_first, t_last = edges(tj, tv)
    bj, bd, be, bv = by_block
    b_first, _ = edges(bd, bv)
    return dict(pos=pos, pos_t=pos.T, gate=g[:, :N_EXPERTS], bexp=bexp.astype(jnp.int32), hi_ok=hi_ok,
                n_used=n_used.astype(jnp.int32).reshape(1), n_blocks=n_blocks, n_eblocks=n_eblocks,
                max_pairs=max_pairs,
                disp=(bj, bd, be, b_first, bv), comb=(tj, td, te, t_first, t_last, tv))


def _dispatch_kernel(pj, pd, pe, pfirst, pvalid, h_ref, post_ref, xs_ref, *, ts):
    k = pl.program_id(0)
    row = pd[k] * MOE_BLK + lax.broadcasted_iota(jnp.int32, (MOE_BLK, 1), 0)
    hit = post_ref[pl.ds(pe[k], 1), :] == row
    got = jnp.dot(jnp.where(hit, 1.0, 0.0).astype(BF16), h_ref[...], preferred_element_type=F32).astype(BF16)

    @pl.when(pfirst[k] == 1)
    def _():
        xs_ref[...] = got

    @pl.when((pfirst[k] == 0) & (pvalid[k] == 1))
    def _():
        xs_ref[...] = xs_ref[...] + got


def _dispatch(h, plan, ts):
    n = h.shape[0]
    rows = plan['n_blocks'] * MOE_BLK
    grid_spec = pltpu.PrefetchScalarGridSpec(
        num_scalar_prefetch=5,
        grid=(plan['max_pairs'],),
        in_specs=[pl.BlockSpec((ts, D_MODEL), lambda k, pj, pd, pe, pf, pv: (pj[k], 0)),
                  pl.BlockSpec((N_EXPERTS, ts), lambda k, pj, pd, pe, pf, pv: (0, pj[k]))],
        out_specs=pl.BlockSpec((MOE_BLK, D_MODEL), lambda k, pj, pd, pe, pf, pv: (pd[k], 0)),
    )
    return pl.pallas_call(
        functools.partial(_dispatch_kernel, ts=ts),
        grid_spec=grid_spec,
        out_shape=jax.ShapeDtypeStruct((rows, D_MODEL), BF16),
        compiler_params=_cparams(("arbitrary",)),
        name="moe_dispatch",
    )(*plan['disp'], h, plan['pos_t'])


def _experts_kernel(bexp, hiok, nused, xlo_ref, xhi_ref, w1_ref, w3_ref, w2_ref, y_ref, acc_scr, *, nf):
    d = pl.program_id(0)
    f = pl.program_id(1)

    @pl.when(d < nused[0])
    def _():
        @pl.when(f == 0)
        def _():
            acc_scr[...] = jnp.zeros_like(acc_scr)

        x = jnp.concatenate([xlo_ref[...], xhi_ref[...]], axis=0)
        a = jnp.dot(x, w1_ref[...].astype(BF16), preferred_element_type=F32)
        g = jnp.dot(x, w3_ref[...].astype(BF16), preferred_element_type=F32)
        acc_scr[...] += jnp.dot((_silu(a) * g).astype(BF16), w2_ref[...].astype(BF16),
                                preferred_element_type=F32)

        @pl.when(f == nf - 1)
        def _():
            y_ref[...] = acc_scr[...].astype(y_ref.dtype)


def _experts(xs, plan, w1, w3, w2):
    dff = w1.shape[2]
    nf = 7
    tf = dff // nf
    assert tf * nf == dff and tf % LANE == 0
    per = MOE_PER_EBLK
    eblk = per * MOE_BLK

    def blk(d, nused):
        return jnp.minimum(d, nused[0] - 1)

    def fidx(d, f, nused):
        return jnp.where(d < nused[0], f, nf - 1)

    def hi_blk(d, hiok, nused):
        b = blk(d, nused)
        return per * b + hiok[b]

    grid_spec = pltpu.PrefetchScalarGridSpec(
        num_scalar_prefetch=3,
        grid=(plan['n_eblocks'], nf),
        in_specs=[pl.BlockSpec((MOE_BLK, D_MODEL), lambda d, f, be, hk, nu: (per * blk(d, nu), 0)),
                  pl.BlockSpec((MOE_BLK, D_MODEL), lambda d, f, be, hk, nu: (hi_blk(d, hk, nu), 0)),
                  pl.BlockSpec((None, D_MODEL, tf), lambda d, f, be, hk, nu: (be[blk(d, nu)], 0, fidx(d, f, nu))),
                  pl.BlockSpec((None, D_MODEL, tf), lambda d, f, be, hk, nu: (be[blk(d, nu)], 0, fidx(d, f, nu))),
                  pl.BlockSpec((None, tf, D_MODEL), lambda d, f, be, hk, nu: (be[blk(d, nu)], fidx(d, f, nu), 0))],
        out_specs=pl.BlockSpec((eblk, D_MODEL), lambda d, f, be, hk, nu: (blk(d, nu), 0)),
        scratch_shapes=[pltpu.VMEM((eblk, D_MODEL), F32)],
    )
    return pl.pallas_call(
        functools.partial(_experts_kernel, nf=nf),
        grid_spec=grid_spec,
        out_shape=jax.ShapeDtypeStruct(xs.shape, BF16),
        compiler_params=_cparams(("arbitrary", "arbitrary")),
        name="moe_experts",
    )(plan['bexp'], plan['hi_ok'], plan['n_used'], xs, xs, w1, w3, w2)


def _combine_kernel(pj, pd, pe, pfirst, plast, pvalid, ys_ref, pos_ref, gate_ref, x_ref, mod_ref, fnw_ref,
                    xo_ref, acc_scr, *, ts, final_norm):
    k = pl.program_id(0)

    @pl.when(pfirst[k] == 1)
    def _():
        acc_scr[...] = jnp.zeros_like(acc_scr)

    @pl.when(pvalid[k] == 1)
    def _():
        e = pe[k]
        lane = lax.broadcasted_iota(jnp.int32, (ts, N_EXPERTS), 1)
        pos_e = jnp.sum(jnp.where(lane == e, pos_ref[...], 0), axis=1, keepdims=True)
        gate_e = jnp.sum(jnp.where(lane == e, gate_ref[...], 0.0), axis=1, keepdims=True)
        row = pd[k] * MOE_BLK + lax.broadcasted_iota(jnp.int32, (1, MOE_BLK), 1)
        hit = pos_e == row
        got = jnp.dot(jnp.where(hit, 1.0, 0.0).astype(BF16), ys_ref[...], preferred_element_type=F32)
        acc_scr[...] += gate_e * got

    @pl.when(plast[k] == 1)
    def _():
        xo = x_ref[...] + mod_ref[5:6, :] * acc_scr[...]
        if final_norm:
            xo = xo * lax.rsqrt(jnp.mean(xo * xo, axis=-1, keepdims=True) + EPS) * fnw_ref[...]
        xo_ref[...] = xo


def _combine(ys, plan, x, mod, fnw, ts, final_norm):
    n = x.shape[0]
    tiles_per_batch = n // (mod.shape[0] * ts)
    sp = lambda fn: (lambda k, pj, pd, pe, pf, pl_, pv: fn(k, pj, pd))
    grid_spec = pltpu.PrefetchScalarGridSpec(
        num_scalar_prefetch=6,
        grid=(plan['max_pairs'],),
        in_specs=[pl.BlockSpec((MOE_BLK, D_MODEL), sp(lambda k, pj, pd: (pd[k], 0))),
                  pl.BlockSpec((ts, N_EXPERTS), sp(lambda k, pj, pd: (pj[k], 0))),
                  pl.BlockSpec((ts, N_EXPERTS), sp(lambda k, pj, pd: (pj[k], 0))),
                  pl.BlockSpec((ts, D_MODEL), sp(lambda k, pj, pd: (pj[k], 0))),
                  pl.BlockSpec((None, 8, D_MODEL), sp(lambda k, pj, pd: (pj[k] // tiles_per_batch, 0, 0))),
                  pl.BlockSpec((1, D_MODEL), sp(lambda k, pj, pd: (0, 0)))],
        out_specs=pl.BlockSpec((ts, D_MODEL), sp(lambda k, pj, pd: (pj[k], 0))),
        scratch_shapes=[pltpu.VMEM((ts, D_MODEL), F32)],
    )
    return pl.pallas_call(
        functools.partial(_combine_kernel, ts=ts, final_norm=final_norm),
        grid_spec=grid_spec,
        out_shape=jax.ShapeDtypeStruct((n, D_MODEL), F32),
        compiler_params=_cparams(("arbitrary",)),
        name="moe_combine",
    )(*plan['comb'], ys, plan['pos'], plan['gate'], x, mod, fnw)


def _moe(x, mod, nw, wr, br, w1, w3, w2, fnw, final_norm):
    b, t, _ = x.shape
    ts = _tile(t, 512)
    h, g = _router(x, mod, nw, wr, br)
    plan = _moe_plan(g.reshape(b * t, LANE), ts)
    xs = _dispatch(h.reshape(b * t, D_MODEL), plan, ts)
    ys = _experts(xs, plan, w1, w3, w2)
    out = _combine(ys, plan, x.reshape(b * t, D_MODEL), mod, fnw, ts, final_norm)
    return out.reshape(b, t, D_MODEL)


def _pad_lanes(v, fill=0.0):
    return jnp.pad(v.astype(F32), (0, LANE - v.shape[0]), constant_values=fill)[None, :]


def kernel(x, c, ctx, c_ctx, w_mod, b_mod, norm_w, w_in, s5_lam_re, s5_lam_im, s5_log_dt, s5_b_re, s5_b_im, s5_c_re, s5_c_im, s5_d, s5_w_glu, lru_conv_w, lru_conv_b, lru_w_a, lru_b_a, lru_w_x, lru_b_x, lru_lam, ssd_conv_w, ssd_conv_b, ssd_a_log, ssd_dt_bias, ssd_d, ssd_norm_w, w_br_a, w_br_b, w_br_c, w_out, ffn_w1, ffn_w3, ffn_w2, moe_w_router, moe_b_router, moe_w1, moe_w3, moe_w2, final_norm_w):
    depth = w_mod.shape[0]
    nb, t_lat, _ = x.shape
    t_ctx = ctx.shape[1]
    assert t_lat % GRID_W == 0 and t_lat % SSD_CHUNK == 0 and t_ctx % SSD_CHUNK == 0

    cond_in = jnp.zeros((8, D_MODEL), F32).at[:nb].set(c).at[nb].set(c_ctx)
    mods = _modulation(cond_in, w_mod, b_mod).reshape(depth, 8, N_MOD, D_MODEL)
    mods = jnp.pad(mods, ((0, 0), (0, 0), (0, 8 - N_MOD), (0, 0)))

    head_of = jnp.arange(SSD_INNER) // SSD_HEAD_DIM
    e_mat = (jnp.arange(LANE)[:, None] == head_of[None, :]).astype(BF16)
    s5_perm = _s5_perm()
    n_lev_lat = max(1, math.ceil(math.log2(t_lat // S5_CHUNK)))
    s5_prep = jax.vmap(functools.partial(_s5_setup, n_levels=n_lev_lat))(
        s5_lam_re, s5_lam_im, s5_log_dt, s5_b_re, s5_b_im, s5_c_re, s5_c_im)
    lru_prep = jax.vmap(_lru_gate_weights)(lru_w_a, lru_b_a, lru_w_x, lru_b_x)

    x_lat, x_ctx = x, ctx
    for l in range(depth):
        ctx_out = l < depth - 1
        last = l == depth - 1
        mod_lat = mods[l, :nb]
        mod_ctx = jnp.broadcast_to(mods[l, nb][None], (nb, 8, D_MODEL))
        nw0 = norm_w[l, 0][None, :]
        nw1 = norm_w[l, 1][None, :]

        wl = w_in[l]
        o_lx = S5_WIDTH
        o_lg = o_lx + LRU_WIDTH
        o_z = o_lg + LRU_WIDTH
        o_xbc = o_z + SSD_INNER
        o_dt = o_xbc + SSD_XBC
        o_g = o_dt + SSD_HEADS
        w_main = jnp.concatenate(
            [wl[:, :o_lx], jnp.zeros((D_MODEL, P_LX - S5_WIDTH), wl.dtype), wl[:, o_lx:o_dt], wl[:, o_g:]],
            axis=1).astype(BF16)
        w_dt = jnp.pad(wl[:, o_dt:o_g], ((0, 0), (0, LANE - SSD_HEADS))).astype(BF16)
        kcat, pm, rm, atab = (a[l] for a in s5_prep)
        wg, bg = (a[l] for a in lru_prep)
        lcw = lru_conv_w[l].astype(F32)
        lcb = lru_conv_b[l][None, :].astype(F32)
        scw = ssd_conv_w[l].astype(F32)
        scb = ssd_conv_b[l][None, :].astype(F32)
        a_log = jnp.concatenate([_pad_lanes(ssd_a_log[l, 0]), _pad_lanes(ssd_a_log[l, 1])], axis=0)
        dt_bias = jnp.concatenate([_pad_lanes(ssd_dt_bias[l, 0]), _pad_lanes(ssd_dt_bias[l, 1])], axis=0)
        dskip_x = jnp.repeat(ssd_d[l].astype(F32), SSD_HEAD_DIM)[None, :]

        def mixers(p, dt, t, w, s5_h0, lru_h0, ssd_h0f, ssd_h0b):
            y5, s5_ht = _s5_scan(_s5_chunks(p, nb, t), s5_perm, kcat, pm, rm, atab, s5_h0)
            ya = _s5_unchunk(y5, nb, t)
            yb, lru_ht = _lru_scan(p, lcw, lcb, wg, bg, lru_lam[l].astype(F32), lru_h0, w,
                                   1 if w > 1 else LRU_WIDTH // LRU_SLAB)
            yc, ssd_htf, ssd_htb = _ssd_scan(p, dt, scw, scb, a_log, dt_bias, dskip_x, e_mat, ssd_h0f, ssd_h0b)
            return ya, yb, yc, s5_ht, lru_ht, ssd_htf, ssd_htb

        p_ctx, dt_ctx = _inproj(x_ctx, mod_ctx, nw0, w_main, w_dt)
        zs5 = jnp.zeros((S5_GROUPS, nb, 4 * S5_STATE), F32)
        zlru = jnp.zeros((nb, 2, LRU_WIDTH), F32)
        zssd = jnp.zeros((nb, SSD_GROUPS * SSD_STATE, SSD_GW), F32)
        ya_c, yb_c, yc_c, s5_h, lru_h, ssd_hf, ssd_hb = mixers(p_ctx, dt_ctx, t_ctx, 1, zs5, zlru, zssd, zssd)

        p_lat, dt_lat = _inproj(x_lat, mod_lat, nw0, w_main, w_dt)
        ya, yb, yc, _, _, _, _ = mixers(p_lat, dt_lat, t_lat, GRID_W, s5_h, lru_h, ssd_hf, ssd_hb)

        mw = (s5_d[l][None, :].astype(F32), s5_w_glu[l].astype(BF16), ssd_norm_w[l][None, :].astype(F32),
              w_br_a[l].astype(BF16), w_br_b[l].astype(BF16), w_br_c[l].astype(BF16), w_out[l].astype(BF16))
        x_lat = _merge(x_lat, mod_lat, p_lat, ya, yb, yc, *mw)
        if ctx_out:
            x_ctx = _merge(x_ctx, mod_ctx, p_ctx, ya_c, yb_c, yc_c, *mw)

        fnw = final_norm_w[None, :].astype(F32)
        if l % 2 == 0:
            fw = (ffn_w1[l // 2].astype(BF16), ffn_w3[l // 2].astype(BF16), ffn_w2[l // 2].astype(BF16))
            x_lat = _ffn(x_lat, mod_lat, nw1, *fw, fnw, last)
            if ctx_out:
                x_ctx = _ffn(x_ctx, mod_ctx, nw1, *fw, fnw, False)
        else:
            wr = jnp.pad(moe_w_router[l // 2].astype(F32), ((0, 0), (0, LANE - N_EXPERTS)))
            br = _pad_lanes(moe_b_router[l // 2])
            ew = (moe_w1[l // 2], moe_w3[l // 2], moe_w2[l // 2])
            x_lat = _moe(x_lat, mod_lat, nw1, wr, br, *ew, fnw, last)
            if ctx_out:
                x_ctx = _moe(x_ctx, mod_ctx, nw1, wr, br, *ew, fnw, False)
    return x_lat
```

```python
import functools
import math

import jax
import jax.numpy as jnp
from jax import lax
from jax.experimental import pallas as pl
from jax.experimental.pallas import tpu as pltpu

F32 = jnp.float32
BF16 = jnp.bfloat16
HIGHEST = lax.Precision.HIGHEST

D_MODEL = 1024
GRID_W = 64
N_MOD = 6
EPS = 1e-6

S5_WIDTH = 768
S5_GROUP = 16
S5_GROUPS = S5_WIDTH // S5_GROUP
S5_STATE = 64
S5_CHUNK = 32
S5_CW = S5_CHUNK * S5_GROUP

LRU_WIDTH = 1024
LRU_BLOCKS = 16
LRU_BLOCK = LRU_WIDTH // LRU_BLOCKS
LRU_CONV = 4
LRU_C = 8.0
LRU_SLAB = 128

SSD_INNER = 1024
SSD_HEAD_DIM = 64
SSD_HEADS = SSD_INNER // SSD_HEAD_DIM
SSD_GROUPS = 4
SSD_HPG = SSD_HEADS // SSD_GROUPS
SSD_STATE = 128
SSD_CONV = 4
SSD_CHUNK = 128
SSD_STEP_CHUNKS = 2
SSD_XBC = SSD_INNER + 2 * SSD_GROUPS * SSD_STATE
SSD_GW = SSD_HPG * SSD_HEAD_DIM

N_EXPERTS = 8
TOP_K = 2
LANE = 128

P_U, P_LX, P_LG, P_Z, P_XBC, P_GATE = 0, 1024, 2048, 3072, 4096, 6144
P_TOTAL = 9216

VMEM_LIMIT = 56 * 1024 * 1024


def _cparams(sem):
    return pltpu.CompilerParams(dimension_semantics=sem, vmem_limit_bytes=VMEM_LIMIT)


def _tile(n, pref):
    t = min(n, pref)
    while n % t:
        t //= 2
    return t


def _sigmoid(x):
    return jax.nn.sigmoid(x)


def _silu(x):
    return x * jax.nn.sigmoid(x)


def _softplus(x):
    return jnp.maximum(x, 0.0) + jnp.log(1.0 + jnp.exp(-jnp.abs(x)))


def _split_dot(x, w_bf16):
    hi = x.astype(BF16)
    lo = (x - hi.astype(F32)).astype(BF16)
    return (jnp.dot(hi, w_bf16, preferred_element_type=F32)
            + jnp.dot(lo, w_bf16, preferred_element_type=F32))


def _rms_mod(x, nw, shift, scale):
    y = x * lax.rsqrt(jnp.mean(x * x, axis=-1, keepdims=True) + EPS) * nw
    return y * (1.0 + scale) + shift


def _mod_kernel(c_ref, w_ref, b_ref, o_ref):
    cond = _silu(c_ref[...])
    o_ref[...] = jnp.dot(cond, w_ref[...], preferred_element_type=F32, precision=HIGHEST) + b_ref[...]


def _modulation(cond_in, w_mod, b_mod):
    depth = w_mod.shape[0]
    n = w_mod.shape[2]
    tn = 1024
    return pl.pallas_call(
        _mod_kernel,
        grid=(depth, n // tn),
        in_specs=[pl.BlockSpec((8, D_MODEL), lambda l, j: (0, 0)),
                  pl.BlockSpec((None, D_MODEL, tn), lambda l, j: (l, 0, j)),
                  pl.BlockSpec((None, 1, tn), lambda l, j: (l, 0, j))],
        out_specs=pl.BlockSpec((None, 8, tn), lambda l, j: (l, 0, j)),
        out_shape=jax.ShapeDtypeStruct((depth, 8, n), F32),
        compiler_params=_cparams(("arbitrary", "arbitrary")),
        name="modulation",
    )(cond_in, w_mod, b_mod.reshape(depth, 1, n))


def _inproj_kernel(x_ref, mod_ref, nw_ref, w_ref, wdt_ref, p_ref, dt_ref, h_scr):
    @pl.when(pl.program_id(2) == 0)
    def _():
        h = _rms_mod(x_ref[...], nw_ref[...], mod_ref[0:1, :], mod_ref[1:2, :]).astype(BF16)
        h_scr[...] = h
        dt_ref[...] = jnp.dot(h, wdt_ref[...], preferred_element_type=F32)

    p_ref[...] = jnp.dot(h_scr[...], w_ref[...], preferred_element_type=F32).astype(p_ref.dtype)


def _inproj(x, mod, nw, w, wdt):
    b, t, _ = x.shape
    tm = _tile(t, 1024)
    tn = 2304
    return pl.pallas_call(
        _inproj_kernel,
        grid=(b, t // tm, P_TOTAL // tn),
        in_specs=[pl.BlockSpec((None, tm, D_MODEL), lambda b_, i, j: (b_, i, 0)),
                  pl.BlockSpec((None, 8, D_MODEL), lambda b_, i, j: (b_, 0, 0)),
                  pl.BlockSpec((1, D_MODEL), lambda b_, i, j: (0, 0)),
                  pl.BlockSpec((D_MODEL, tn), lambda b_, i, j: (0, j)),
                  pl.BlockSpec((D_MODEL, LANE), lambda b_, i, j: (0, 0))],
        out_specs=[pl.BlockSpec((None, tm, tn), lambda b_, i, j: (b_, i, j)),
                   pl.BlockSpec((None, tm, LANE), lambda b_, i, j: (b_, i, 0))],
        out_shape=[jax.ShapeDtypeStruct((b, t, P_TOTAL), BF16),
                   jax.ShapeDtypeStruct((b, t, LANE), F32)],
        scratch_shapes=[pltpu.VMEM((tm, D_MODEL), BF16)],
        compiler_params=_cparams(("arbitrary", "arbitrary", "arbitrary")),
        name="inproj",
    )(x, mod, nw, w, wdt)


def _s5_setup(lam_re, lam_im, log_dt, b_re, b_im, c_re, c_im, n_levels):
    t = S5_CHUNK
    dt = jnp.exp(log_dt.astype(F32))[..., None]
    lre, lim = lam_re.astype(F32), lam_im.astype(F32)
    are, aim = lre * dt, lim * dt
    k = jnp.arange(t + 1, dtype=F32)[:, None]
    mag = jnp.exp(are[:, :, None, :] * k)
    pw_re = mag * jnp.cos(aim[:, :, None, :] * k)
    pw_im = mag * jnp.sin(aim[:, :, None, :] * k)
    a_re, a_im = pw_re[:, :, 1], pw_im[:, :, 1]
    den = lre * lre + lim * lim
    q_re = ((a_re - 1.0) * lre + a_im * lim) / den
    q_im = (a_im * lre - (a_re - 1.0) * lim) / den
    bb_re = q_re[..., None] * b_re - q_im[..., None] * b_im
    bb_im = q_re[..., None] * b_im + q_im[..., None] * b_re
    cr, ci = c_re.astype(F32), c_im.astype(F32)

    cb_re = cr[..., None] * bb_re[:, :, None] - ci[..., None] * bb_im[:, :, None]
    cb_im = cr[..., None] * bb_im[:, :, None] + ci[..., None] * bb_re[:, :, None]
    kern = (jnp.einsum('dgkp,dgjpi->dgkji', pw_re[:, :, :t], cb_re, precision=HIGHEST)
            - jnp.einsum('dgkp,dgjpi->dgkji', pw_im[:, :, :t], cb_im, precision=HIGHEST))
    kf, kb = kern[0], kern[1]
    taps = jnp.concatenate([kb[:, 1:][:, ::-1], (kf[:, 0] + kb[:, 0])[:, None], kf[:, 1:]], axis=1)
    kcat = taps.transpose(0, 3, 1, 2).reshape(S5_GROUPS, S5_GROUP, (2 * t - 1) * S5_GROUP)
    kcat = jnp.pad(kcat, ((0, 0), (0, 0), (0, S5_GROUP)))

    pf_re, pf_im = pw_re[0, :, :t][:, ::-1], pw_im[0, :, :t][:, ::-1]
    pb_re, pb_im = pw_re[1, :, :t], pw_im[1, :, :t]

    def in_map(p_re, p_im, d):
        w_re = p_re[:, :, None, :] * bb_re[d].transpose(0, 2, 1)[:, None] - p_im[:, :, None, :] * bb_im[d].transpose(0, 2, 1)[:, None]
        w_im = p_re[:, :, None, :] * bb_im[d].transpose(0, 2, 1)[:, None] + p_im[:, :, None, :] * bb_re[d].transpose(0, 2, 1)[:, None]
        return w_re, w_im

    wf_re, wf_im = in_map(pf_re, pf_im, 0)
    wb_re, wb_im = in_map(pb_re, pb_im, 1)
    pm = jnp.concatenate([wf_re, wf_im, wb_re, wb_im], axis=-1).reshape(S5_GROUPS, S5_CW, 4 * S5_STATE)

    def out_map(p_re, p_im, d):
        c_r = cr[d].transpose(0, 2, 1)[:, :, None, :]
        c_i = ci[d].transpose(0, 2, 1)[:, :, None, :]
        e_r = p_re.transpose(0, 2, 1)[..., None]
        e_i = p_im.transpose(0, 2, 1)[..., None]
        m_re = c_r * e_r - c_i * e_i
        m_im = c_r * e_i + c_i * e_r
        return m_re, -m_im

    rf_re, rf_im = out_map(pw_re[0, :, 1:t + 1], pw_im[0, :, 1:t + 1], 0)
    rb_re, rb_im = out_map(pw_re[1, :, 1:t + 1][:, ::-1], pw_im[1, :, 1:t + 1][:, ::-1], 1)
    rm = jnp.concatenate([rf_re, rf_im, rb_re, rb_im], axis=1).reshape(S5_GROUPS, 4 * S5_STATE, S5_CW)

    lev = (t * 2.0 ** jnp.arange(n_levels, dtype=F32))[:, None]
    lmag = jnp.exp(are[:, :, None, :] * lev)
    l_re = lmag * jnp.cos(aim[:, :, None, :] * lev)
    l_im = lmag * jnp.sin(aim[:, :, None, :] * lev)
    row_a = jnp.concatenate([l_re[0], l_re[0], l_re[1], l_re[1]], axis=-1)
    row_b = jnp.concatenate([-l_im[0], l_im[0], -l_im[1], l_im[1]], axis=-1)
    atab = jnp.stack([row_a, row_b], axis=2)
    return kcat, pm.astype(BF16), rm.astype(BF16), atab


def _shift_rows(h, d, down):
    n = h.shape[0]
    if d >= n:
        return jnp.zeros_like(h)
    row = lax.broadcasted_iota(jnp.int32, h.shape, 0)
    if down:
        return jnp.where(row >= d, pltpu.roll(h, d, 0), 0.0)
    return jnp.where(row < n - d, pltpu.roll(h, n - d, 0), 0.0)


S5_SLAB_GROUPS = LANE // S5_GROUP
S5_PIECES = S5_CW // LANE
S5_PER_PIECE = LANE // S5_GROUP


def _s5_kernel(u_ref, perm_ref, kc_ref, pm_ref, rm_ref, at_ref, h0_ref, y_ref, ht_ref,
               x_scr, yg_scr, tz_scr, hin_scr, *, nb, nc, n_levels):
    half = 2 * S5_STATE
    row = lax.broadcasted_iota(jnp.int32, (nc, half), 0)

    for v in range(S5_PIECES):
        a = jnp.concatenate(
            [jnp.concatenate([u_ref[b, S5_PER_PIECE * v + sl] for b in range(nb)], axis=0)
             for sl in range(S5_PER_PIECE)], axis=1)
        xp = jnp.dot(a.astype(BF16), perm_ref[...], preferred_element_type=F32).astype(BF16)
        for gl in range(S5_SLAB_GROUPS):
            x_scr[gl, :, v * LANE:(v + 1) * LANE] = xp[:, gl * LANE:(gl + 1) * LANE]

    def group(g, _):
        kc = kc_ref[g]
        for s in range(S5_CHUNK):
            off = (S5_CHUNK - 1 - s) * S5_GROUP
            tz_scr[s * S5_GROUP:(s + 1) * S5_GROUP, :] = kc[:, off:off + S5_CW].astype(BF16)
        x = x_scr[g]
        s_all = jnp.dot(x, pm_ref[g], preferred_element_type=F32)

        def cmul(h, k, lo):
            a = at_ref[g, k, 0:1, lo:lo + half]
            b = at_ref[g, k, 1:2, lo:lo + half]
            return a * h + b * pltpu.roll(h, S5_STATE, 1)

        for b in range(nb):
            for d in range(2):
                lo = d * half
                h = s_all[b * nc:(b + 1) * nc, lo:lo + half]
                h0 = h0_ref[g, b:b + 1, lo:lo + half]
                edge = 0 if d == 0 else nc - 1
                h = h + jnp.where(row == edge, cmul(jnp.broadcast_to(h0, (nc, half)), 0, lo), 0.0)
                for k in range(n_levels):
                    if (1 << k) < nc:
                        h = h + cmul(_shift_rows(h, 1 << k, d == 0), k, lo)
                ht_ref[g, b:b + 1, lo:lo + half] = h[nc - 1 - edge:nc - edge]
                hin = _shift_rows(h, 1, d == 0)
                hin = jnp.where(row == edge, jnp.broadcast_to(h0, (nc, half)), hin)
                hin_scr[b * nc:(b + 1) * nc, lo:lo + half] = hin

        y = jnp.dot(x, tz_scr[...], preferred_element_type=F32)
        y = y + jnp.dot(hin_scr[...].astype(BF16), rm_ref[g], preferred_element_type=F32)
        yg_scr[g] = y.astype(BF16)
        return 0
    lax.fori_loop(0, S5_SLAB_GROUPS, group, 0)

    for v in range(S5_PIECES):
        cat = jnp.concatenate([yg_scr[gl, :, v * LANE:(v + 1) * LANE] for gl in range(S5_SLAB_GROUPS)], axis=1)
        yp = jnp.dot(cat, perm_ref[...], preferred_element_type=F32)
        for tl in range(S5_PER_PIECE):
            for b in range(nb):
                y_ref[b, S5_PER_PIECE * v + tl] = yp[b * nc:(b + 1) * nc, tl * LANE:(tl + 1) * LANE].astype(y_ref.dtype)


def _s5_scan(u, perm, kcat, pm, rm, atab, h0):
    nb, _, nc, _ = u.shape
    rows = nb * nc
    n_levels = atab.shape[1]
    gs = S5_SLAB_GROUPS
    kern = functools.partial(_s5_kernel, nb=nb, nc=nc, n_levels=n_levels)
    tok_spec = pl.BlockSpec((nb, S5_CHUNK, nc, LANE), lambda i: (0, 0, 0, i))
    return pl.pallas_call(
        kern,
        grid=(S5_GROUPS // gs,),
        in_specs=[tok_spec,
                  pl.BlockSpec((gs * LANE, gs * LANE), lambda i: (0, 0)),
                  pl.BlockSpec((gs, S5_GROUP, 2 * S5_CW), lambda i: (i, 0, 0)),
                  pl.BlockSpec((gs, S5_CW, 4 * S5_STATE), lambda i: (i, 0, 0)),
                  pl.BlockSpec((gs, 4 * S5_STATE, S5_CW), lambda i: (i, 0, 0)),
                  pl.BlockSpec((gs, n_levels, 2, 4 * S5_STATE), lambda i: (i, 0, 0, 0)),
                  pl.BlockSpec((gs, nb, 4 * S5_STATE), lambda i: (i, 0, 0))],
        out_specs=[tok_spec,
                   pl.BlockSpec((gs, nb, 4 * S5_STATE), lambda i: (i, 0, 0))],
        out_shape=[jax.ShapeDtypeStruct(u.shape, u.dtype),
                   jax.ShapeDtypeStruct((S5_GROUPS, nb, 4 * S5_STATE), F32)],
        scratch_shapes=[pltpu.VMEM((gs, rows, S5_CW), BF16), pltpu.VMEM((gs, rows, S5_CW), BF16),
                        pltpu.VMEM((S5_CW, S5_CW), BF16), pltpu.VMEM((rows, 4 * S5_STATE), F32)],
        compiler_params=_cparams(("arbitrary",)),
        name="s5_scan",
    )(u, perm, kcat, pm, rm, atab, h0)


def _s5_chunks(p, nb, t):
    nc = t // S5_CHUNK
    u = p[:, :, P_U:P_U + S5_WIDTH].reshape(nb, nc, S5_CHUNK, S5_WIDTH).transpose(0, 2, 1, 3)
    return u if nc % 16 == 0 else u.astype(F32)


def _s5_unchunk(y, nb, t):
    return y.transpose(0, 2, 1, 3).reshape(nb, t, S5_WIDTH)


def _s5_perm():
    a = jnp.arange(S5_SLAB_GROUPS * LANE)
    s, g, i = a // LANE, (a % LANE) // S5_GROUP, a % S5_GROUP
    dst = g * LANE + s * S5_GROUP + i
    return (dst[:, None] == a[None, :]).astype(BF16)


def _lru_kernel(x_ref, cw_ref, cb_ref, wg_ref, bg_ref, lam_ref, h0_ref, out_ref, ht_ref,
                xp_scr, xc_scr, af_scr, bf_scr, ab_scr, bb_scr, cf_scr, cr_scr, *, w, s, pad_top, ch, ns):
    l = w * s
    n_ch = l // ch
    slabs = range(ns)
    lanes = [slice(k * LRU_SLAB, (k + 1) * LRU_SLAB) for k in slabs]
    zero_slab = jnp.zeros((w, LRU_SLAB), F32)
    one_slab = jnp.ones((w, LRU_SLAB), F32)

    for k in slabs:
        xp_scr[k, 0:pad_top, :] = jnp.zeros((pad_top, LRU_SLAB), F32)
        xp_scr[k, pad_top + l:pad_top + l + pad_top, :] = jnp.zeros((pad_top, LRU_SLAB), F32)

    def copy_body(i, _):
        r0 = pl.multiple_of(i * ch, ch)
        for k in slabs:
            xp_scr[k, pl.ds(pad_top + r0, ch), :] = x_ref[pl.ds(r0, ch), lanes[k]].astype(F32)
        return 0
    lax.fori_loop(0, n_ch, copy_body, 0)

    def conv_body(i, _):
        r0 = pl.multiple_of(i * ch, ch)
        for k in slabs:
            acc = cb_ref[:, lanes[k]] + cw_ref[2:3, lanes[k]] * xp_scr[k, pl.ds(pad_top + r0, ch), :]
            acc = acc + cw_ref[0:1, lanes[k]] * xp_scr[k, pl.ds(pad_top + r0 - 2 * w, ch), :]
            acc = acc + cw_ref[1:2, lanes[k]] * xp_scr[k, pl.ds(pad_top + r0 - w, ch), :]
            acc = acc + cw_ref[3:4, lanes[k]] * xp_scr[k, pl.ds(pad_top + r0 + w, ch), :]
            xc_scr[k, pl.ds(r0, ch), :] = acc
        return 0
    lax.fori_loop(0, n_ch, conv_body, 0)

    if w > 1:
        def prev_col(v):
            return _shift_rows(v, 1, True)

        def next_col(v):
            return _shift_rows(v, 1, False)

        for k in slabs:
            x_last = xp_scr[k, pad_top + (s - 1) * w:pad_top + s * w, :]
            x_last2 = xp_scr[k, pad_top + (s - 2) * w:pad_top + (s - 1) * w, :]
            x_first = xp_scr[k, pad_top:pad_top + w, :]
            w0, w1, w3 = cw_ref[0:1, lanes[k]], cw_ref[1:2, lanes[k]], cw_ref[3:4, lanes[k]]
            xc_scr[k, 0:w, :] = xc_scr[k, 0:w, :] + w0 * prev_col(x_last2) + w1 * prev_col(x_last)
            xc_scr[k, w:2 * w, :] = xc_scr[k, w:2 * w, :] + w0 * prev_col(x_last)
            xc_scr[k, (s - 1) * w:s * w, :] = xc_scr[k, (s - 1) * w:s * w, :] + w3 * next_col(x_first)

    c_half = (0.5 * LRU_C) * -_softplus(-lam_ref[...])

    def coef_body(i, _):
        r0 = pl.multiple_of(i * ch, ch)
        for k in slabs:
            xc = xc_scr[k, pl.ds(r0, ch), :]
            hx = 0.5 * xc
            g = jnp.dot(xc.astype(BF16), wg_ref[k], preferred_element_type=F32) + bg_ref[k]
            for d, (a_scr, b_scr) in enumerate(((af_scr, bf_scr), (ab_scr, bb_scr))):
                t_r = jnp.tanh(g[:, (2 * d) * LRU_SLAB:(2 * d + 1) * LRU_SLAB])
                t_i = jnp.tanh(g[:, (2 * d + 1) * LRU_SLAB:(2 * d + 2) * LRU_SLAB])
                c = c_half[d:d + 1, lanes[k]]
                a = jnp.exp(c * t_r + c)
                a_scr[k, pl.ds(r0, ch), :] = a
                b_scr[k, pl.ds(r0, ch), :] = jnp.sqrt(1.0 - a * a) * (hx * (t_i + 1.0))
        return 0
    lax.fori_loop(0, n_ch, coef_body, 0)

    def scan_body(r, carry):
        rf = pl.multiple_of(r * w, w)
        rb = pl.multiple_of((s - 1 - r) * w, w)
        out = []
        for k in slabs:
            hf, pf, hb, pb = carry[k]
            a = af_scr[k, pl.ds(rf, w), :]
            hf = a * hf + bf_scr[k, pl.ds(rf, w), :]
            pf = a * pf
            bf_scr[k, pl.ds(rf, w), :] = hf
            af_scr[k, pl.ds(rf, w), :] = pf
            a = ab_scr[k, pl.ds(rb, w), :]
            hb = a * hb + bb_scr[k, pl.ds(rb, w), :]
            pb = a * pb
            bb_scr[k, pl.ds(rb, w), :] = hb
            ab_scr[k, pl.ds(rb, w), :] = pb
            out.append((hf, pf, hb, pb))
        return tuple(out)
    lax.fori_loop(0, s, scan_body, tuple((zero_slab, one_slab, zero_slab, one_slab) for _ in slabs))

    def carry_f(c, carry):
        out = []
        for k in slabs:
            cf_scr[k, pl.ds(c, 1), :] = carry[k]
            out.append(bf_scr[k, pl.ds((s - 1) * w + c, 1), :] + af_scr[k, pl.ds((s - 1) * w + c, 1), :] * carry[k])
        return tuple(out)
    fin = lax.fori_loop(0, w, carry_f, tuple(h0_ref[0:1, lanes[k]] for k in slabs))
    for k in slabs:
        ht_ref[0:1, lanes[k]] = fin[k]

    def carry_b(i, carry):
        c = w - 1 - i
        out = []
        for k in slabs:
            cr_scr[k, pl.ds(c, 1), :] = carry[k]
            out.append(bb_scr[k, pl.ds(c, 1), :] + ab_scr[k, pl.ds(c, 1), :] * carry[k])
        return tuple(out)
    fin = lax.fori_loop(0, w, carry_b, tuple(h0_ref[1:2, lanes[k]] for k in slabs))
    for k in slabs:
        ht_ref[1:2, lanes[k]] = fin[k]

    def out_body(r, _):
        r0 = pl.multiple_of(r * w, w)
        for k in slabs:
            y = (bf_scr[k, pl.ds(r0, w), :] + af_scr[k, pl.ds(r0, w), :] * cf_scr[k]
                 + bb_scr[k, pl.ds(r0, w), :] + ab_scr[k, pl.ds(r0, w), :] * cr_scr[k])
            bf_scr[k, pl.ds(r0, w), :] = y
        return 0
    lax.fori_loop(0, s, out_body, 0)

    def store_body(i, _):
        r0 = pl.multiple_of(i * ch, ch)
        for k in slabs:
            out_ref[pl.ds(r0, ch), lanes[k]] = bf_scr[k, pl.ds(r0, ch), :].astype(out_ref.dtype)
        return 0
    lax.fori_loop(0, n_ch, store_body, 0)


def _lru_scan(p, cw, cb, wg, bg, lam, h0, w, ns):
    b, t, _ = p.shape
    s = t // w
    assert s * w == t and s >= 4
    ch = _tile(t, 256)
    pad_top = max(2 * w, 8)
    cwid = ns * LRU_SLAB
    col0 = P_LX // cwid
    kern = functools.partial(_lru_kernel, w=w, s=s, pad_top=pad_top, ch=ch, ns=ns)
    return pl.pallas_call(
        kern,
        grid=(b, LRU_WIDTH // cwid),
        in_specs=[pl.BlockSpec((None, t, cwid), lambda b_, k: (b_, 0, col0 + k)),
                  pl.BlockSpec((LRU_CONV, cwid), lambda b_, k: (0, k)),
                  pl.BlockSpec((1, cwid), lambda b_, k: (0, k)),
                  pl.BlockSpec((ns, LRU_SLAB, 4 * LRU_SLAB), lambda b_, k: (k, 0, 0)),
                  pl.BlockSpec((ns, 1, 4 * LRU_SLAB), lambda b_, k: (k, 0, 0)),
                  pl.BlockSpec((2, cwid), lambda b_, k: (0, k)),
                  pl.BlockSpec((None, 2, cwid), lambda b_, k: (b_, 0, k))],
        out_specs=[pl.BlockSpec((None, t, cwid), lambda b_, k: (b_, 0, k)),
                   pl.BlockSpec((None, 2, cwid), lambda b_, k: (b_, 0, k))],
        out_shape=[jax.ShapeDtypeStruct((b, t, LRU_WIDTH), BF16),
                   jax.ShapeDtypeStruct((b, 2, LRU_WIDTH), F32)],
        scratch_shapes=[pltpu.VMEM((ns, t + 2 * pad_top, LRU_SLAB), F32),
                        pltpu.VMEM((ns, t, LRU_SLAB), F32),
                        pltpu.VMEM((ns, t, LRU_SLAB), F32), pltpu.VMEM((ns, t, LRU_SLAB), F32),
                        pltpu.VMEM((ns, t, LRU_SLAB), F32), pltpu.VMEM((ns, t, LRU_SLAB), F32),
                        pltpu.VMEM((ns, w, LRU_SLAB), F32), pltpu.VMEM((ns, w, LRU_SLAB), F32)],
        compiler_params=_cparams(("arbitrary", "arbitrary")),
        name="lru_scan",
    )(p, cw, cb, wg, bg, lam, h0)


def _lru_gate_weights(w_a, b_a, w_x, b_x):
    n_slab = LRU_WIDTH // LRU_SLAB
    per = LRU_SLAB // LRU_BLOCK

    def slab_diag(wm):
        wm = wm.reshape(n_slab, per, LRU_BLOCK, LRU_BLOCK)
        eye = jnp.eye(per, dtype=wm.dtype)
        return jnp.einsum('spkj,pq->spkqj', wm, eye).reshape(n_slab, LRU_SLAB, LRU_SLAB)

    wg = jnp.concatenate([slab_diag(w_a[0]), slab_diag(w_x[0]), slab_diag(w_a[1]), slab_diag(w_x[1])], axis=2)
    bg = jnp.concatenate([b_a[0].reshape(n_slab, 1, LRU_SLAB), b_x[0].reshape(n_slab, 1, LRU_SLAB),
                          b_a[1].reshape(n_slab, 1, LRU_SLAB), b_x[1].reshape(n_slab, 1, LRU_SLAB)], axis=2)
    return (0.5 * wg).astype(BF16), (0.5 * bg).astype(F32)


def _ssd_core(xs, bm, cm, dtr, alog, dbias, e_ref, h_scr, y_scr, reverse):
    q = SSD_CHUNK
    a = -jnp.exp(alog)
    dt = _softplus(dtr + dbias)
    adt = dt * a
    row = lax.broadcasted_iota(jnp.int32, (q, q), 0)
    col = lax.broadcasted_iota(jnp.int32, (q, q), 1)
    tri = (col >= row) if reverse else (col <= row)
    cs = jnp.dot(tri.astype(F32), adt, preferred_element_type=F32, precision=HIGHEST)
    cs_t = cs.T
    dt_t = dt.T
    end = 0 if reverse else q - 1
    tot = cs[end:end + 1, :]
    e = e_ref[...]
    w_x = _split_dot(dt * jnp.exp(tot - cs), e)
    ecs_x = _split_dot(jnp.exp(cs), e)
    xsb = xs.astype(BF16)
    xw = (xs * w_x).astype(BF16)
    decay_row = ecs_x[end:end + 1, :]

    for g in range(SSD_GROUPS):
        cg = cm[:, g * SSD_STATE:(g + 1) * SSD_STATE]
        bg = bm[:, g * SSD_STATE:(g + 1) * SSD_STATE]
        sc = lax.dot_general(cg, bg, (((1,), (1,)), ((), ())), preferred_element_type=F32)
        for j in range(SSD_HPG):
            hd = g * SSD_HPG + j
            cols = slice(hd * SSD_HEAD_DIM, (hd + 1) * SSD_HEAD_DIM)
            diff = cs[:, hd:hd + 1] - cs_t[hd:hd + 1, :]
            lm = jnp.where(tri, jnp.exp(diff), 0.0) * dt_t[hd:hd + 1, :]
            y_scr[:, cols] = jnp.dot((sc * lm).astype(BF16), xsb[:, cols], preferred_element_type=F32)
        ch = slice(g * SSD_GW, (g + 1) * SSD_GW)
        rows = slice(g * SSD_STATE, (g + 1) * SSD_STATE)
        h_old = h_scr[rows, :]
        y_off = jnp.dot(cg, h_old.astype(BF16), preferred_element_type=F32) * ecs_x[:, ch]
        y_scr[:, ch] = y_scr[:, ch] + y_off
        st = lax.dot_general(bg, xw[:, ch], (((0,), (0,)), ((), ())), preferred_element_type=F32)
        h_scr[rows, :] = decay_row[:, ch] * h_old + st
    return y_scr[...]


def _ssd_fwd_kernel(x_ref, xprev_ref, xnext_ref, dt_ref, cw_ref, cb_ref, alog_ref, dbias_ref, e_ref, h0_ref,
                    y_ref, xc_ref, ht_ref, h_scr, y_scr, xcf_scr, *, nc):
    c = pl.program_id(1)

    @pl.when(c == 0)
    def _():
        h_scr[...] = h0_ref[...]

    q = x_ref.shape[0]
    x = x_ref[...].astype(F32)
    prev = jnp.where(c > 0, xprev_ref[...].astype(F32), 0.0)
    nxt = jnp.where(c < nc - 1, xnext_ref[...].astype(F32), 0.0)
    row8 = lax.broadcasted_iota(jnp.int32, (8, SSD_XBC), 0)

    def behind(k):
        xr = pltpu.roll(x, k, 0)
        top = jnp.where(row8 < k, pltpu.roll(prev, k, 0)[0:8], xr[0:8])
        return jnp.concatenate([top, xr[8:]], axis=0)

    xm2, xm1 = behind(2), behind(1)
    xr = pltpu.roll(x, q - 1, 0)
    bottom = jnp.where(row8 == 7, pltpu.roll(nxt, 15, 0)[8:16], xr[q - 8:q])
    xp1 = jnp.concatenate([xr[:q - 8], bottom], axis=0)
    conv = (cb_ref[...] + cw_ref[0:1, :] * xm2 + cw_ref[1:2, :] * xm1
            + cw_ref[2:3, :] * x + cw_ref[3:4, :] * xp1)
    xc = _silu(conv)
    xc_ref[...] = xc.astype(xc_ref.dtype)
    xcf_scr[...] = xc
    for k in range(q // SSD_CHUNK):
        r = slice(k * SSD_CHUNK, (k + 1) * SSD_CHUNK)
        xs = xcf_scr[r, :SSD_INNER]
        bm = xcf_scr[r, SSD_INNER:SSD_INNER + SSD_GROUPS * SSD_STATE].astype(BF16)
        cm = xcf_scr[r, SSD_INNER + SSD_GROUPS * SSD_STATE:].astype(BF16)
        y_ref[r, :] = _ssd_core(xs, bm, cm, dt_ref[r, :], alog_ref[...], dbias_ref[...], e_ref, h_scr, y_scr, False)

    @pl.when(c == nc - 1)
    def _():
        ht_ref[...] = h_scr[...]


def _ssd_bwd_kernel(xc_ref, dt_ref, yf_ref, alog_ref, dbias_ref, dskip_ref, e_ref, h0_ref,
                    y_ref, ht_ref, h_scr, y_scr, *, nc):
    c = pl.program_id(1)

    @pl.when(c == 0)
    def _():
        h_scr[...] = h0_ref[...]

    for k in reversed(range(xc_ref.shape[0] // SSD_CHUNK)):
        r = slice(k * SSD_CHUNK, (k + 1) * SSD_CHUNK)
        xs = xc_ref[r, :SSD_INNER].astype(F32)
        bm = xc_ref[r, SSD_INNER:SSD_INNER + SSD_GROUPS * SSD_STATE]
        cm = xc_ref[r, SSD_INNER + SSD_GROUPS * SSD_STATE:]
        y = _ssd_core(xs, bm, cm, dt_ref[r, :], alog_ref[...], dbias_ref[...], e_ref, h_scr, y_scr, True)
        y_ref[r, :] = y + yf_ref[r, :] + dskip_ref[...] * xs

    @pl.when(c == nc - 1)
    def _():
        ht_ref[...] = h_scr[...]


def _ssd_scan(p, dt, cw, cb, a_log, dt_bias, dskip_x, e_mat, h0f, h0b):
    b, t, _ = p.shape
    q = SSD_STEP_CHUNKS * SSD_CHUNK
    nc = t // q
    assert nc * q == t
    xbc_blk = P_XBC // SSD_XBC
    n16 = t // 16
    hs = (SSD_GROUPS * SSD_STATE, SSD_GW)
    const2 = lambda b_, c: (0, 0)
    state_spec = pl.BlockSpec((None,) + hs, lambda b_, c: (b_, 0, 0))

    yf, xc, htf = pl.pallas_call(
        functools.partial(_ssd_fwd_kernel, nc=nc),
        grid=(b, nc),
        in_specs=[pl.BlockSpec((None, q, SSD_XBC), lambda b_, c: (b_, c, xbc_blk)),
                  pl.BlockSpec((None, 16, SSD_XBC),
                               lambda b_, c: (b_, jnp.maximum(c * (q // 16) - 1, 0), xbc_blk)),
                  pl.BlockSpec((None, 16, SSD_XBC),
                               lambda b_, c: (b_, jnp.minimum((c + 1) * (q // 16), n16 - 1), xbc_blk)),
                  pl.BlockSpec((None, q, LANE), lambda b_, c: (b_, c, 0)),
                  pl.BlockSpec((SSD_CONV, SSD_XBC), const2),
                  pl.BlockSpec((1, SSD_XBC), const2),
                  pl.BlockSpec((1, LANE), const2),
                  pl.BlockSpec((1, LANE), const2),
                  pl.BlockSpec((LANE, SSD_INNER), const2),
                  state_spec],
        out_specs=[pl.BlockSpec((None, q, SSD_INNER), lambda b_, c: (b_, c, 0)),
                   pl.BlockSpec((None, q, SSD_XBC), lambda b_, c: (b_, c, 0)),
                   state_spec],
        out_shape=[jax.ShapeDtypeStruct((b, t, SSD_INNER), F32),
                   jax.ShapeDtypeStruct((b, t, SSD_XBC), BF16),
                   jax.ShapeDtypeStruct((b,) + hs, F32)],
        scratch_shapes=[pltpu.VMEM(hs, F32), pltpu.VMEM((SSD_CHUNK, SSD_INNER), F32),
                        pltpu.VMEM((q, SSD_XBC), F32)],
        compiler_params=_cparams(("arbitrary", "arbitrary")),
        name="ssd_fwd",
    )(p, p, p, dt, cw, cb, a_log[0:1], dt_bias[0:1], e_mat, h0f)

    rev = lambda b_, c: (b_, nc - 1 - c, 0)
    y, htb = pl.pallas_call(
        functools.partial(_ssd_bwd_kernel, nc=nc),
        grid=(b, nc),
        in_specs=[pl.BlockSpec((None, q, SSD_XBC), rev),
                  pl.BlockSpec((None, q, LANE), rev),
                  pl.BlockSpec((None, q, SSD_INNER), rev),
                  pl.BlockSpec((1, LANE), const2),
                  pl.BlockSpec((1, LANE), const2),
                  pl.BlockSpec((1, SSD_INNER), const2),
                  pl.BlockSpec((LANE, SSD_INNER), const2),
                  state_spec],
        out_specs=[pl.BlockSpec((None, q, SSD_INNER), rev), state_spec],
        out_shape=[jax.ShapeDtypeStruct((b, t, SSD_INNER), F32),
                   jax.ShapeDtypeStruct((b,) + hs, F32)],
        scratch_shapes=[pltpu.VMEM(hs, F32), pltpu.VMEM((SSD_CHUNK, SSD_INNER), F32)],
        compiler_params=_cparams(("arbitrary", "arbitrary")),
        name="ssd_bwd",
    )(xc, dt, yf, a_log[1:2], dt_bias[1:2], dskip_x, e_mat, h0b)
    return y, htf, htb


def _merge_kernel(x_ref, mod_ref, u_ref, ya_ref, yb_ref, g_ref, yc_ref, z_ref, gates_ref,
                  d5_ref, wglu_ref, snw_ref, wa_ref, wb_ref, wc_ref, wo_ref, xo_ref):
    ya = jax.nn.gelu(ya_ref[...].astype(F32) + d5_ref[...] * u_ref[...].astype(F32))
    ya = ya * _sigmoid(jnp.dot(ya.astype(BF16), wglu_ref[...], preferred_element_type=F32))
    yb = yb_ref[...].astype(F32) * jax.nn.gelu(g_ref[...].astype(F32))
    yc = yc_ref[...] * _silu(z_ref[...].astype(F32))
    yc = yc * lax.rsqrt(jnp.mean(yc * yc, axis=-1, keepdims=True) + EPS) * snw_ref[...]
    mix = _sigmoid(gates_ref[:, 0:D_MODEL].astype(F32)) * jnp.dot(
        ya.astype(BF16), wa_ref[...], preferred_element_type=F32)
    mix = mix + _sigmoid(gates_ref[:, D_MODEL:2 * D_MODEL].astype(F32)) * jnp.dot(
        yb.astype(BF16), wb_ref[...], preferred_element_type=F32)
    mix = mix + _sigmoid(gates_ref[:, 2 * D_MODEL:3 * D_MODEL].astype(F32)) * jnp.dot(
        yc.astype(BF16), wc_ref[...], preferred_element_type=F32)
    y = jnp.dot(mix.astype(BF16), wo_ref[...], preferred_element_type=F32)
    xo_ref[...] = x_ref[...] + mod_ref[2:3, :] * y


def _merge(x, mod, p, ya, yb, yc, d5, wglu, snw, wa, wb, wc, wo):
    b, t, _ = x.shape
    tm = _tile(t, 512)
    tok = lambda width, blk: pl.BlockSpec((None, tm, width), lambda b_, i: (b_, i, blk))
    full = lambda r, c_: pl.BlockSpec((r, c_), lambda b_, i: (0, 0))
    return pl.pallas_call(
        _merge_kernel,
        grid=(b, t // tm),
        in_specs=[tok(D_MODEL, 0),
                  pl.BlockSpec((None, 8, D_MODEL), lambda b_, i: (b_, 0, 0)),
                  tok(S5_WIDTH, P_U // S5_WIDTH),
                  tok(S5_WIDTH, 0),
                  tok(LRU_WIDTH, 0),
                  tok(LRU_WIDTH, P_LG // LRU_WIDTH),
                  tok(SSD_INNER, 0),
                  tok(SSD_INNER, P_Z // SSD_INNER),
                  tok(3 * D_MODEL, P_GATE // (3 * D_MODEL)),
                  full(1, S5_WIDTH), full(S5_WIDTH, S5_WIDTH), full(1, SSD_INNER),
                  full(S5_WIDTH, D_MODEL), full(LRU_WIDTH, D_MODEL), full(SSD_INNER, D_MODEL),
                  full(D_MODEL, D_MODEL)],
        out_specs=tok(D_MODEL, 0),
        out_shape=jax.ShapeDtypeStruct((b, t, D_MODEL), F32),
        compiler_params=_cparams(("arbitrary", "arbitrary")),
        name="merge",
    )(x, mod, p, ya, yb, p, yc, p, p, d5, wglu, snw, wa, wb, wc, wo)


def _ffn_kernel(x_ref, mod_ref, nw_ref, w1_ref, w3_ref, w2_ref, fnw_ref, xo_ref, h_scr, acc_scr, *, nf, final_norm):
    f = pl.program_id(2)

    @pl.when(f == 0)
    def _():
        h_scr[...] = _rms_mod(x_ref[...], nw_ref[...], mod_ref[3:4, :], mod_ref[4:5, :]).astype(BF16)
        acc_scr[...] = jnp.zeros_like(acc_scr)

    h = h_scr[...]
    a = jnp.dot(h, w1_ref[...], preferred_element_type=F32)
    g = jnp.dot(h, w3_ref[...], preferred_element_type=F32)
    acc_scr[...] += jnp.dot((_silu(a) * g).astype(BF16), w2_ref[...], preferred_element_type=F32)

    @pl.when(f == nf - 1)
    def _():
        xo = x_ref[...] + mod_ref[5:6, :] * acc_scr[...]
        if final_norm:
            xo = xo * lax.rsqrt(jnp.mean(xo * xo, axis=-1, keepdims=True) + EPS) * fnw_ref[...]
        xo_ref[...] = xo


def _ffn(x, mod, nw, w1, w3, w2, fnw, final_norm):
    b, t, _ = x.shape
    dff = w1.shape[1]
    tm = _tile(t, 512)
    tf = dff // 2
    nf = dff // tf
    return pl.pallas_call(
        functools.partial(_ffn_kernel, nf=nf, final_norm=final_norm),
        grid=(b, t // tm, nf),
        in_specs=[pl.BlockSpec((None, tm, D_MODEL), lambda b_, i, f: (b_, i, 0)),
                  pl.BlockSpec((None, 8, D_MODEL), lambda b_, i, f: (b_, 0, 0)),
                  pl.BlockSpec((1, D_MODEL), lambda b_, i, f: (0, 0)),
                  pl.BlockSpec((D_MODEL, tf), lambda b_, i, f: (0, f)),
                  pl.BlockSpec((D_MODEL, tf), lambda b_, i, f: (0, f)),
                  pl.BlockSpec((tf, D_MODEL), lambda b_, i, f: (f, 0)),
                  pl.BlockSpec((1, D_MODEL), lambda b_, i, f: (0, 0))],
        out_specs=pl.BlockSpec((None, tm, D_MODEL), lambda b_, i, f: (b_, i, 0)),
        out_shape=jax.ShapeDtypeStruct((b, t, D_MODEL), F32),
        scratch_shapes=[pltpu.VMEM((tm, D_MODEL), BF16), pltpu.VMEM((tm, D_MODEL), F32)],
        compiler_params=_cparams(("arbitrary", "arbitrary", "arbitrary")),
        name="ffn",
    )(x, mod, nw, w1, w3, w2, fnw)


MOE_BLK = 512
MOE_PER_EBLK = 2
SEL_LANE = 8


def _router_kernel(x_ref, mod_ref, nw_ref, wr_ref, br_ref, h_ref, g_ref):
    h = _rms_mod(x_ref[...], nw_ref[...], mod_ref[3:4, :], mod_ref[4:5, :])
    h_ref[...] = h.astype(BF16)
    logits = jnp.dot(h, wr_ref[...], preferred_element_type=F32, precision=HIGHEST) + br_ref[...]
    lane = lax.broadcasted_iota(jnp.int32, logits.shape, 1)
    logits = jnp.where(lane < N_EXPERTS, logits, -jnp.inf)
    t1 = jnp.max(logits, axis=-1, keepdims=True)
    i1 = jnp.min(jnp.where(logits == t1, lane, LANE), axis=-1, keepdims=True)
    rest = jnp.where(lane == i1, -jnp.inf, logits)
    t2 = jnp.max(rest, axis=-1, keepdims=True)
    i2 = jnp.min(jnp.where(rest == t2, lane, LANE), axis=-1, keepdims=True)
    e2 = jnp.exp(t2 - t1)
    den = 1.0 + e2
    gate = jnp.where(lane == i1, 1.0 / den, 0.0) + jnp.where(lane == i2, e2 / den, 0.0)
    flag = jnp.where((lane == i1 + SEL_LANE) | (lane == i2 + SEL_LANE), 1.0, 0.0)
    g_ref[...] = gate + flag


def _router(x, mod, nw, wr, br):
    b, t, _ = x.shape
    tm = _tile(t, 512)
    return pl.pallas_call(
        _router_kernel,
        grid=(b, t // tm),
        in_specs=[pl.BlockSpec((None, tm, D_MODEL), lambda b_, i: (b_, i, 0)),
                  pl.BlockSpec((None, 8, D_MODEL), lambda b_, i: (b_, 0, 0)),
                  pl.BlockSpec((1, D_MODEL), lambda b_, i: (0, 0)),
                  pl.BlockSpec((D_MODEL, LANE), lambda b_, i: (0, 0)),
                  pl.BlockSpec((1, LANE), lambda b_, i: (0, 0))],
        out_specs=[pl.BlockSpec((None, tm, D_MODEL), lambda b_, i: (b_, i, 0)),
                   pl.BlockSpec((None, tm, LANE), lambda b_, i: (b_, i, 0))],
        out_shape=[jax.ShapeDtypeStruct((b, t, D_MODEL), BF16),
                   jax.ShapeDtypeStruct((b, t, LANE), F32)],
        compiler_params=_cparams(("arbitrary", "arbitrary")),
        name="router",
    )(x, mod, nw, wr, br)


def _moe_plan(g, ts):
    n = g.shape[0]
    blk = MOE_BLK
    eblk = MOE_PER_EBLK * MOE_BLK
    nt = n // ts
    n_eblocks = TOP_K * n // eblk + N_EXPERTS
    n_blocks = n_eblocks * MOE_PER_EBLK
    max_pairs = n_blocks + N_EXPERTS * nt
    sel = g[:, SEL_LANE:SEL_LANE + N_EXPERTS] > 0.5
    cum = jnp.cumsum(sel.astype(jnp.int32), axis=0)
    cnt = cum[-1]
    neb = (cnt + eblk - 1) // eblk
    eb_end = jnp.cumsum(neb)
    eb_beg = eb_end - neb
    seg_off = eb_beg * eblk
    n_used = eb_end[-1]
    pos = jnp.where(sel, seg_off[None, :] + cum - 1, -1).astype(jnp.int32)
    eb = jnp.arange(n_eblocks)
    bexp = jnp.minimum(jnp.sum(eb_end[None, :] <= eb[:, None], axis=1), N_EXPERTS - 1)

    tile_end = cum[ts - 1::ts]
    tile_beg = jnp.concatenate([jnp.zeros((1, N_EXPERTS), jnp.int32), tile_end[:-1]], axis=0)
    d_lo = (seg_off[None, :] + tile_beg) // blk
    d_hi = (seg_off[None, :] + tile_end - 1) // blk
    npairs = jnp.where(tile_end > tile_beg, d_hi - d_lo + 1, 0)

    def expand(cnt_flat, cell_j, cell_e):
        cend = jnp.cumsum(cnt_flat)
        total = cend[-1]
        k = jnp.minimum(jnp.arange(max_pairs), total - 1)
        cell = jnp.sum(cend[None, :] <= k[:, None], axis=1)
        i = k - (cend[cell] - cnt_flat[cell])
        j, e = cell_j[cell], cell_e[cell]
        d = d_lo[j, e] + i
        valid = (jnp.arange(max_pairs) < total).astype(jnp.int32)
        return j.astype(jnp.int32), d.astype(jnp.int32), e.astype(jnp.int32), valid

    jj, ee = jnp.meshgrid(jnp.arange(nt), jnp.arange(N_EXPERTS), indexing='ij')
    by_tile = expand(npairs.reshape(-1), jj.reshape(-1), ee.reshape(-1))

    blk_e = jnp.repeat(bexp, MOE_PER_EBLK)
    q0 = jnp.arange(n_blocks) * blk - seg_off[blk_e]
    q1 = jnp.minimum(cnt[blk_e], q0 + blk) - 1
    ends = tile_end.T[blk_e]
    has = q0 < cnt[blk_e]
    j_lo = jnp.where(has, jnp.sum(ends <= q0[:, None], axis=1), 0)
    j_hi = jnp.where(has, jnp.sum(ends <= q1[:, None], axis=1), -1)

    def edges(key, valid):
        prev = jnp.concatenate([jnp.full((1,), -1, jnp.int32), key[:-1]])
        nxt = jnp.concatenate([key[1:], jnp.full((1,), -1, jnp.int32)])
        nxt_valid = jnp.concatenate([valid[1:], jnp.zeros((1,), jnp.int32)])
        first = ((key != prev) & (valid == 1)).astype(jnp.int32)
        last = (((key != nxt) | (nxt_valid == 0)) & (valid == 1)).astype(jnp.int32)
        return first, last

    tj, td, te, tv = by_tile
    t_first, t_last = edges(tj, tv)
    return dict(pos=pos, pos_t=pos.T, gate=g[:, :N_EXPERTS], bexp=bexp.astype(jnp.int32),
                n_used=n_used.astype(jnp.int32).reshape(1), n_blocks=n_blocks, n_eblocks=n_eblocks,
                max_pairs=max_pairs,
                disp=(blk_e.astype(jnp.int32), j_lo.astype(jnp.int32), j_hi.astype(jnp.int32)),
                comb=(tj, td, te, t_first, t_last, tv))


def _dispatch_kernel(be, jlo, jhi, h_ref, post_ref, xs_ref, acc_scr, *, ts):
    d = pl.program_id(0)
    e = be[d]
    row = d * MOE_BLK + lax.broadcasted_iota(jnp.int32, (MOE_BLK, 1), 0)
    acc_scr[...] = jnp.zeros_like(acc_scr)

    def tile(j, _):
        t0 = pl.multiple_of(j * ts, ts)
        hit = post_ref[pl.ds(e, 1), pl.ds(t0, ts)] == row
        acc_scr[...] += jnp.dot(jnp.where(hit, 1.0, 0.0).astype(BF16), h_ref[pl.ds(t0, ts), :],
                                preferred_element_type=F32)
        return 0
    lax.fori_loop(jlo[d], jhi[d] + 1, tile, 0)
    xs_ref[...] = acc_scr[...].astype(xs_ref.dtype)


def _dispatch(h, plan, ts):
    n = h.shape[0]
    rows = plan['n_blocks'] * MOE_BLK
    grid_spec = pltpu.PrefetchScalarGridSpec(
        num_scalar_prefetch=3,
        grid=(plan['n_blocks'],),
        in_specs=[pl.BlockSpec((n, D_MODEL), lambda d, be, lo, hi: (0, 0), pipeline_mode=pl.Buffered(1)),
                  pl.BlockSpec((N_EXPERTS, n), lambda d, be, lo, hi: (0, 0), pipeline_mode=pl.Buffered(1))],
        out_specs=pl.BlockSpec((MOE_BLK, D_MODEL), lambda d, be, lo, hi: (d, 0)),
        scratch_shapes=[pltpu.VMEM((MOE_BLK, D_MODEL), F32)],
    )
    return pl.pallas_call(
        functools.partial(_dispatch_kernel, ts=ts),
        grid_spec=grid_spec,
        out_shape=jax.ShapeDtypeStruct((rows, D_MODEL), BF16),
        compiler_params=_cparams(("arbitrary",)),
        name="moe_dispatch",
    )(*plan['disp'], h, plan['pos_t'])


def _experts_kernel(bexp, nused, x_ref, w1_ref, w3_ref, w2_ref, y_ref, acc_scr, *, nf):
    d = pl.program_id(0)
    f = pl.program_id(1)

    @pl.when(d < nused[0])
    def _():
        @pl.when(f == 0)
        def _():
            acc_scr[...] = jnp.zeros_like(acc_scr)

        x = x_ref[...]
        a = jnp.dot(x, w1_ref[...].astype(BF16), preferred_element_type=F32)
        g = jnp.dot(x, w3_ref[...].astype(BF16), preferred_element_type=F32)
        acc_scr[...] += jnp.dot((_silu(a) * g).astype(BF16), w2_ref[...].astype(BF16),
                                preferred_element_type=F32)

        @pl.when(f == nf - 1)
        def _():
            y_ref[...] = acc_scr[...].astype(y_ref.dtype)


def _experts(xs, plan, w1, w3, w2):
    dff = w1.shape[2]
    nf = 4
    tf = dff // nf
    assert tf * nf == dff and tf % LANE == 0
    eblk = MOE_PER_EBLK * MOE_BLK

    def blk(d, nused):
        return jnp.minimum(d, nused[0] - 1)

    def fidx(d, f, nused):
        return jnp.where(d < nused[0], f, nf - 1)

    grid_spec = pltpu.PrefetchScalarGridSpec(
        num_scalar_prefetch=2,
        grid=(plan['n_eblocks'], nf),
        in_specs=[pl.BlockSpec((eblk, D_MODEL), lambda d, f, be, nu: (blk(d, nu), 0)),
                  pl.BlockSpec((None, D_MODEL, tf), lambda d, f, be, nu: (be[blk(d, nu)], 0, fidx(d, f, nu))),
                  pl.BlockSpec((None, D_MODEL, tf), lambda d, f, be, nu: (be[blk(d, nu)], 0, fidx(d, f, nu))),
                  pl.BlockSpec((None, tf, D_MODEL), lambda d, f, be, nu: (be[blk(d, nu)], fidx(d, f, nu), 0))],
        out_specs=pl.BlockSpec((eblk, D_MODEL), lambda d, f, be, nu: (blk(d, nu), 0)),
        scratch_shapes=[pltpu.VMEM((eblk, D_MODEL), F32)],
    )
    return pl.pallas_call(
        functools.partial(_experts_kernel, nf=nf),
        grid_spec=grid_spec,
        out_shape=jax.ShapeDtypeStruct(xs.shape, BF16),
        compiler_params=_cparams(("arbitrary", "arbitrary")),
        name="moe_experts",
    )(plan['bexp'], plan['n_used'], xs, w1, w3, w2)


def _combine_kernel(pj, pd, pe, pfirst, plast, pvalid, ys_ref, pos_ref, gate_ref, x_ref, mod_ref, fnw_ref,
                    xo_ref, acc_scr, *, ts, final_norm):
    k = pl.program_id(0)

    @pl.when(pfirst[k] == 1)
    def _():
        acc_scr[...] = jnp.zeros_like(acc_scr)

    @pl.when(pvalid[k] == 1)
    def _():
        e = pe[k]
        lane = lax.broadcasted_iota(jnp.int32, (ts, N_EXPERTS), 1)
        pos_e = jnp.sum(jnp.where(lane == e, pos_ref[...], 0), axis=1, keepdims=True)
        gate_e = jnp.sum(jnp.where(lane == e, gate_ref[...], 0.0), axis=1, keepdims=True)
        row = pd[k] * MOE_BLK + lax.broadcasted_iota(jnp.int32, (1, MOE_BLK), 1)
        hit = pos_e == row
        got = jnp.dot(jnp.where(hit, 1.0, 0.0).astype(BF16), ys_ref[...], preferred_element_type=F32)
        acc_scr[...] += gate_e * got

    @pl.when(plast[k] == 1)
    def _():
        xo = x_ref[...] + mod_ref[5:6, :] * acc_scr[...]
        if final_norm:
            xo = xo * lax.rsqrt(jnp.mean(xo * xo, axis=-1, keepdims=True) + EPS) * fnw_ref[...]
        xo_ref[...] = xo


def _combine(ys, plan, x, mod, fnw, ts, final_norm):
    n = x.shape[0]
    tiles_per_batch = n // (mod.shape[0] * ts)
    sp = lambda fn: (lambda k, pj, pd, pe, pf, pl_, pv: fn(k, pj, pd))
    grid_spec = pltpu.PrefetchScalarGridSpec(
        num_scalar_prefetch=6,
        grid=(plan['max_pairs'],),
        in_specs=[pl.BlockSpec((MOE_BLK, D_MODEL), sp(lambda k, pj, pd: (pd[k], 0))),
                  pl.BlockSpec((ts, N_EXPERTS), sp(lambda k, pj, pd: (pj[k], 0))),
                  pl.BlockSpec((ts, N_EXPERTS), sp(lambda k, pj, pd: (pj[k], 0))),
                  pl.BlockSpec((ts, D_MODEL), sp(lambda k, pj, pd: (pj[k], 0))),
                  pl.BlockSpec((None, 8, D_MODEL), sp(lambda k, pj, pd: (pj[k] // tiles_per_batch, 0, 0))),
                  pl.BlockSpec((1, D_MODEL), sp(lambda k, pj, pd: (0, 0)))],
        out_specs=pl.BlockSpec((ts, D_MODEL), sp(lambda k, pj, pd: (pj[k], 0))),
        scratch_shapes=[pltpu.VMEM((ts, D_MODEL), F32)],
    )
    return pl.pallas_call(
        functools.partial(_combine_kernel, ts=ts, final_norm=final_norm),
        grid_spec=grid_spec,
        out_shape=jax.ShapeDtypeStruct((n, D_MODEL), F32),
        compiler_params=_cparams(("arbitrary",)),
        name="moe_combine",
    )(*plan['comb'], ys, plan['pos'], plan['gate'], x, mod, fnw)


def _moe(x, mod, nw, wr, br, w1, w3, w2, fnw, final_norm):
    b, t, _ = x.shape
    ts = _tile(t, 512)
    h, g = _router(x, mod, nw, wr, br)
    plan = _moe_plan(g.reshape(b * t, LANE), ts)
    xs = _dispatch(h.reshape(b * t, D_MODEL), plan, ts)
    ys = _experts(xs, plan, w1, w3, w2)
    out = _combine(ys, plan, x.reshape(b * t, D_MODEL), mod, fnw, ts, final_norm)
    return out.reshape(b, t, D_MODEL)


def _pad_lanes(v, fill=0.0):
    return jnp.pad(v.astype(F32), (0, LANE - v.shape[0]), constant_values=fill)[None, :]


def kernel(x, c, ctx, c_ctx, w_mod, b_mod, norm_w, w_in, s5_lam_re, s5_lam_im, s5_log_dt, s5_b_re, s5_b_im, s5_c_re, s5_c_im, s5_d, s5_w_glu, lru_conv_w, lru_conv_b, lru_w_a, lru_b_a, lru_w_x, lru_b_x, lru_lam, ssd_conv_w, ssd_conv_b, ssd_a_log, ssd_dt_bias, ssd_d, ssd_norm_w, w_br_a, w_br_b, w_br_c, w_out, ffn_w1, ffn_w3, ffn_w2, moe_w_router, moe_b_router, moe_w1, moe_w3, moe_w2, final_norm_w):
    depth = w_mod.shape[0]
    nb, t_lat, _ = x.shape
    t_ctx = ctx.shape[1]
    assert t_lat % GRID_W == 0 and t_lat % SSD_CHUNK == 0 and t_ctx % SSD_CHUNK == 0

    cond_in = jnp.zeros((8, D_MODEL), F32).at[:nb].set(c).at[nb].set(c_ctx)
    mods = _modulation(cond_in, w_mod, b_mod).reshape(depth, 8, N_MOD, D_MODEL)
    mods = jnp.pad(mods, ((0, 0), (0, 0), (0, 8 - N_MOD), (0, 0)))

    head_of = jnp.arange(SSD_INNER) // SSD_HEAD_DIM
    e_mat = (jnp.arange(LANE)[:, None] == head_of[None, :]).astype(BF16)
    s5_perm = _s5_perm()
    n_lev_lat = max(1, math.ceil(math.log2(t_lat // S5_CHUNK)))
    s5_prep = jax.vmap(functools.partial(_s5_setup, n_levels=n_lev_lat))(
        s5_lam_re, s5_lam_im, s5_log_dt, s5_b_re, s5_b_im, s5_c_re, s5_c_im)
    lru_prep = jax.vmap(_lru_gate_weights)(lru_w_a, lru_b_a, lru_w_x, lru_b_x)

    x_lat, x_ctx = x, ctx
    for l in range(depth):
        ctx_out = l < depth - 1
        last = l == depth - 1
        mod_lat = mods[l, :nb]
        mod_ctx = jnp.broadcast_to(mods[l, nb][None], (nb, 8, D_MODEL))
        nw0 = norm_w[l, 0][None, :]
        nw1 = norm_w[l, 1][None, :]

        wl = w_in[l]
        o_lx = S5_WIDTH
        o_lg = o_lx + LRU_WIDTH
        o_z = o_lg + LRU_WIDTH
        o_xbc = o_z + SSD_INNER
        o_dt = o_xbc + SSD_XBC
        o_g = o_dt + SSD_HEADS
        w_main = jnp.concatenate(
            [wl[:, :o_lx], jnp.zeros((D_MODEL, P_LX - S5_WIDTH), wl.dtype), wl[:, o_lx:o_dt], wl[:, o_g:]],
            axis=1).astype(BF16)
        w_dt = jnp.pad(wl[:, o_dt:o_g], ((0, 0), (0, LANE - SSD_HEADS))).astype(BF16)
        kcat, pm, rm, atab = (a[l] for a in s5_prep)
        wg, bg = (a[l] for a in lru_prep)
        lcw = lru_conv_w[l].astype(F32)
        lcb = lru_conv_b[l][None, :].astype(F32)
        scw = ssd_conv_w[l].astype(F32)
        scb = ssd_conv_b[l][None, :].astype(F32)
        a_log = jnp.concatenate([_pad_lanes(ssd_a_log[l, 0]), _pad_lanes(ssd_a_log[l, 1])], axis=0)
        dt_bias = jnp.concatenate([_pad_lanes(ssd_dt_bias[l, 0]), _pad_lanes(ssd_dt_bias[l, 1])], axis=0)
        dskip_x = jnp.repeat(ssd_d[l].astype(F32), SSD_HEAD_DIM)[None, :]

        def mixers(p, dt, t, w, s5_h0, lru_h0, ssd_h0f, ssd_h0b):
            y5, s5_ht = _s5_scan(_s5_chunks(p, nb, t), s5_perm, kcat, pm, rm, atab, s5_h0)
            ya = _s5_unchunk(y5, nb, t)
            yb, lru_ht = _lru_scan(p, lcw, lcb, wg, bg, lru_lam[l].astype(F32), lru_h0, w,
                                   1 if w > 1 else LRU_WIDTH // LRU_SLAB)
            yc, ssd_htf, ssd_htb = _ssd_scan(p, dt, scw, scb, a_log, dt_bias, dskip_x, e_mat, ssd_h0f, ssd_h0b)
            return ya, yb, yc, s5_ht, lru_ht, ssd_htf, ssd_htb

        p_ctx, dt_ctx = _inproj(x_ctx, mod_ctx, nw0, w_main, w_dt)
        zs5 = jnp.zeros((S5_GROUPS, nb, 4 * S5_STATE), F32)
        zlru = jnp.zeros((nb, 2, LRU_WIDTH), F32)
        zssd = jnp.zeros((nb, SSD_GROUPS * SSD_STATE, SSD_GW), F32)
        ya_c, yb_c, yc_c, s5_h, lru_h, ssd_hf, ssd_hb = mixers(p_ctx, dt_ctx, t_ctx, 1, zs5, zlru, zssd, zssd)

        p_lat, dt_lat = _inproj(x_lat, mod_lat, nw0, w_main, w_dt)
        ya, yb, yc, _, _, _, _ = mixers(p_lat, dt_lat, t_lat, GRID_W, s5_h, lru_h, ssd_hf, ssd_hb)

        mw = (s5_d[l][None, :].astype(F32), s5_w_glu[l].astype(BF16), ssd_norm_w[l][None, :].astype(F32),
              w_br_a[l].astype(BF16), w_br_b[l].astype(BF16), w_br_c[l].astype(BF16), w_out[l].astype(BF16))
        x_lat = _merge(x_lat, mod_lat, p_lat, ya, yb, yc, *mw)
        if ctx_out:
            x_ctx = _merge(x_ctx, mod_ctx, p_ctx, ya_c, yb_c, yc_c, *mw)

        fnw = final_norm_w[None, :].astype(F32)
        if l % 2 == 0:
            fw = (ffn_w1[l // 2].astype(BF16), ffn_w3[l // 2].astype(BF16), ffn_w2[l // 2].astype(BF16))
            x_lat = _ffn(x_lat, mod_lat, nw1, *fw, fnw, last)
            if ctx_out:
                x_ctx = _ffn(x_ctx, mod_ctx, nw1, *fw, fnw, False)
        else:
            wr = jnp.pad(moe_w_router[l // 2].astype(F32), ((0, 0), (0, LANE - N_EXPERTS)))
            br = _pad_lanes(moe_b_router[l // 2])
            ew = (moe_w1[l // 2], moe_w3[l // 2], moe_w2[l // 2])
            x_lat = _moe(x_lat, mod_lat, nw1, wr, br, *ew, fnw, last)
            if ctx_out:
                x_ctx = _moe(x_ctx, mod_ctx, nw1, wr, br, *ew, fnw, False)
    return x_lat
```

```python
import functools
import math

import jax
import jax.numpy as jnp
from jax import lax
from jax.experimental import pallas as pl
from jax.experimental.pallas import tpu as pltpu

F32 = jnp.float32
BF16 = jnp.bfloat16
HIGHEST = lax.Precision.HIGHEST

D_MODEL = 1024
GRID_W = 64
N_MOD = 6
EPS = 1e-6

S5_WIDTH = 768
S5_GROUP = 16
S5_GROUPS = S5_WIDTH // S5_GROUP
S5_STATE = 64
S5_CHUNK = 32
S5_CW = S5_CHUNK * S5_GROUP

LRU_WIDTH = 1024
LRU_BLOCKS = 16
LRU_BLOCK = LRU_WIDTH // LRU_BLOCKS
LRU_CONV = 4
LRU_C = 8.0
LRU_SLAB = 128

SSD_INNER = 1024
SSD_HEAD_DIM = 64
SSD_HEADS = SSD_INNER // SSD_HEAD_DIM
SSD_GROUPS = 4
SSD_HPG = SSD_HEADS // SSD_GROUPS
SSD_STATE = 128
SSD_CONV = 4
SSD_CHUNK = 128
SSD_STEP_CHUNKS = 2
SSD_XBC = SSD_INNER + 2 * SSD_GROUPS * SSD_STATE
SSD_GW = SSD_HPG * SSD_HEAD_DIM

N_EXPERTS = 8
TOP_K = 2
LANE = 128

P_U, P_LX, P_LG, P_Z, P_XBC, P_GATE = 0, 1024, 2048, 3072, 4096, 6144
P_TOTAL = 9216

VMEM_LIMIT = 56 * 1024 * 1024


def _cparams(sem):
    return pltpu.CompilerParams(dimension_semantics=sem, vmem_limit_bytes=VMEM_LIMIT)


def _tile(n, pref):
    t = min(n, pref)
    while n % t:
        t //= 2
    return t


def _sigmoid(x):
    return jax.nn.sigmoid(x)


def _silu(x):
    return x * jax.nn.sigmoid(x)


def _softplus(x):
    return jnp.maximum(x, 0.0) + jnp.log(1.0 + jnp.exp(-jnp.abs(x)))


def _split_dot(x, w_bf16):
    hi = x.astype(BF16)
    lo = (x - hi.astype(F32)).astype(BF16)
    return (jnp.dot(hi, w_bf16, preferred_element_type=F32)
            + jnp.dot(lo, w_bf16, preferred_element_type=F32))


def _rms_mod(x, nw, shift, scale):
    y = x * lax.rsqrt(jnp.mean(x * x, axis=-1, keepdims=True) + EPS) * nw
    return y * (1.0 + scale) + shift


def _mod_kernel(c_ref, w_ref, b_ref, o_ref):
    cond = _silu(c_ref[...])
    o_ref[...] = jnp.dot(cond, w_ref[...], preferred_element_type=F32, precision=HIGHEST) + b_ref[...]


def _modulation(cond_in, w_mod, b_mod):
    depth = w_mod.shape[0]
    n = w_mod.shape[2]
    tn = 1024
    return pl.pallas_call(
        _mod_kernel,
        grid=(depth, n // tn),
        in_specs=[pl.BlockSpec((8, D_MODEL), lambda l, j: (0, 0)),
                  pl.BlockSpec((None, D_MODEL, tn), lambda l, j: (l, 0, j)),
                  pl.BlockSpec((None, 1, tn), lambda l, j: (l, 0, j))],
        out_specs=pl.BlockSpec((None, 8, tn), lambda l, j: (l, 0, j)),
        out_shape=jax.ShapeDtypeStruct((depth, 8, n), F32),
        compiler_params=_cparams(("arbitrary", "arbitrary")),
        name="modulation",
    )(cond_in, w_mod, b_mod.reshape(depth, 1, n))


def _inproj_kernel(x_ref, mod_ref, nw_ref, w_ref, wdt_ref, p_ref, dt_ref, h_scr):
    @pl.when(pl.program_id(2) == 0)
    def _():
        h = _rms_mod(x_ref[...], nw_ref[...], mod_ref[0:1, :], mod_ref[1:2, :]).astype(BF16)
        h_scr[...] = h
        dt_ref[...] = jnp.dot(h, wdt_ref[...], preferred_element_type=F32)

    p_ref[...] = jnp.dot(h_scr[...], w_ref[...], preferred_element_type=F32).astype(p_ref.dtype)


def _inproj(x, mod, nw, w, wdt):
    b, t, _ = x.shape
    tm = _tile(t, 1024)
    tn = 2304
    return pl.pallas_call(
        _inproj_kernel,
        grid=(b, t // tm, P_TOTAL // tn),
        in_specs=[pl.BlockSpec((None, tm, D_MODEL), lambda b_, i, j: (b_, i, 0)),
                  pl.BlockSpec((None, 8, D_MODEL), lambda b_, i, j: (b_, 0, 0)),
                  pl.BlockSpec((1, D_MODEL), lambda b_, i, j: (0, 0)),
                  pl.BlockSpec((D_MODEL, tn), lambda b_, i, j: (0, j)),
                  pl.BlockSpec((D_MODEL, LANE), lambda b_, i, j: (0, 0))],
        out_specs=[pl.BlockSpec((None, tm, tn), lambda b_, i, j: (b_, i, j)),
                   pl.BlockSpec((None, tm, LANE), lambda b_, i, j: (b_, i, 0))],
        out_shape=[jax.ShapeDtypeStruct((b, t, P_TOTAL), BF16),
                   jax.ShapeDtypeStruct((b, t, LANE), F32)],
        scratch_shapes=[pltpu.VMEM((tm, D_MODEL), BF16)],
        compiler_params=_cparams(("arbitrary", "arbitrary", "arbitrary")),
        name="inproj",
    )(x, mod, nw, w, wdt)


def _s5_setup(lam_re, lam_im, log_dt, b_re, b_im, c_re, c_im, n_levels):
    t = S5_CHUNK
    dt = jnp.exp(log_dt.astype(F32))[..., None]
    lre, lim = lam_re.astype(F32), lam_im.astype(F32)
    are, aim = lre * dt, lim * dt
    k = jnp.arange(t + 1, dtype=F32)[:, None]
    mag = jnp.exp(are[:, :, None, :] * k)
    pw_re = mag * jnp.cos(aim[:, :, None, :] * k)
    pw_im = mag * jnp.sin(aim[:, :, None, :] * k)
    a_re, a_im = pw_re[:, :, 1], pw_im[:, :, 1]
    den = lre * lre + lim * lim
    q_re = ((a_re - 1.0) * lre + a_im * lim) / den
    q_im = (a_im * lre - (a_re - 1.0) * lim) / den
    bb_re = q_re[..., None] * b_re - q_im[..., None] * b_im
    bb_im = q_re[..., None] * b_im + q_im[..., None] * b_re
    cr, ci = c_re.astype(F32), c_im.astype(F32)

    cb_re = cr[..., None] * bb_re[:, :, None] - ci[..., None] * bb_im[:, :, None]
    cb_im = cr[..., None] * bb_im[:, :, None] + ci[..., None] * bb_re[:, :, None]
    kern = (jnp.einsum('dgkp,dgjpi->dgkji', pw_re[:, :, :t], cb_re, precision=HIGHEST)
            - jnp.einsum('dgkp,dgjpi->dgkji', pw_im[:, :, :t], cb_im, precision=HIGHEST))
    kf, kb = kern[0], kern[1]
    taps = jnp.concatenate([kb[:, 1:][:, ::-1], (kf[:, 0] + kb[:, 0])[:, None], kf[:, 1:]], axis=1)
    kcat = taps.transpose(0, 3, 1, 2).reshape(S5_GROUPS, S5_GROUP, (2 * t - 1) * S5_GROUP)
    kcat = jnp.pad(kcat, ((0, 0), (0, 0), (0, S5_GROUP)))

    pf_re, pf_im = pw_re[0, :, :t][:, ::-1], pw_im[0, :, :t][:, ::-1]
    pb_re, pb_im = pw_re[1, :, :t], pw_im[1, :, :t]

    def in_map(p_re, p_im, d):
        w_re = p_re[:, :, None, :] * bb_re[d].transpose(0, 2, 1)[:, None] - p_im[:, :, None, :] * bb_im[d].transpose(0, 2, 1)[:, None]
        w_im = p_re[:, :, None, :] * bb_im[d].transpose(0, 2, 1)[:, None] + p_im[:, :, None, :] * bb_re[d].transpose(0, 2, 1)[:, None]
        return w_re, w_im

    wf_re, wf_im = in_map(pf_re, pf_im, 0)
    wb_re, wb_im = in_map(pb_re, pb_im, 1)
    pm = jnp.concatenate([wf_re, wf_im, wb_re, wb_im], axis=-1).reshape(S5_GROUPS, S5_CW, 4 * S5_STATE)

    def out_map(p_re, p_im, d):
        c_r = cr[d].transpose(0, 2, 1)[:, :, None, :]
        c_i = ci[d].transpose(0, 2, 1)[:, :, None, :]
        e_r = p_re.transpose(0, 2, 1)[..., None]
        e_i = p_im.transpose(0, 2, 1)[..., None]
        m_re = c_r * e_r - c_i * e_i
        m_im = c_r * e_i + c_i * e_r
        return m_re, -m_im

    rf_re, rf_im = out_map(pw_re[0, :, 1:t + 1], pw_im[0, :, 1:t + 1], 0)
    rb_re, rb_im = out_map(pw_re[1, :, 1:t + 1][:, ::-1], pw_im[1, :, 1:t + 1][:, ::-1], 1)
    rm = jnp.concatenate([rf_re, rf_im, rb_re, rb_im], axis=1).reshape(S5_GROUPS, 4 * S5_STATE, S5_CW)

    lev = (t * 2.0 ** jnp.arange(n_levels, dtype=F32))[:, None]
    lmag = jnp.exp(are[:, :, None, :] * lev)
    l_re = lmag * jnp.cos(aim[:, :, None, :] * lev)
    l_im = lmag * jnp.sin(aim[:, :, None, :] * lev)
    row_a = jnp.concatenate([l_re[0], l_re[0], l_re[1], l_re[1]], axis=-1)
    row_b = jnp.concatenate([-l_im[0], l_im[0], -l_im[1], l_im[1]], axis=-1)
    atab = jnp.stack([row_a, row_b], axis=2)
    return kcat, pm.astype(BF16), rm.astype(BF16), atab


def _shift_rows(h, d, down):
    n = h.shape[0]
    if d >= n:
        return jnp.zeros_like(h)
    row = lax.broadcasted_iota(jnp.int32, h.shape, 0)
    if down:
        return jnp.where(row >= d, pltpu.roll(h, d, 0), 0.0)
    return jnp.where(row < n - d, pltpu.roll(h, n - d, 0), 0.0)


S5_SLAB_GROUPS = LANE // S5_GROUP
S5_PIECES = S5_CW // LANE
S5_PER_PIECE = LANE // S5_GROUP


def _s5_kernel(u_ref, perm_ref, kc_ref, pm_ref, rm_ref, at_ref, h0_ref, y_ref, ht_ref,
               x_scr, yg_scr, tz_scr, hin_scr, *, nb, nc, n_levels):
    half = 2 * S5_STATE
    row = lax.broadcasted_iota(jnp.int32, (nc, half), 0)

    for v in range(S5_PIECES):
        a = jnp.concatenate(
            [jnp.concatenate([u_ref[b, S5_PER_PIECE * v + sl] for b in range(nb)], axis=0)
             for sl in range(S5_PER_PIECE)], axis=1)
        xp = jnp.dot(a.astype(BF16), perm_ref[...], preferred_element_type=F32).astype(BF16)
        for gl in range(S5_SLAB_GROUPS):
            x_scr[gl, :, v * LANE:(v + 1) * LANE] = xp[:, gl * LANE:(gl + 1) * LANE]

    def group(g, _):
        kc = kc_ref[g]
        for s in range(S5_CHUNK):
            off = (S5_CHUNK - 1 - s) * S5_GROUP
            tz_scr[s * S5_GROUP:(s + 1) * S5_GROUP, :] = kc[:, off:off + S5_CW].astype(BF16)
        x = x_scr[g]
        s_all = jnp.dot(x, pm_ref[g], preferred_element_type=F32)

        def cmul(h, k, lo):
            a = at_ref[g, k, 0:1, lo:lo + half]
            b = at_ref[g, k, 1:2, lo:lo + half]
            return a * h + b * pltpu.roll(h, S5_STATE, 1)

        for b in range(nb):
            for d in range(2):
                lo = d * half
                h = s_all[b * nc:(b + 1) * nc, lo:lo + half]
                h0 = h0_ref[g, b:b + 1, lo:lo + half]
                edge = 0 if d == 0 else nc - 1
                h = h + jnp.where(row == edge, cmul(jnp.broadcast_to(h0, (nc, half)), 0, lo), 0.0)
                for k in range(n_levels):
                    if (1 << k) < nc:
                        h = h + cmul(_shift_rows(h, 1 << k, d == 0), k, lo)
                ht_ref[g, b:b + 1, lo:lo + half] = h[nc - 1 - edge:nc - edge]
                hin = _shift_rows(h, 1, d == 0)
                hin = jnp.where(row == edge, jnp.broadcast_to(h0, (nc, half)), hin)
                hin_scr[b * nc:(b + 1) * nc, lo:lo + half] = hin

        y = jnp.dot(x, tz_scr[...], preferred_element_type=F32)
        y = y + jnp.dot(hin_scr[...].astype(BF16), rm_ref[g], preferred_element_type=F32)
        yg_scr[g] = y.astype(BF16)
        return 0
    lax.fori_loop(0, S5_SLAB_GROUPS, group, 0)

    for v in range(S5_PIECES):
        cat = jnp.concatenate([yg_scr[gl, :, v * LANE:(v + 1) * LANE] for gl in range(S5_SLAB_GROUPS)], axis=1)
        yp = jnp.dot(cat, perm_ref[...], preferred_element_type=F32)
        for tl in range(S5_PER_PIECE):
            for b in range(nb):
                y_ref[b, S5_PER_PIECE * v + tl] = yp[b * nc:(b + 1) * nc, tl * LANE:(tl + 1) * LANE].astype(y_ref.dtype)


def _s5_scan(u, perm, kcat, pm, rm, atab, h0, layer):
    nb, _, nc, _ = u.shape
    rows = nb * nc
    n_levels = atab.shape[2]
    gs = S5_SLAB_GROUPS
    kern = functools.partial(_s5_kernel, nb=nb, nc=nc, n_levels=n_levels)
    tok_spec = pl.BlockSpec((nb, S5_CHUNK, nc, LANE), lambda i: (0, 0, 0, i))
    return pl.pallas_call(
        kern,
        grid=(S5_GROUPS // gs,),
        in_specs=[tok_spec,
                  pl.BlockSpec((gs * LANE, gs * LANE), lambda i: (0, 0)),
                  pl.BlockSpec((None, gs, S5_GROUP, 2 * S5_CW), lambda i: (layer, i, 0, 0)),
                  pl.BlockSpec((None, gs, S5_CW, 4 * S5_STATE), lambda i: (layer, i, 0, 0)),
                  pl.BlockSpec((None, gs, 4 * S5_STATE, S5_CW), lambda i: (layer, i, 0, 0)),
                  pl.BlockSpec((None, gs, n_levels, 2, 4 * S5_STATE), lambda i: (layer, i, 0, 0, 0)),
                  pl.BlockSpec((gs, nb, 4 * S5_STATE), lambda i: (i, 0, 0))],
        out_specs=[tok_spec,
                   pl.BlockSpec((gs, nb, 4 * S5_STATE), lambda i: (i, 0, 0))],
        out_shape=[jax.ShapeDtypeStruct(u.shape, u.dtype),
                   jax.ShapeDtypeStruct((S5_GROUPS, nb, 4 * S5_STATE), F32)],
        scratch_shapes=[pltpu.VMEM((gs, rows, S5_CW), BF16), pltpu.VMEM((gs, rows, S5_CW), BF16),
                        pltpu.VMEM((S5_CW, S5_CW), BF16), pltpu.VMEM((rows, 4 * S5_STATE), F32)],
        compiler_params=_cparams(("arbitrary",)),
        name="s5_scan",
    )(u, perm, kcat, pm, rm, atab, h0)


def _s5_chunks(p, nb, t):
    nc = t // S5_CHUNK
    u = p[:, :, P_U:P_U + S5_WIDTH].reshape(nb, nc, S5_CHUNK, S5_WIDTH).transpose(0, 2, 1, 3)
    return u if nc % 16 == 0 else u.astype(F32)


def _s5_unchunk(y, nb, t):
    return y.transpose(0, 2, 1, 3).reshape(nb, t, S5_WIDTH)


def _s5_perm():
    a = jnp.arange(S5_SLAB_GROUPS * LANE)
    s, g, i = a // LANE, (a % LANE) // S5_GROUP, a % S5_GROUP
    dst = g * LANE + s * S5_GROUP + i
    return (dst[:, None] == a[None, :]).astype(BF16)


def _lru_kernel(x_ref, cw_ref, cb_ref, wg_ref, bg_ref, lam_ref, h0_ref, out_ref, ht_ref,
                xp_scr, xc_scr, af_scr, bf_scr, ab_scr, bb_scr, cf_scr, cr_scr, *, w, s, pad_top, ch, ns):
    l = w * s
    n_ch = l // ch
    slabs = range(ns)
    lanes = [slice(k * LRU_SLAB, (k + 1) * LRU_SLAB) for k in slabs]
    zero_slab = jnp.zeros((w, LRU_SLAB), F32)
    one_slab = jnp.ones((w, LRU_SLAB), F32)

    for k in slabs:
        xp_scr[k, 0:pad_top, :] = jnp.zeros((pad_top, LRU_SLAB), F32)
        xp_scr[k, pad_top + l:pad_top + l + pad_top, :] = jnp.zeros((pad_top, LRU_SLAB), F32)

    def copy_body(i, _):
        r0 = pl.multiple_of(i * ch, ch)
        for k in slabs:
            xp_scr[k, pl.ds(pad_top + r0, ch), :] = x_ref[pl.ds(r0, ch), lanes[k]].astype(F32)
        return 0
    lax.fori_loop(0, n_ch, copy_body, 0)

    def conv_body(i, _):
        r0 = pl.multiple_of(i * ch, ch)
        for k in slabs:
            acc = cb_ref[:, lanes[k]] + cw_ref[2:3, lanes[k]] * xp_scr[k, pl.ds(pad_top + r0, ch), :]
            acc = acc + cw_ref[0:1, lanes[k]] * xp_scr[k, pl.ds(pad_top + r0 - 2 * w, ch), :]
            acc = acc + cw_ref[1:2, lanes[k]] * xp_scr[k, pl.ds(pad_top + r0 - w, ch), :]
            acc = acc + cw_ref[3:4, lanes[k]] * xp_scr[k, pl.ds(pad_top + r0 + w, ch), :]
            xc_scr[k, pl.ds(r0, ch), :] = acc
        return 0
    lax.fori_loop(0, n_ch, conv_body, 0)

    if w > 1:
        def prev_col(v):
            return _shift_rows(v, 1, True)

        def next_col(v):
            return _shift_rows(v, 1, False)

        for k in slabs:
            x_last = xp_scr[k, pad_top + (s - 1) * w:pad_top + s * w, :]
            x_last2 = xp_scr[k, pad_top + (s - 2) * w:pad_top + (s - 1) * w, :]
            x_first = xp_scr[k, pad_top:pad_top + w, :]
            w0, w1, w3 = cw_ref[0:1, lanes[k]], cw_ref[1:2, lanes[k]], cw_ref[3:4, lanes[k]]
            xc_scr[k, 0:w, :] = xc_scr[k, 0:w, :] + w0 * prev_col(x_last2) + w1 * prev_col(x_last)
            xc_scr[k, w:2 * w, :] = xc_scr[k, w:2 * w, :] + w0 * prev_col(x_last)
            xc_scr[k, (s - 1) * w:s * w, :] = xc_scr[k, (s - 1) * w:s * w, :] + w3 * next_col(x_first)

    c_half = (0.5 * LRU_C) * -_softplus(-lam_ref[...])

    def coef_body(i, _):
        r0 = pl.multiple_of(i * ch, ch)
        for k in slabs:
            xc = xc_scr[k, pl.ds(r0, ch), :]
            hx = 0.5 * xc
            g = jnp.dot(xc.astype(BF16), wg_ref[k], preferred_element_type=F32) + bg_ref[k]
            for d, (a_scr, b_scr) in enumerate(((af_scr, bf_scr), (ab_scr, bb_scr))):
                t_r = jnp.tanh(g[:, (2 * d) * LRU_SLAB:(2 * d + 1) * LRU_SLAB])
                t_i = jnp.tanh(g[:, (2 * d + 1) * LRU_SLAB:(2 * d + 2) * LRU_SLAB])
                c = c_half[d:d + 1, lanes[k]]
                a = jnp.exp(c * t_r + c)
                a_scr[k, pl.ds(r0, ch), :] = a
                b_scr[k, pl.ds(r0, ch), :] = jnp.sqrt(1.0 - a * a) * (hx * (t_i + 1.0))
        return 0
    lax.fori_loop(0, n_ch, coef_body, 0)

    def scan_body(r, carry):
        rf = pl.multiple_of(r * w, w)
        rb = pl.multiple_of((s - 1 - r) * w, w)
        out = []
        for k in slabs:
            hf, pf, hb, pb = carry[k]
            a = af_scr[k, pl.ds(rf, w), :]
            hf = a * hf + bf_scr[k, pl.ds(rf, w), :]
            pf = a * pf
            bf_scr[k, pl.ds(rf, w), :] = hf
            af_scr[k, pl.ds(rf, w), :] = pf
            a = ab_scr[k, pl.ds(rb, w), :]
            hb = a * hb + bb_scr[k, pl.ds(rb, w), :]
            pb = a * pb
            bb_scr[k, pl.ds(rb, w), :] = hb
            ab_scr[k, pl.ds(rb, w), :] = pb
            out.append((hf, pf, hb, pb))
        return tuple(out)
    lax.fori_loop(0, s, scan_body, tuple((zero_slab, one_slab, zero_slab, one_slab) for _ in slabs))

    def carry_f(c, carry):
        out = []
        for k in slabs:
            cf_scr[k, pl.ds(c, 1), :] = carry[k]
            out.append(bf_scr[k, pl.ds((s - 1) * w + c, 1), :] + af_scr[k, pl.ds((s - 1) * w + c, 1), :] * carry[k])
        return tuple(out)
    fin = lax.fori_loop(0, w, carry_f, tuple(h0_ref[0:1, lanes[k]] for k in slabs))
    for k in slabs:
        ht_ref[0:1, lanes[k]] = fin[k]

    def carry_b(i, carry):
        c = w - 1 - i
        out = []
        for k in slabs:
            cr_scr[k, pl.ds(c, 1), :] = carry[k]
            out.append(bb_scr[k, pl.ds(c, 1), :] + ab_scr[k, pl.ds(c, 1), :] * carry[k])
        return tuple(out)
    fin = lax.fori_loop(0, w, carry_b, tuple(h0_ref[1:2, lanes[k]] for k in slabs))
    for k in slabs:
        ht_ref[1:2, lanes[k]] = fin[k]

    def out_body(r, _):
        r0 = pl.multiple_of(r * w, w)
        for k in slabs:
            y = (bf_scr[k, pl.ds(r0, w), :] + af_scr[k, pl.ds(r0, w), :] * cf_scr[k]
                 + bb_scr[k, pl.ds(r0, w), :] + ab_scr[k, pl.ds(r0, w), :] * cr_scr[k])
            bf_scr[k, pl.ds(r0, w), :] = y
        return 0
    lax.fori_loop(0, s, out_body, 0)

    def store_body(i, _):
        r0 = pl.multiple_of(i * ch, ch)
        for k in slabs:
            out_ref[pl.ds(r0, ch), lanes[k]] = bf_scr[k, pl.ds(r0, ch), :].astype(out_ref.dtype)
        return 0
    lax.fori_loop(0, n_ch, store_body, 0)


def _lru_scan(p, cw, cb, wg, bg, lam, h0, w, ns, layer):
    b, t, _ = p.shape
    s = t // w
    assert s * w == t and s >= 4
    ch = _tile(t, 256)
    pad_top = max(2 * w, 8)
    cwid = ns * LRU_SLAB
    col0 = P_LX // cwid
    kern = functools.partial(_lru_kernel, w=w, s=s, pad_top=pad_top, ch=ch, ns=ns)
    return pl.pallas_call(
        kern,
        grid=(b, LRU_WIDTH // cwid),
        in_specs=[pl.BlockSpec((None, t, cwid), lambda b_, k: (b_, 0, col0 + k)),
                  pl.BlockSpec((LRU_CONV, cwid), lambda b_, k: (0, k)),
                  pl.BlockSpec((1, cwid), lambda b_, k: (0, k)),
                  pl.BlockSpec((None, ns, LRU_SLAB, 4 * LRU_SLAB), lambda b_, k: (layer, k, 0, 0)),
                  pl.BlockSpec((None, ns, 1, 4 * LRU_SLAB), lambda b_, k: (layer, k, 0, 0)),
                  pl.BlockSpec((2, cwid), lambda b_, k: (0, k)),
                  pl.BlockSpec((None, 2, cwid), lambda b_, k: (b_, 0, k))],
        out_specs=[pl.BlockSpec((None, t, cwid), lambda b_, k: (b_, 0, k)),
                   pl.BlockSpec((None, 2, cwid), lambda b_, k: (b_, 0, k))],
        out_shape=[jax.ShapeDtypeStruct((b, t, LRU_WIDTH), BF16),
                   jax.ShapeDtypeStruct((b, 2, LRU_WIDTH), F32)],
        scratch_shapes=[pltpu.VMEM((ns, t + 2 * pad_top, LRU_SLAB), F32),
                        pltpu.VMEM((ns, t, LRU_SLAB), F32),
                        pltpu.VMEM((ns, t, LRU_SLAB), F32), pltpu.VMEM((ns, t, LRU_SLAB), F32),
                        pltpu.VMEM((ns, t, LRU_SLAB), F32), pltpu.VMEM((ns, t, LRU_SLAB), F32),
                        pltpu.VMEM((ns, w, LRU_SLAB), F32), pltpu.VMEM((ns, w, LRU_SLAB), F32)],
        compiler_params=_cparams(("arbitrary", "arbitrary")),
        name="lru_scan",
    )(p, cw, cb, wg, bg, lam, h0)


def _lru_gate_weights(w_a, b_a, w_x, b_x):
    n_slab = LRU_WIDTH // LRU_SLAB
    per = LRU_SLAB // LRU_BLOCK

    def slab_diag(wm):
        wm = wm.reshape(n_slab, per, LRU_BLOCK, LRU_BLOCK)
        eye = jnp.eye(per, dtype=wm.dtype)
        return jnp.einsum('spkj,pq->spkqj', wm, eye).reshape(n_slab, LRU_SLAB, LRU_SLAB)

    wg = jnp.concatenate([slab_diag(w_a[0]), slab_diag(w_x[0]), slab_diag(w_a[1]), slab_diag(w_x[1])], axis=2)
    bg = jnp.concatenate([b_a[0].reshape(n_slab, 1, LRU_SLAB), b_x[0].reshape(n_slab, 1, LRU_SLAB),
                          b_a[1].reshape(n_slab, 1, LRU_SLAB), b_x[1].reshape(n_slab, 1, LRU_SLAB)], axis=2)
    return (0.5 * wg).astype(BF16), (0.5 * bg).astype(F32)


def _ssd_core(xs, bm, cm, dtr, alog, dbias, e_ref, h_scr, y_scr, reverse):
    q = SSD_CHUNK
    a = -jnp.exp(alog)
    dt = _softplus(dtr + dbias)
    adt = dt * a
    row = lax.broadcasted_iota(jnp.int32, (q, q), 0)
    col = lax.broadcasted_iota(jnp.int32, (q, q), 1)
    tri = (col >= row) if reverse else (col <= row)
    cs = jnp.dot(tri.astype(F32), adt, preferred_element_type=F32, precision=HIGHEST)
    cs_t = cs.T
    dt_t = dt.T
    end = 0 if reverse else q - 1
    tot = cs[end:end + 1, :]
    e = e_ref[...]
    w_x = _split_dot(dt * jnp.exp(tot - cs), e)
    ecs_x = _split_dot(jnp.exp(cs), e)
    xsb = xs.astype(BF16)
    xw = (xs * w_x).astype(BF16)
    decay_row = ecs_x[end:end + 1, :]

    for g in range(SSD_GROUPS):
        cg = cm[:, g * SSD_STATE:(g + 1) * SSD_STATE]
        bg = bm[:, g * SSD_STATE:(g + 1) * SSD_STATE]
        sc = lax.dot_general(cg, bg, (((1,), (1,)), ((), ())), preferred_element_type=F32)
        for j in range(SSD_HPG):
            hd = g * SSD_HPG + j
            cols = slice(hd * SSD_HEAD_DIM, (hd + 1) * SSD_HEAD_DIM)
            diff = cs[:, hd:hd + 1] - cs_t[hd:hd + 1, :]
            lm = jnp.where(tri, jnp.exp(diff), 0.0) * dt_t[hd:hd + 1, :]
            y_scr[:, cols] = jnp.dot((sc * lm).astype(BF16), xsb[:, cols], preferred_element_type=F32)
        ch = slice(g * SSD_GW, (g + 1) * SSD_GW)
        rows = slice(g * SSD_STATE, (g + 1) * SSD_STATE)
        h_old = h_scr[rows, :]
        y_off = jnp.dot(cg, h_old.astype(BF16), preferred_element_type=F32) * ecs_x[:, ch]
        y_scr[:, ch] = y_scr[:, ch] + y_off
        st = lax.dot_general(bg, xw[:, ch], (((0,), (0,)), ((), ())), preferred_element_type=F32)
        h_scr[rows, :] = decay_row[:, ch] * h_old + st
    return y_scr[...]


def _ssd_fwd_kernel(x_ref, xprev_ref, xnext_ref, dt_ref, cw_ref, cb_ref, alog_ref, dbias_ref, e_ref, h0_ref,
                    y_ref, xc_ref, ht_ref, h_scr, y_scr, xcf_scr, *, nc):
    c = pl.program_id(1)

    @pl.when(c == 0)
    def _():
        h_scr[...] = h0_ref[...]

    q = x_ref.shape[0]
    x = x_ref[...].astype(F32)
    prev = jnp.where(c > 0, xprev_ref[...].astype(F32), 0.0)
    nxt = jnp.where(c < nc - 1, xnext_ref[...].astype(F32), 0.0)
    row8 = lax.broadcasted_iota(jnp.int32, (8, SSD_XBC), 0)

    def behind(k):
        xr = pltpu.roll(x, k, 0)
        top = jnp.where(row8 < k, pltpu.roll(prev, k, 0)[0:8], xr[0:8])
        return jnp.concatenate([top, xr[8:]], axis=0)

    xm2, xm1 = behind(2), behind(1)
    xr = pltpu.roll(x, q - 1, 0)
    bottom = jnp.where(row8 == 7, pltpu.roll(nxt, 15, 0)[8:16], xr[q - 8:q])
    xp1 = jnp.concatenate([xr[:q - 8], bottom], axis=0)
    conv = (cb_ref[...] + cw_ref[0:1, :] * xm2 + cw_ref[1:2, :] * xm1
            + cw_ref[2:3, :] * x + cw_ref[3:4, :] * xp1)
    xc = _silu(conv)
    xc_ref[...] = xc.astype(xc_ref.dtype)
    xcf_scr[...] = xc
    for k in range(q // SSD_CHUNK):
        r = slice(k * SSD_CHUNK, (k + 1) * SSD_CHUNK)
        xs = xcf_scr[r, :SSD_INNER]
        bm = xcf_scr[r, SSD_INNER:SSD_INNER + SSD_GROUPS * SSD_STATE].astype(BF16)
        cm = xcf_scr[r, SSD_INNER + SSD_GROUPS * SSD_STATE:].astype(BF16)
        y_ref[r, :] = _ssd_core(xs, bm, cm, dt_ref[r, :], alog_ref[...], dbias_ref[...], e_ref, h_scr, y_scr, False)

    @pl.when(c == nc - 1)
    def _():
        ht_ref[...] = h_scr[...]


def _ssd_bwd_kernel(xc_ref, dt_ref, yf_ref, alog_ref, dbias_ref, dskip_ref, e_ref, h0_ref,
                    y_ref, ht_ref, h_scr, y_scr, *, nc):
    c = pl.program_id(1)

    @pl.when(c == 0)
    def _():
        h_scr[...] = h0_ref[...]

    for k in reversed(range(xc_ref.shape[0] // SSD_CHUNK)):
        r = slice(k * SSD_CHUNK, (k + 1) * SSD_CHUNK)
        xs = xc_ref[r, :SSD_INNER].astype(F32)
        bm = xc_ref[r, SSD_INNER:SSD_INNER + SSD_GROUPS * SSD_STATE]
        cm = xc_ref[r, SSD_INNER + SSD_GROUPS * SSD_STATE:]
        y = _ssd_core(xs, bm, cm, dt_ref[r, :], alog_ref[...], dbias_ref[...], e_ref, h_scr, y_scr, True)
        y_ref[r, :] = y + yf_ref[r, :] + dskip_ref[...] * xs

    @pl.when(c == nc - 1)
    def _():
        ht_ref[...] = h_scr[...]


def _ssd_scan(p, dt, cw, cb, a_log, dt_bias, dskip_x, e_mat, h0f, h0b):
    b, t, _ = p.shape
    q = SSD_STEP_CHUNKS * SSD_CHUNK
    nc = t // q
    assert nc * q == t
    xbc_blk = P_XBC // SSD_XBC
    n16 = t // 16
    hs = (SSD_GROUPS * SSD_STATE, SSD_GW)
    const2 = lambda b_, c: (0, 0)
    state_spec = pl.BlockSpec((None,) + hs, lambda b_, c: (b_, 0, 0))

    yf, xc, htf = pl.pallas_call(
        functools.partial(_ssd_fwd_kernel, nc=nc),
        grid=(b, nc),
        in_specs=[pl.BlockSpec((None, q, SSD_XBC), lambda b_, c: (b_, c, xbc_blk)),
                  pl.BlockSpec((None, 16, SSD_XBC),
                               lambda b_, c: (b_, jnp.maximum(c * (q // 16) - 1, 0), xbc_blk)),
                  pl.BlockSpec((None, 16, SSD_XBC),
                               lambda b_, c: (b_, jnp.minimum((c + 1) * (q // 16), n16 - 1), xbc_blk)),
                  pl.BlockSpec((None, q, LANE), lambda b_, c: (b_, c, 0)),
                  pl.BlockSpec((SSD_CONV, SSD_XBC), const2),
                  pl.BlockSpec((1, SSD_XBC), const2),
                  pl.BlockSpec((1, LANE), const2),
                  pl.BlockSpec((1, LANE), const2),
                  pl.BlockSpec((LANE, SSD_INNER), const2),
                  state_spec],
        out_specs=[pl.BlockSpec((None, q, SSD_INNER), lambda b_, c: (b_, c, 0)),
                   pl.BlockSpec((None, q, SSD_XBC), lambda b_, c: (b_, c, 0)),
                   state_spec],
        out_shape=[jax.ShapeDtypeStruct((b, t, SSD_INNER), F32),
                   jax.ShapeDtypeStruct((b, t, SSD_XBC), BF16),
                   jax.ShapeDtypeStruct((b,) + hs, F32)],
        scratch_shapes=[pltpu.VMEM(hs, F32), pltpu.VMEM((SSD_CHUNK, SSD_INNER), F32),
                        pltpu.VMEM((q, SSD_XBC), F32)],
        compiler_params=_cparams(("arbitrary", "arbitrary")),
        name="ssd_fwd",
    )(p, p, p, dt, cw, cb, a_log[0:1], dt_bias[0:1], e_mat, h0f)

    rev = lambda b_, c: (b_, nc - 1 - c, 0)
    y, htb = pl.pallas_call(
        functools.partial(_ssd_bwd_kernel, nc=nc),
        grid=(b, nc),
        in_specs=[pl.BlockSpec((None, q, SSD_XBC), rev),
                  pl.BlockSpec((None, q, LANE), rev),
                  pl.BlockSpec((None, q, SSD_INNER), rev),
                  pl.BlockSpec((1, LANE), const2),
                  pl.BlockSpec((1, LANE), const2),
                  pl.BlockSpec((1, SSD_INNER), const2),
                  pl.BlockSpec((LANE, SSD_INNER), const2),
                  state_spec],
        out_specs=[pl.BlockSpec((None, q, SSD_INNER), rev), state_spec],
        out_shape=[jax.ShapeDtypeStruct((b, t, SSD_INNER), F32),
                   jax.ShapeDtypeStruct((b,) + hs, F32)],
        scratch_shapes=[pltpu.VMEM(hs, F32), pltpu.VMEM((SSD_CHUNK, SSD_INNER), F32)],
        compiler_params=_cparams(("arbitrary", "arbitrary")),
        name="ssd_bwd",
    )(xc, dt, yf, a_log[1:2], dt_bias[1:2], dskip_x, e_mat, h0b)
    return y, htf, htb


def _merge_kernel(x_ref, mod_ref, u_ref, ya_ref, yb_ref, g_ref, yc_ref, z_ref, gates_ref,
                  d5_ref, wglu_ref, snw_ref, wa_ref, wb_ref, wc_ref, wo_ref, xo_ref):
    ya = jax.nn.gelu(ya_ref[...].astype(F32) + d5_ref[...] * u_ref[...].astype(F32))
    ya = ya * _sigmoid(jnp.dot(ya.astype(BF16), wglu_ref[...], preferred_element_type=F32))
    yb = yb_ref[...].astype(F32) * jax.nn.gelu(g_ref[...].astype(F32))
    yc = yc_ref[...] * _silu(z_ref[...].astype(F32))
    yc = yc * lax.rsqrt(jnp.mean(yc * yc, axis=-1, keepdims=True) + EPS) * snw_ref[...]
    mix = _sigmoid(gates_ref[:, 0:D_MODEL].astype(F32)) * jnp.dot(
        ya.astype(BF16), wa_ref[...], preferred_element_type=F32)
    mix = mix + _sigmoid(gates_ref[:, D_MODEL:2 * D_MODEL].astype(F32)) * jnp.dot(
        yb.astype(BF16), wb_ref[...], preferred_element_type=F32)
    mix = mix + _sigmoid(gates_ref[:, 2 * D_MODEL:3 * D_MODEL].astype(F32)) * jnp.dot(
        yc.astype(BF16), wc_ref[...], preferred_element_type=F32)
    y = jnp.dot(mix.astype(BF16), wo_ref[...], preferred_element_type=F32)
    xo_ref[...] = x_ref[...] + mod_ref[2:3, :] * y


def _merge(x, mod, p, ya, yb, yc, d5, wglu, snw, wa, wb, wc, wo):
    b, t, _ = x.shape
    tm = _tile(t, 512)
    tok = lambda width, blk: pl.BlockSpec((None, tm, width), lambda b_, i: (b_, i, blk))
    full = lambda r, c_: pl.BlockSpec((r, c_), lambda b_, i: (0, 0))
    return pl.pallas_call(
        _merge_kernel,
        grid=(b, t // tm),
        in_specs=[tok(D_MODEL, 0),
                  pl.BlockSpec((None, 8, D_MODEL), lambda b_, i: (b_, 0, 0)),
                  tok(S5_WIDTH, P_U // S5_WIDTH),
                  tok(S5_WIDTH, 0),
                  tok(LRU_WIDTH, 0),
                  tok(LRU_WIDTH, P_LG // LRU_WIDTH),
                  tok(SSD_INNER, 0),
                  tok(SSD_INNER, P_Z // SSD_INNER),
                  tok(3 * D_MODEL, P_GATE // (3 * D_MODEL)),
                  full(1, S5_WIDTH), full(S5_WIDTH, S5_WIDTH), full(1, SSD_INNER),
                  full(S5_WIDTH, D_MODEL), full(LRU_WIDTH, D_MODEL), full(SSD_INNER, D_MODEL),
                  full(D_MODEL, D_MODEL)],
        out_specs=tok(D_MODEL, 0),
        out_shape=jax.ShapeDtypeStruct((b, t, D_MODEL), F32),
        compiler_params=_cparams(("arbitrary", "arbitrary")),
        name="merge",
    )(x, mod, p, ya, yb, p, yc, p, p, d5, wglu, snw, wa, wb, wc, wo)


def _ffn_kernel(x_ref, mod_ref, nw_ref, w1_ref, w3_ref, w2_ref, fnw_ref, xo_ref, h_scr, acc_scr, *, nf, final_norm):
    f = pl.program_id(2)

    @pl.when(f == 0)
    def _():
        h_scr[...] = _rms_mod(x_ref[...], nw_ref[...], mod_ref[3:4, :], mod_ref[4:5, :]).astype(BF16)
        acc_scr[...] = jnp.zeros_like(acc_scr)

    h = h_scr[...]
    a = jnp.dot(h, w1_ref[...], preferred_element_type=F32)
    g = jnp.dot(h, w3_ref[...], preferred_element_type=F32)
    acc_scr[...] += jnp.dot((_silu(a) * g).astype(BF16), w2_ref[...], preferred_element_type=F32)

    @pl.when(f == nf - 1)
    def _():
        xo = x_ref[...] + mod_ref[5:6, :] * acc_scr[...]
        if final_norm:
            xo = xo * lax.rsqrt(jnp.mean(xo * xo, axis=-1, keepdims=True) + EPS) * fnw_ref[...]
        xo_ref[...] = xo


def _ffn(x, mod, nw, w1, w3, w2, fnw, final_norm):
    b, t, _ = x.shape
    dff = w1.shape[1]
    tm = _tile(t, 512)
    tf = dff // 2
    nf = dff // tf
    return pl.pallas_call(
        functools.partial(_ffn_kernel, nf=nf, final_norm=final_norm),
        grid=(b, t // tm, nf),
        in_specs=[pl.BlockSpec((None, tm, D_MODEL), lambda b_, i, f: (b_, i, 0)),
                  pl.BlockSpec((None, 8, D_MODEL), lambda b_, i, f: (b_, 0, 0)),
                  pl.BlockSpec((1, D_MODEL), lambda b_, i, f: (0, 0)),
                  pl.BlockSpec((D_MODEL, tf), lambda b_, i, f: (0, f)),
                  pl.BlockSpec((D_MODEL, tf), lambda b_, i, f: (0, f)),
                  pl.BlockSpec((tf, D_MODEL), lambda b_, i, f: (f, 0)),
                  pl.BlockSpec((1, D_MODEL), lambda b_, i, f: (0, 0))],
        out_specs=pl.BlockSpec((None, tm, D_MODEL), lambda b_, i, f: (b_, i, 0)),
        out_shape=jax.ShapeDtypeStruct((b, t, D_MODEL), F32),
        scratch_shapes=[pltpu.VMEM((tm, D_MODEL), BF16), pltpu.VMEM((tm, D_MODEL), F32)],
        compiler_params=_cparams(("arbitrary", "arbitrary", "arbitrary")),
        name="ffn",
    )(x, mod, nw, w1, w3, w2, fnw)


MOE_BLK = 512
MOE_PER_EBLK = 2
SEL_LANE = 8


def _router_kernel(x_ref, mod_ref, nw_ref, wr_ref, br_ref, h_ref, g_ref):
    h = _rms_mod(x_ref[...], nw_ref[...], mod_ref[3:4, :], mod_ref[4:5, :])
    h_ref[...] = h.astype(BF16)
    logits = jnp.dot(h, wr_ref[...], preferred_element_type=F32, precision=HIGHEST) + br_ref[...]
    lane = lax.broadcasted_iota(jnp.int32, logits.shape, 1)
    logits = jnp.where(lane < N_EXPERTS, logits, -jnp.inf)
    t1 = jnp.max(logits, axis=-1, keepdims=True)
    i1 = jnp.min(jnp.where(logits == t1, lane, LANE), axis=-1, keepdims=True)
    rest = jnp.where(lane == i1, -jnp.inf, logits)
    t2 = jnp.max(rest, axis=-1, keepdims=True)
    i2 = jnp.min(jnp.where(rest == t2, lane, LANE), axis=-1, keepdims=True)
    e2 = jnp.exp(t2 - t1)
    den = 1.0 + e2
    gate = jnp.where(lane == i1, 1.0 / den, 0.0) + jnp.where(lane == i2, e2 / den, 0.0)
    flag = jnp.where((lane == i1 + SEL_LANE) | (lane == i2 + SEL_LANE), 1.0, 0.0)
    g_ref[...] = gate + flag


def _router(x, mod, nw, wr, br):
    b, t, _ = x.shape
    tm = _tile(t, 512)
    return pl.pallas_call(
        _router_kernel,
        grid=(b, t // tm),
        in_specs=[pl.BlockSpec((None, tm, D_MODEL), lambda b_, i: (b_, i, 0)),
                  pl.BlockSpec((None, 8, D_MODEL), lambda b_, i: (b_, 0, 0)),
                  pl.BlockSpec((1, D_MODEL), lambda b_, i: (0, 0)),
                  pl.BlockSpec((D_MODEL, LANE), lambda b_, i: (0, 0)),
                  pl.BlockSpec((1, LANE), lambda b_, i: (0, 0))],
        out_specs=[pl.BlockSpec((None, tm, D_MODEL), lambda b_, i: (b_, i, 0)),
                   pl.BlockSpec((None, tm, LANE), lambda b_, i: (b_, i, 0))],
        out_shape=[jax.ShapeDtypeStruct((b, t, D_MODEL), BF16),
                   jax.ShapeDtypeStruct((b, t, LANE), F32)],
        compiler_params=_cparams(("arbitrary", "arbitrary")),
        name="router",
    )(x, mod, nw, wr, br)


def _moe_plan(g, ts):
    n = g.shape[0]
    blk = MOE_BLK
    eblk = MOE_PER_EBLK * MOE_BLK
    nt = n // ts
    n_eblocks = TOP_K * n // eblk + N_EXPERTS
    n_blocks = n_eblocks * MOE_PER_EBLK
    max_pairs = n_blocks + N_EXPERTS * nt
    sel = g[:, SEL_LANE:SEL_LANE + N_EXPERTS] > 0.5
    cum = jnp.cumsum(sel.astype(jnp.int32), axis=0)
    cnt = cum[-1]
    neb = (cnt + eblk - 1) // eblk
    eb_end = jnp.cumsum(neb)
    eb_beg = eb_end - neb
    seg_off = eb_beg * eblk
    n_used = eb_end[-1]
    pos = jnp.where(sel, seg_off[None, :] + cum - 1, -1).astype(jnp.int32)
    eb = jnp.arange(n_eblocks)
    bexp = jnp.minimum(jnp.sum(eb_end[None, :] <= eb[:, None], axis=1), N_EXPERTS - 1)

    tile_end = cum[ts - 1::ts]
    tile_beg = jnp.concatenate([jnp.zeros((1, N_EXPERTS), jnp.int32), tile_end[:-1]], axis=0)
    d_lo = (seg_off[None, :] + tile_beg) // blk
    d_hi = (seg_off[None, :] + tile_end - 1) // blk
    npairs = jnp.where(tile_end > tile_beg, d_hi - d_lo + 1, 0)

    def expand(cnt_flat, cell_j, cell_e):
        cend = jnp.cumsum(cnt_flat)
        total = cend[-1]
        k = jnp.minimum(jnp.arange(max_pairs), total - 1)
        cell = jnp.sum(cend[None, :] <= k[:, None], axis=1)
        i = k - (cend[cell] - cnt_flat[cell])
        j, e = cell_j[cell], cell_e[cell]
        d = d_lo[j, e] + i
        valid = (jnp.arange(max_pairs) < total).astype(jnp.int32)
        return j.astype(jnp.int32), d.astype(jnp.int32), e.astype(jnp.int32), valid

    jj, ee = jnp.meshgrid(jnp.arange(nt), jnp.arange(N_EXPERTS), indexing='ij')
    by_tile = expand(npairs.reshape(-1), jj.reshape(-1), ee.reshape(-1))

    blk_e = jnp.repeat(bexp, MOE_PER_EBLK)
    q0 = jnp.arange(n_blocks) * blk - seg_off[blk_e]
    q1 = jnp.minimum(cnt[blk_e], q0 + blk) - 1
    ends = tile_end.T[blk_e]
    has = q0 < cnt[blk_e]
    j_lo = jnp.where(has, jnp.sum(ends <= q0[:, None], axis=1), 0)
    j_hi = jnp.where(has, jnp.sum(ends <= q1[:, None], axis=1), -1)

    def edges(key, valid):
        prev = jnp.concatenate([jnp.full((1,), -1, jnp.int32), key[:-1]])
        nxt = jnp.concatenate([key[1:], jnp.full((1,), -1, jnp.int32)])
        nxt_valid = jnp.concatenate([valid[1:], jnp.zeros((1,), jnp.int32)])
        first = ((key != prev) & (valid == 1)).astype(jnp.int32)
        last = (((key != nxt) | (nxt_valid == 0)) & (valid == 1)).astype(jnp.int32)
        return first, last

    tj, td, te, tv = by_tile
    t_first, t_last = edges(tj, tv)
    return dict(pos=pos, pos_t=pos.T, gate=g[:, :N_EXPERTS], bexp=bexp.astype(jnp.int32),
                n_used=n_used.astype(jnp.int32).reshape(1), n_blocks=n_blocks, n_eblocks=n_eblocks,
                max_pairs=max_pairs,
                disp=(blk_e.astype(jnp.int32), j_lo.astype(jnp.int32), j_hi.astype(jnp.int32)),
                comb=(tj, td, te, t_first, t_last, tv))


def _dispatch_kernel(be, jlo, jhi, h_ref, post_ref, xs_ref, acc_scr, *, ts):
    d = pl.program_id(0)
    e = be[d]
    row = d * MOE_BLK + lax.broadcasted_iota(jnp.int32, (MOE_BLK, 1), 0)
    acc_scr[...] = jnp.zeros_like(acc_scr)

    def tile(j, _):
        t0 = pl.multiple_of(j * ts, ts)
        hit = post_ref[pl.ds(e, 1), pl.ds(t0, ts)] == row
        acc_scr[...] += jnp.dot(jnp.where(hit, 1.0, 0.0).astype(BF16), h_ref[pl.ds(t0, ts), :],
                                preferred_element_type=F32)
        return 0
    lax.fori_loop(jlo[d], jhi[d] + 1, tile, 0)
    xs_ref[...] = acc_scr[...].astype(xs_ref.dtype)


def _dispatch(h, plan, ts):
    n = h.shape[0]
    rows = plan['n_blocks'] * MOE_BLK
    grid_spec = pltpu.PrefetchScalarGridSpec(
        num_scalar_prefetch=3,
        grid=(plan['n_blocks'],),
        in_specs=[pl.BlockSpec((n, D_MODEL), lambda d, be, lo, hi: (0, 0), pipeline_mode=pl.Buffered(1)),
                  pl.BlockSpec((N_EXPERTS, n), lambda d, be, lo, hi: (0, 0), pipeline_mode=pl.Buffered(1))],
        out_specs=pl.BlockSpec((MOE_BLK, D_MODEL), lambda d, be, lo, hi: (d, 0)),
        scratch_shapes=[pltpu.VMEM((MOE_BLK, D_MODEL), F32)],
    )
    return pl.pallas_call(
        functools.partial(_dispatch_kernel, ts=ts),
        grid_spec=grid_spec,
        out_shape=jax.ShapeDtypeStruct((rows, D_MODEL), BF16),
        compiler_params=_cparams(("arbitrary",)),
        name="moe_dispatch",
    )(*plan['disp'], h, plan['pos_t'])


def _experts_kernel(bexp, nused, x_ref, w1_ref, w3_ref, w2_ref, y_ref, acc_scr, *, nf):
    d = pl.program_id(0)
    f = pl.program_id(1)

    @pl.when(d < nused[0])
    def _():
        @pl.when(f == 0)
        def _():
            acc_scr[...] = jnp.zeros_like(acc_scr)

        x = x_ref[...]
        a = jnp.dot(x, w1_ref[...].astype(BF16), preferred_element_type=F32)
        g = jnp.dot(x, w3_ref[...].astype(BF16), preferred_element_type=F32)
        acc_scr[...] += jnp.dot((_silu(a) * g).astype(BF16), w2_ref[...].astype(BF16),
                                preferred_element_type=F32)

        @pl.when(f == nf - 1)
        def _():
            y_ref[...] = acc_scr[...].astype(y_ref.dtype)


def _experts(xs, plan, w1, w3, w2):
    dff = w1.shape[2]
    nf = 7
    tf = dff // nf
    assert tf * nf == dff and tf % LANE == 0
    eblk = MOE_PER_EBLK * MOE_BLK

    def blk(d, nused):
        return jnp.minimum(d, nused[0] - 1)

    def fidx(d, f, nused):
        return jnp.where(d < nused[0], f, nf - 1)

    grid_spec = pltpu.PrefetchScalarGridSpec(
        num_scalar_prefetch=2,
        grid=(plan['n_eblocks'], nf),
        in_specs=[pl.BlockSpec((eblk, D_MODEL), lambda d, f, be, nu: (blk(d, nu), 0)),
                  pl.BlockSpec((None, D_MODEL, tf), lambda d, f, be, nu: (be[blk(d, nu)], 0, fidx(d, f, nu))),
                  pl.BlockSpec((None, D_MODEL, tf), lambda d, f, be, nu: (be[blk(d, nu)], 0, fidx(d, f, nu))),
                  pl.BlockSpec((None, tf, D_MODEL), lambda d, f, be, nu: (be[blk(d, nu)], fidx(d, f, nu), 0))],
        out_specs=pl.BlockSpec((eblk, D_MODEL), lambda d, f, be, nu: (blk(d, nu), 0)),
        scratch_shapes=[pltpu.VMEM((eblk, D_MODEL), F32)],
    )
    return pl.pallas_call(
        functools.partial(_experts_kernel, nf=nf),
        grid_spec=grid_spec,
        out_shape=jax.ShapeDtypeStruct(xs.shape, BF16),
        compiler_params=_cparams(("arbitrary", "arbitrary")),
        name="moe_experts",
    )(plan['bexp'], plan['n_used'], xs, w1, w3, w2)


def _combine_kernel(pj, pd, pe, pfirst, plast, pvalid, ys_ref, pos_ref, gate_ref, x_ref, mod_ref, fnw_ref,
                    xo_ref, acc_scr, *, ts, final_norm):
    k = pl.program_id(0)

    @pl.when(pfirst[k] == 1)
    def _():
        acc_scr[...] = jnp.zeros_like(acc_scr)

    @pl.when(pvalid[k] == 1)
    def _():
        e = pe[k]
        lane = lax.broadcasted_iota(jnp.int32, (ts, N_EXPERTS), 1)
        pos_e = jnp.sum(jnp.where(lane == e, pos_ref[...], 0), axis=1, keepdims=True)
        gate_e = jnp.sum(jnp.where(lane == e, gate_ref[...], 0.0), axis=1, keepdims=True)
        row = pd[k] * MOE_BLK + lax.broadcasted_iota(jnp.int32, (1, MOE_BLK), 1)
        hit = pos_e == row
        got = jnp.dot(jnp.where(hit, 1.0, 0.0).astype(BF16), ys_ref[...], preferred_element_type=F32)
        acc_scr[...] += gate_e * got

    @pl.when(plast[k] == 1)
    def _():
        xo = x_ref[...] + mod_ref[5:6, :] * acc_scr[...]
        if final_norm:
            xo = xo * lax.rsqrt(jnp.mean(xo * xo, axis=-1, keepdims=True) + EPS) * fnw_ref[...]
        xo_ref[...] = xo


def _combine(ys, plan, x, mod, fnw, ts, final_norm):
    n = x.shape[0]
    tiles_per_batch = n // (mod.shape[0] * ts)
    sp = lambda fn: (lambda k, pj, pd, pe, pf, pl_, pv: fn(k, pj, pd))
    grid_spec = pltpu.PrefetchScalarGridSpec(
        num_scalar_prefetch=6,
        grid=(plan['max_pairs'],),
        in_specs=[pl.BlockSpec((MOE_BLK, D_MODEL), sp(lambda k, pj, pd: (pd[k], 0))),
                  pl.BlockSpec((ts, N_EXPERTS), sp(lambda k, pj, pd: (pj[k], 0))),
                  pl.BlockSpec((ts, N_EXPERTS), sp(lambda k, pj, pd: (pj[k], 0))),
                  pl.BlockSpec((ts, D_MODEL), sp(lambda k, pj, pd: (pj[k], 0))),
                  pl.BlockSpec((None, 8, D_MODEL), sp(lambda k, pj, pd: (pj[k] // tiles_per_batch, 0, 0))),
                  pl.BlockSpec((1, D_MODEL), sp(lambda k, pj, pd: (0, 0)))],
        out_specs=pl.BlockSpec((ts, D_MODEL), sp(lambda k, pj, pd: (pj[k], 0))),
        scratch_shapes=[pltpu.VMEM((ts, D_MODEL), F32)],
    )
    return pl.pallas_call(
        functools.partial(_combine_kernel, ts=ts, final_norm=final_norm),
        grid_spec=grid_spec,
        out_shape=jax.ShapeDtypeStruct((n, D_MODEL), F32),
        compiler_params=_cparams(("arbitrary",)),
        name="moe_combine",
    )(*plan['comb'], ys, plan['pos'], plan['gate'], x, mod, fnw)


def _moe(x, mod, nw, wr, br, w1, w3, w2, fnw, final_norm):
    b, t, _ = x.shape
    ts = _tile(t, 1024)
    h, g = _router(x, mod, nw, wr, br)
    plan = _moe_plan(g.reshape(b * t, LANE), ts)
    xs = _dispatch(h.reshape(b * t, D_MODEL), plan, ts)
    ys = _experts(xs, plan, w1, w3, w2)
    out = _combine(ys, plan, x.reshape(b * t, D_MODEL), mod, fnw, ts, final_norm)
    return out.reshape(b, t, D_MODEL)


def _pad_lanes(v, fill=0.0):
    return jnp.pad(v.astype(F32), (0, LANE - v.shape[0]), constant_values=fill)[None, :]


def kernel(x, c, ctx, c_ctx, w_mod, b_mod, norm_w, w_in, s5_lam_re, s5_lam_im, s5_log_dt, s5_b_re, s5_b_im, s5_c_re, s5_c_im, s5_d, s5_w_glu, lru_conv_w, lru_conv_b, lru_w_a, lru_b_a, lru_w_x, lru_b_x, lru_lam, ssd_conv_w, ssd_conv_b, ssd_a_log, ssd_dt_bias, ssd_d, ssd_norm_w, w_br_a, w_br_b, w_br_c, w_out, ffn_w1, ffn_w3, ffn_w2, moe_w_router, moe_b_router, moe_w1, moe_w3, moe_w2, final_norm_w):
    depth = w_mod.shape[0]
    nb, t_lat, _ = x.shape
    t_ctx = ctx.shape[1]
    assert t_lat % GRID_W == 0 and t_lat % SSD_CHUNK == 0 and t_ctx % SSD_CHUNK == 0

    cond_in = jnp.zeros((8, D_MODEL), F32).at[:nb].set(c).at[nb].set(c_ctx)
    mods = _modulation(cond_in, w_mod, b_mod).reshape(depth, 8, N_MOD, D_MODEL)
    mods = jnp.pad(mods, ((0, 0), (0, 0), (0, 8 - N_MOD), (0, 0)))

    head_of = jnp.arange(SSD_INNER) // SSD_HEAD_DIM
    e_mat = (jnp.arange(LANE)[:, None] == head_of[None, :]).astype(BF16)
    s5_perm = _s5_perm()
    n_lev_lat = max(1, math.ceil(math.log2(t_lat // S5_CHUNK)))
    s5_prep = jax.vmap(functools.partial(_s5_setup, n_levels=n_lev_lat))(
        s5_lam_re, s5_lam_im, s5_log_dt, s5_b_re, s5_b_im, s5_c_re, s5_c_im)
    lru_prep = jax.vmap(_lru_gate_weights)(lru_w_a, lru_b_a, lru_w_x, lru_b_x)

    x_lat, x_ctx = x, ctx
    for l in range(depth):
        ctx_out = l < depth - 1
        last = l == depth - 1
        mod_lat = mods[l, :nb]
        mod_ctx = jnp.broadcast_to(mods[l, nb][None], (nb, 8, D_MODEL))
        nw0 = norm_w[l, 0][None, :]
        nw1 = norm_w[l, 1][None, :]

        wl = w_in[l]
        o_lx = S5_WIDTH
        o_lg = o_lx + LRU_WIDTH
        o_z = o_lg + LRU_WIDTH
        o_xbc = o_z + SSD_INNER
        o_dt = o_xbc + SSD_XBC
        o_g = o_dt + SSD_HEADS
        w_main = jnp.concatenate(
            [wl[:, :o_lx], jnp.zeros((D_MODEL, P_LX - S5_WIDTH), wl.dtype), wl[:, o_lx:o_dt], wl[:, o_g:]],
            axis=1).astype(BF16)
        w_dt = jnp.pad(wl[:, o_dt:o_g], ((0, 0), (0, LANE - SSD_HEADS))).astype(BF16)
        kcat, pm, rm, atab = s5_prep
        wg, bg = lru_prep
        lcw = lru_conv_w[l].astype(F32)
        lcb = lru_conv_b[l][None, :].astype(F32)
        scw = ssd_conv_w[l].astype(F32)
        scb = ssd_conv_b[l][None, :].astype(F32)
        a_log = jnp.concatenate([_pad_lanes(ssd_a_log[l, 0]), _pad_lanes(ssd_a_log[l, 1])], axis=0)
        dt_bias = jnp.concatenate([_pad_lanes(ssd_dt_bias[l, 0]), _pad_lanes(ssd_dt_bias[l, 1])], axis=0)
        dskip_x = jnp.repeat(ssd_d[l].astype(F32), SSD_HEAD_DIM)[None, :]

        def mixers(p, dt, t, w, s5_h0, lru_h0, ssd_h0f, ssd_h0b):
            y5, s5_ht = _s5_scan(_s5_chunks(p, nb, t), s5_perm, kcat, pm, rm, atab, s5_h0, l)
            ya = _s5_unchunk(y5, nb, t)
            yb, lru_ht = _lru_scan(p, lcw, lcb, wg, bg, lru_lam[l].astype(F32), lru_h0, w,
                                   1 if w > 1 else LRU_WIDTH // LRU_SLAB, l)
            yc, ssd_htf, ssd_htb = _ssd_scan(p, dt, scw, scb, a_log, dt_bias, dskip_x, e_mat, ssd_h0f, ssd_h0b)
            return ya, yb, yc, s5_ht, lru_ht, ssd_htf, ssd_htb

        p_ctx, dt_ctx = _inproj(x_ctx, mod_ctx, nw0, w_main, w_dt)
        zs5 = jnp.zeros((S5_GROUPS, nb, 4 * S5_STATE), F32)
        zlru = jnp.zeros((nb, 2, LRU_WIDTH), F32)
        zssd = jnp.zeros((nb, SSD_GROUPS * SSD_STATE, SSD_GW), F32)
        ya_c, yb_c, yc_c, s5_h, lru_h, ssd_hf, ssd_hb = mixers(p_ctx, dt_ctx, t_ctx, 1, zs5, zlru, zssd, zssd)

        p_lat, dt_lat = _inproj(x_lat, mod_lat, nw0, w_main, w_dt)
        ya, yb, yc, _, _, _, _ = mixers(p_lat, dt_lat, t_lat, GRID_W, s5_h, lru_h, ssd_hf, ssd_hb)

        mw = (s5_d[l][None, :].astype(F32), s5_w_glu[l].astype(BF16), ssd_norm_w[l][None, :].astype(F32),
              w_br_a[l].astype(BF16), w_br_b[l].astype(BF16), w_br_c[l].astype(BF16), w_out[l].astype(BF16))
        x_lat = _merge(x_lat, mod_lat, p_lat, ya, yb, yc, *mw)
        if ctx_out:
            x_ctx = _merge(x_ctx, mod_ctx, p_ctx, ya_c, yb_c, yc_c, *mw)

        fnw = final_norm_w[None, :].astype(F32)
        if l % 2 == 0:
            fw = (ffn_w1[l // 2].astype(BF16), ffn_w3[l // 2].astype(BF16), ffn_w2[l // 2].astype(BF16))
            x_lat = _ffn(x_lat, mod_lat, nw1, *fw, fnw, last)
            if ctx_out:
                x_ctx = _ffn(x_ctx, mod_ctx, nw1, *fw, fnw, False)
        else:
            wr = jnp.pad(moe_w_router[l // 2].astype(F32), ((0, 0), (0, LANE - N_EXPERTS)))
            br = _pad_lanes(moe_b_router[l // 2])
            ew = (moe_w1[l // 2], moe_w3[l // 2], moe_w2[l // 2])
            x_lat = _moe(x_lat, mod_lat, nw1, wr, br, *ew, fnw, last)
            if ctx_out:
                x_ctx = _moe(x_ctx, mod_ctx, nw1, wr, br, *ew, fnw, False)
    return x_lat
```

```python
import functools
import math

import jax
import jax.numpy as jnp
from jax import lax
from jax.experimental import pallas as pl
from jax.experimental.pallas import tpu as pltpu

F32 = jnp.float32
BF16 = jnp.bfloat16
HIGHEST = lax.Precision.HIGHEST

D_MODEL = 1024
GRID_W = 64
N_MOD = 6
EPS = 1e-6

S5_WIDTH = 768
S5_GROUP = 16
S5_GROUPS = S5_WIDTH // S5_GROUP
S5_STATE = 64
S5_CHUNK = 32
S5_CW = S5_CHUNK * S5_GROUP

LRU_WIDTH = 1024
LRU_BLOCKS = 16
LRU_BLOCK = LRU_WIDTH // LRU_BLOCKS
LRU_CONV = 4
LRU_C = 8.0
LRU_SLAB = 128

SSD_INNER = 1024
SSD_HEAD_DIM = 64
SSD_HEADS = SSD_INNER // SSD_HEAD_DIM
SSD_GROUPS = 4
SSD_HPG = SSD_HEADS // SSD_GROUPS
SSD_STATE = 128
SSD_CONV = 4
SSD_CHUNK = 128
SSD_STEP_CHUNKS = 4
SSD_XBC = SSD_INNER + 2 * SSD_GROUPS * SSD_STATE
SSD_GW = SSD_HPG * SSD_HEAD_DIM

N_EXPERTS = 8
TOP_K = 2
LANE = 128

P_U, P_LX, P_LG, P_Z, P_XBC, P_GATE = 0, 1024, 2048, 3072, 4096, 6144
P_TOTAL = 9216

VMEM_LIMIT = 56 * 1024 * 1024


def _cparams(sem):
    return pltpu.CompilerParams(dimension_semantics=sem, vmem_limit_bytes=VMEM_LIMIT)


def _tile(n, pref):
    t = min(n, pref)
    while n % t:
        t //= 2
    return t


def _sigmoid(x):
    return jax.nn.sigmoid(x)


def _silu(x):
    return x * jax.nn.sigmoid(x)


def _softplus(x):
    return jnp.maximum(x, 0.0) + jnp.log(1.0 + jnp.exp(-jnp.abs(x)))


def _split_dot(x, w_bf16):
    hi = x.astype(BF16)
    lo = (x - hi.astype(F32)).astype(BF16)
    return (jnp.dot(hi, w_bf16, preferred_element_type=F32)
            + jnp.dot(lo, w_bf16, preferred_element_type=F32))


def _rms_mod(x, nw, shift, scale):
    y = x * lax.rsqrt(jnp.mean(x * x, axis=-1, keepdims=True) + EPS) * nw
    return y * (1.0 + scale) + shift


def _mod_kernel(c_ref, w_ref, b_ref, o_ref):
    cond = _silu(c_ref[...])
    o_ref[...] = jnp.dot(cond, w_ref[...], preferred_element_type=F32, precision=HIGHEST) + b_ref[...]


def _modulation(cond_in, w_mod, b_mod):
    depth = w_mod.shape[0]
    n = w_mod.shape[2]
    tn = 1024
    return pl.pallas_call(
        _mod_kernel,
        grid=(depth, n // tn),
        in_specs=[pl.BlockSpec((8, D_MODEL), lambda l, j: (0, 0)),
                  pl.BlockSpec((None, D_MODEL, tn), lambda l, j: (l, 0, j)),
                  pl.BlockSpec((None, 1, tn), lambda l, j: (l, 0, j))],
        out_specs=pl.BlockSpec((None, 8, tn), lambda l, j: (l, 0, j)),
        out_shape=jax.ShapeDtypeStruct((depth, 8, n), F32),
        compiler_params=_cparams(("arbitrary", "arbitrary")),
        name="modulation",
    )(cond_in, w_mod, b_mod.reshape(depth, 1, n))


def _inproj_kernel(x_ref, mod_ref, nw_ref, w_ref, wdt_ref, p_ref, dt_ref, h_scr):
    @pl.when(pl.program_id(2) == 0)
    def _():
        h = _rms_mod(x_ref[...], nw_ref[...], mod_ref[0:1, :], mod_ref[1:2, :]).astype(BF16)
        h_scr[...] = h
        dt_ref[...] = jnp.dot(h, wdt_ref[...], preferred_element_type=F32)

    p_ref[...] = jnp.dot(h_scr[...], w_ref[...], preferred_element_type=F32).astype(p_ref.dtype)


def _inproj(x, mod, nw, w, wdt):
    b, t, _ = x.shape
    tm = _tile(t, 1024)
    tn = 2304
    return pl.pallas_call(
        _inproj_kernel,
        grid=(b, t // tm, P_TOTAL // tn),
        in_specs=[pl.BlockSpec((None, tm, D_MODEL), lambda b_, i, j: (b_, i, 0)),
                  pl.BlockSpec((None, 8, D_MODEL), lambda b_, i, j: (b_, 0, 0)),
                  pl.BlockSpec((1, D_MODEL), lambda b_, i, j: (0, 0)),
                  pl.BlockSpec((D_MODEL, tn), lambda b_, i, j: (0, j)),
                  pl.BlockSpec((D_MODEL, LANE), lambda b_, i, j: (0, 0))],
        out_specs=[pl.BlockSpec((None, tm, tn), lambda b_, i, j: (b_, i, j)),
                   pl.BlockSpec((None, tm, LANE), lambda b_, i, j: (b_, i, 0))],
        out_shape=[jax.ShapeDtypeStruct((b, t, P_TOTAL), BF16),
                   jax.ShapeDtypeStruct((b, t, LANE), F32)],
        scratch_shapes=[pltpu.VMEM((tm, D_MODEL), BF16)],
        compiler_params=_cparams(("arbitrary", "arbitrary", "arbitrary")),
        name="inproj",
    )(x, mod, nw, w, wdt)


def _s5_setup(lam_re, lam_im, log_dt, b_re, b_im, c_re, c_im, n_levels):
    t = S5_CHUNK
    dt = jnp.exp(log_dt.astype(F32))[..., None]
    lre, lim = lam_re.astype(F32), lam_im.astype(F32)
    are, aim = lre * dt, lim * dt
    k = jnp.arange(t + 1, dtype=F32)[:, None]
    mag = jnp.exp(are[:, :, None, :] * k)
    pw_re = mag * jnp.cos(aim[:, :, None, :] * k)
    pw_im = mag * jnp.sin(aim[:, :, None, :] * k)
    a_re, a_im = pw_re[:, :, 1], pw_im[:, :, 1]
    den = lre * lre + lim * lim
    q_re = ((a_re - 1.0) * lre + a_im * lim) / den
    q_im = (a_im * lre - (a_re - 1.0) * lim) / den
    bb_re = q_re[..., None] * b_re - q_im[..., None] * b_im
    bb_im = q_re[..., None] * b_im + q_im[..., None] * b_re
    cr, ci = c_re.astype(F32), c_im.astype(F32)

    cb_re = cr[..., None] * bb_re[:, :, None] - ci[..., None] * bb_im[:, :, None]
    cb_im = cr[..., None] * bb_im[:, :, None] + ci[..., None] * bb_re[:, :, None]
    kern = (jnp.einsum('dgkp,dgjpi->dgkji', pw_re[:, :, :t], cb_re, precision=HIGHEST)
            - jnp.einsum('dgkp,dgjpi->dgkji', pw_im[:, :, :t], cb_im, precision=HIGHEST))
    kf, kb = kern[0], kern[1]
    taps = jnp.concatenate([kb[:, 1:][:, ::-1], (kf[:, 0] + kb[:, 0])[:, None], kf[:, 1:]], axis=1)
    kcat = taps.transpose(0, 3, 1, 2).reshape(S5_GROUPS, S5_GROUP, (2 * t - 1) * S5_GROUP)
    kcat = jnp.pad(kcat, ((0, 0), (0, 0), (0, S5_GROUP)))

    pf_re, pf_im = pw_re[0, :, :t][:, ::-1], pw_im[0, :, :t][:, ::-1]
    pb_re, pb_im = pw_re[1, :, :t], pw_im[1, :, :t]

    def in_map(p_re, p_im, d):
        w_re = p_re[:, :, None, :] * bb_re[d].transpose(0, 2, 1)[:, None] - p_im[:, :, None, :] * bb_im[d].transpose(0, 2, 1)[:, None]
        w_im = p_re[:, :, None, :] * bb_im[d].transpose(0, 2, 1)[:, None] + p_im[:, :, None, :] * bb_re[d].transpose(0, 2, 1)[:, None]
        return w_re, w_im

    wf_re, wf_im = in_map(pf_re, pf_im, 0)
    wb_re, wb_im = in_map(pb_re, pb_im, 1)
    pm = jnp.concatenate([wf_re, wf_im, wb_re, wb_im], axis=-1).reshape(S5_GROUPS, S5_CW, 4 * S5_STATE)

    def out_map(p_re, p_im, d):
        c_r = cr[d].transpose(0, 2, 1)[:, :, None, :]
        c_i = ci[d].transpose(0, 2, 1)[:, :, None, :]
        e_r = p_re.transpose(0, 2, 1)[..., None]
        e_i = p_im.transpose(0, 2, 1)[..., None]
        m_re = c_r * e_r - c_i * e_i
        m_im = c_r * e_i + c_i * e_r
        return m_re, -m_im

    rf_re, rf_im = out_map(pw_re[0, :, 1:t + 1], pw_im[0, :, 1:t + 1], 0)
    rb_re, rb_im = out_map(pw_re[1, :, 1:t + 1][:, ::-1], pw_im[1, :, 1:t + 1][:, ::-1], 1)
    rm = jnp.concatenate([rf_re, rf_im, rb_re, rb_im], axis=1).reshape(S5_GROUPS, 4 * S5_STATE, S5_CW)

    lev = (t * 2.0 ** jnp.arange(n_levels, dtype=F32))[:, None]
    lmag = jnp.exp(are[:, :, None, :] * lev)
    l_re = lmag * jnp.cos(aim[:, :, None, :] * lev)
    l_im = lmag * jnp.sin(aim[:, :, None, :] * lev)
    row_a = jnp.concatenate([l_re[0], l_re[0], l_re[1], l_re[1]], axis=-1)
    row_b = jnp.concatenate([-l_im[0], l_im[0], -l_im[1], l_im[1]], axis=-1)
    atab = jnp.stack([row_a, row_b], axis=2)
    return kcat, pm.astype(BF16), rm.astype(BF16), atab


def _shift_rows(h, d, down):
    n = h.shape[0]
    if d >= n:
        return jnp.zeros_like(h)
    row = lax.broadcasted_iota(jnp.int32, h.shape, 0)
    if down:
        return jnp.where(row >= d, pltpu.roll(h, d, 0), 0.0)
    return jnp.where(row < n - d, pltpu.roll(h, n - d, 0), 0.0)


S5_SLAB_GROUPS = LANE // S5_GROUP
S5_PIECES = S5_CW // LANE
S5_PER_PIECE = LANE // S5_GROUP


def _s5_kernel(u_ref, perm_ref, kc_ref, pm_ref, rm_ref, at_ref, h0_ref, y_ref, ht_ref,
               x_scr, yg_scr, tz_scr, hin_scr, *, nb, nc, n_levels):
    half = 2 * S5_STATE
    row = lax.broadcasted_iota(jnp.int32, (nc, half), 0)

    for v in range(S5_PIECES):
        a = jnp.concatenate(
            [jnp.concatenate([u_ref[b, S5_PER_PIECE * v + sl] for b in range(nb)], axis=0)
             for sl in range(S5_PER_PIECE)], axis=1)
        xp = jnp.dot(a.astype(BF16), perm_ref[...], preferred_element_type=F32).astype(BF16)
        for gl in range(S5_SLAB_GROUPS):
            x_scr[gl, :, v * LANE:(v + 1) * LANE] = xp[:, gl * LANE:(gl + 1) * LANE]

    def group(g, _):
        kc = kc_ref[g]
        for s in range(S5_CHUNK):
            off = (S5_CHUNK - 1 - s) * S5_GROUP
            tz_scr[s * S5_GROUP:(s + 1) * S5_GROUP, :] = kc[:, off:off + S5_CW].astype(BF16)
        x = x_scr[g]
        s_all = jnp.dot(x, pm_ref[g], preferred_element_type=F32)

        def cmul(h, k, lo):
            a = at_ref[g, k, 0:1, lo:lo + half]
            b = at_ref[g, k, 1:2, lo:lo + half]
            return a * h + b * pltpu.roll(h, S5_STATE, 1)

        for b in range(nb):
            for d in range(2):
                lo = d * half
                h = s_all[b * nc:(b + 1) * nc, lo:lo + half]
                h0 = h0_ref[g, b:b + 1, lo:lo + half]
                edge = 0 if d == 0 else nc - 1
                h = h + jnp.where(row == edge, cmul(jnp.broadcast_to(h0, (nc, half)), 0, lo), 0.0)
                for k in range(n_levels):
                    if (1 << k) < nc:
                        h = h + cmul(_shift_rows(h, 1 << k, d == 0), k, lo)
                ht_ref[g, b:b + 1, lo:lo + half] = h[nc - 1 - edge:nc - edge]
                hin = _shift_rows(h, 1, d == 0)
                hin = jnp.where(row == edge, jnp.broadcast_to(h0, (nc, half)), hin)
                hin_scr[b * nc:(b + 1) * nc, lo:lo + half] = hin

        y = jnp.dot(x, tz_scr[...], preferred_element_type=F32)
        y = y + jnp.dot(hin_scr[...].astype(BF16), rm_ref[g], preferred_element_type=F32)
        yg_scr[g] = y.astype(BF16)
        return 0
    lax.fori_loop(0, S5_SLAB_GROUPS, group, 0)

    for v in range(S5_PIECES):
        cat = jnp.concatenate([yg_scr[gl, :, v * LANE:(v + 1) * LANE] for gl in range(S5_SLAB_GROUPS)], axis=1)
        yp = jnp.dot(cat, perm_ref[...], preferred_element_type=F32)
        for tl in range(S5_PER_PIECE):
            for b in range(nb):
                y_ref[b, S5_PER_PIECE * v + tl] = yp[b * nc:(b + 1) * nc, tl * LANE:(tl + 1) * LANE].astype(y_ref.dtype)


def _s5_scan(u, perm, kcat, pm, rm, atab, h0, layer):
    nb, _, nc, _ = u.shape
    rows = nb * nc
    n_levels = atab.shape[2]
    gs = S5_SLAB_GROUPS
    kern = functools.partial(_s5_kernel, nb=nb, nc=nc, n_levels=n_levels)
    tok_spec = pl.BlockSpec((nb, S5_CHUNK, nc, LANE), lambda i: (0, 0, 0, i))
    return pl.pallas_call(
        kern,
        grid=(S5_GROUPS // gs,),
        in_specs=[tok_spec,
                  pl.BlockSpec((gs * LANE, gs * LANE), lambda i: (0, 0)),
                  pl.BlockSpec((None, gs, S5_GROUP, 2 * S5_CW), lambda i: (layer, i, 0, 0)),
                  pl.BlockSpec((None, gs, S5_CW, 4 * S5_STATE), lambda i: (layer, i, 0, 0)),
                  pl.BlockSpec((None, gs, 4 * S5_STATE, S5_CW), lambda i: (layer, i, 0, 0)),
                  pl.BlockSpec((None, gs, n_levels, 2, 4 * S5_STATE), lambda i: (layer, i, 0, 0, 0)),
                  pl.BlockSpec((gs, nb, 4 * S5_STATE), lambda i: (i, 0, 0))],
        out_specs=[tok_spec,
                   pl.BlockSpec((gs, nb, 4 * S5_STATE), lambda i: (i, 0, 0))],
        out_shape=[jax.ShapeDtypeStruct(u.shape, u.dtype),
                   jax.ShapeDtypeStruct((S5_GROUPS, nb, 4 * S5_STATE), F32)],
        scratch_shapes=[pltpu.VMEM((gs, rows, S5_CW), BF16), pltpu.VMEM((gs, rows, S5_CW), BF16),
                        pltpu.VMEM((S5_CW, S5_CW), BF16), pltpu.VMEM((rows, 4 * S5_STATE), F32)],
        compiler_params=_cparams(("arbitrary",)),
        name="s5_scan",
    )(u, perm, kcat, pm, rm, atab, h0)


def _s5_chunks(p, nb, t):
    nc = t // S5_CHUNK
    u = p[:, :, P_U:P_U + S5_WIDTH].reshape(nb, nc, S5_CHUNK, S5_WIDTH).transpose(0, 2, 1, 3)
    return u if nc % 16 == 0 else u.astype(F32)


def _s5_unchunk(y, nb, t):
    return y.transpose(0, 2, 1, 3).reshape(nb, t, S5_WIDTH)


def _s5_perm():
    a = jnp.arange(S5_SLAB_GROUPS * LANE)
    s, g, i = a // LANE, (a % LANE) // S5_GROUP, a % S5_GROUP
    dst = g * LANE + s * S5_GROUP + i
    return (dst[:, None] == a[None, :]).astype(BF16)


def _lru_kernel(x_ref, cw_ref, cb_ref, wg_ref, bg_ref, lam_ref, h0_ref, out_ref, ht_ref,
                xp_scr, xc_scr, af_scr, bf_scr, ab_scr, bb_scr, cf_scr, cr_scr, *, w, s, pad_top, ch, ns):
    l = w * s
    n_ch = l // ch
    slabs = range(ns)
    lanes = [slice(k * LRU_SLAB, (k + 1) * LRU_SLAB) for k in slabs]
    zero_slab = jnp.zeros((w, LRU_SLAB), F32)
    one_slab = jnp.ones((w, LRU_SLAB), F32)

    for k in slabs:
        xp_scr[k, 0:pad_top, :] = jnp.zeros((pad_top, LRU_SLAB), F32)
        xp_scr[k, pad_top + l:pad_top + l + pad_top, :] = jnp.zeros((pad_top, LRU_SLAB), F32)

    def copy_body(i, _):
        r0 = pl.multiple_of(i * ch, ch)
        for k in slabs:
            xp_scr[k, pl.ds(pad_top + r0, ch), :] = x_ref[pl.ds(r0, ch), lanes[k]].astype(F32)
        return 0
    lax.fori_loop(0, n_ch, copy_body, 0)

    def conv_body(i, _):
        r0 = pl.multiple_of(i * ch, ch)
        for k in slabs:
            acc = cb_ref[:, lanes[k]] + cw_ref[2:3, lanes[k]] * xp_scr[k, pl.ds(pad_top + r0, ch), :]
            acc = acc + cw_ref[0:1, lanes[k]] * xp_scr[k, pl.ds(pad_top + r0 - 2 * w, ch), :]
            acc = acc + cw_ref[1:2, lanes[k]] * xp_scr[k, pl.ds(pad_top + r0 - w, ch), :]
            acc = acc + cw_ref[3:4, lanes[k]] * xp_scr[k, pl.ds(pad_top + r0 + w, ch), :]
            xc_scr[k, pl.ds(r0, ch), :] = acc
        return 0
    lax.fori_loop(0, n_ch, conv_body, 0)

    if w > 1:
        def prev_col(v):
            return _shift_rows(v, 1, True)

        def next_col(v):
            return _shift_rows(v, 1, False)

        for k in slabs:
            x_last = xp_scr[k, pad_top + (s - 1) * w:pad_top + s * w, :]
            x_last2 = xp_scr[k, pad_top + (s - 2) * w:pad_top + (s - 1) * w, :]
            x_first = xp_scr[k, pad_top:pad_top + w, :]
            w0, w1, w3 = cw_ref[0:1, lanes[k]], cw_ref[1:2, lanes[k]], cw_ref[3:4, lanes[k]]
            xc_scr[k, 0:w, :] = xc_scr[k, 0:w, :] + w0 * prev_col(x_last2) + w1 * prev_col(x_last)
            xc_scr[k, w:2 * w, :] = xc_scr[k, w:2 * w, :] + w0 * prev_col(x_last)
            xc_scr[k, (s - 1) * w:s * w, :] = xc_scr[k, (s - 1) * w:s * w, :] + w3 * next_col(x_first)

    c_half = (0.5 * LRU_C) * -_softplus(-lam_ref[...])

    def coef_body(i, _):
        r0 = pl.multiple_of(i * ch, ch)
        for k in slabs:
            xc = xc_scr[k, pl.ds(r0, ch), :]
            hx = 0.5 * xc
            g = jnp.dot(xc.astype(BF16), wg_ref[k], preferred_element_type=F32) + bg_ref[k]
            for d, (a_scr, b_scr) in enumerate(((af_scr, bf_scr), (ab_scr, bb_scr))):
                t_r = jnp.tanh(g[:, (2 * d) * LRU_SLAB:(2 * d + 1) * LRU_SLAB])
                t_i = jnp.tanh(g[:, (2 * d + 1) * LRU_SLAB:(2 * d + 2) * LRU_SLAB])
                c = c_half[d:d + 1, lanes[k]]
                a = jnp.exp(c * t_r + c)
                a_scr[k, pl.ds(r0, ch), :] = a
                b_scr[k, pl.ds(r0, ch), :] = jnp.sqrt(1.0 - a * a) * (hx * (t_i + 1.0))
        return 0
    lax.fori_loop(0, n_ch, coef_body, 0)

    def scan_body(r, carry):
        rf = pl.multiple_of(r * w, w)
        rb = pl.multiple_of((s - 1 - r) * w, w)
        out = []
        for k in slabs:
            hf, pf, hb, pb = carry[k]
            a = af_scr[k, pl.ds(rf, w), :]
            hf = a * hf + bf_scr[k, pl.ds(rf, w), :]
            pf = a * pf
            bf_scr[k, pl.ds(rf, w), :] = hf
            af_scr[k, pl.ds(rf, w), :] = pf
            a = ab_scr[k, pl.ds(rb, w), :]
            hb = a * hb + bb_scr[k, pl.ds(rb, w), :]
            pb = a * pb
            bb_scr[k, pl.ds(rb, w), :] = hb
            ab_scr[k, pl.ds(rb, w), :] = pb
            out.append((hf, pf, hb, pb))
        return tuple(out)
    lax.fori_loop(0, s, scan_body, tuple((zero_slab, one_slab, zero_slab, one_slab) for _ in slabs))

    def carry_f(c, carry):
        out = []
        for k in slabs:
            cf_scr[k, pl.ds(c, 1), :] = carry[k]
            out.append(bf_scr[k, pl.ds((s - 1) * w + c, 1), :] + af_scr[k, pl.ds((s - 1) * w + c, 1), :] * carry[k])
        return tuple(out)
    fin = lax.fori_loop(0, w, carry_f, tuple(h0_ref[0:1, lanes[k]] for k in slabs))
    for k in slabs:
        ht_ref[0:1, lanes[k]] = fin[k]

    def carry_b(i, carry):
        c = w - 1 - i
        out = []
        for k in slabs:
            cr_scr[k, pl.ds(c, 1), :] = carry[k]
            out.append(bb_scr[k, pl.ds(c, 1), :] + ab_scr[k, pl.ds(c, 1), :] * carry[k])
        return tuple(out)
    fin = lax.fori_loop(0, w, carry_b, tuple(h0_ref[1:2, lanes[k]] for k in slabs))
    for k in slabs:
        ht_ref[1:2, lanes[k]] = fin[k]

    def out_body(r, _):
        r0 = pl.multiple_of(r * w, w)
        for k in slabs:
            y = (bf_scr[k, pl.ds(r0, w), :] + af_scr[k, pl.ds(r0, w), :] * cf_scr[k]
                 + bb_scr[k, pl.ds(r0, w), :] + ab_scr[k, pl.ds(r0, w), :] * cr_scr[k])
            bf_scr[k, pl.ds(r0, w), :] = y
        return 0
    lax.fori_loop(0, s, out_body, 0)

    def store_body(i, _):
        r0 = pl.multiple_of(i * ch, ch)
        for k in slabs:
            out_ref[pl.ds(r0, ch), lanes[k]] = bf_scr[k, pl.ds(r0, ch), :].astype(out_ref.dtype)
        return 0
    lax.fori_loop(0, n_ch, store_body, 0)


def _lru_scan(p, cw, cb, wg, bg, lam, h0, w, ns, layer):
    b, t, _ = p.shape
    s = t // w
    assert s * w == t and s >= 4
    ch = _tile(t, 256)
    pad_top = max(2 * w, 8)
    cwid = ns * LRU_SLAB
    col0 = P_LX // cwid
    kern = functools.partial(_lru_kernel, w=w, s=s, pad_top=pad_top, ch=ch, ns=ns)
    return pl.pallas_call(
        kern,
        grid=(b, LRU_WIDTH // cwid),
        in_specs=[pl.BlockSpec((None, t, cwid), lambda b_, k: (b_, 0, col0 + k)),
                  pl.BlockSpec((LRU_CONV, cwid), lambda b_, k: (0, k)),
                  pl.BlockSpec((1, cwid), lambda b_, k: (0, k)),
                  pl.BlockSpec((None, ns, LRU_SLAB, 4 * LRU_SLAB), lambda b_, k: (layer, k, 0, 0)),
                  pl.BlockSpec((None, ns, 1, 4 * LRU_SLAB), lambda b_, k: (layer, k, 0, 0)),
                  pl.BlockSpec((2, cwid), lambda b_, k: (0, k)),
                  pl.BlockSpec((None, 2, cwid), lambda b_, k: (b_, 0, k))],
        out_specs=[pl.BlockSpec((None, t, cwid), lambda b_, k: (b_, 0, k)),
                   pl.BlockSpec((None, 2, cwid), lambda b_, k: (b_, 0, k))],
        out_shape=[jax.ShapeDtypeStruct((b, t, LRU_WIDTH), BF16),
                   jax.ShapeDtypeStruct((b, 2, LRU_WIDTH), F32)],
        scratch_shapes=[pltpu.VMEM((ns, t + 2 * pad_top, LRU_SLAB), F32),
                        pltpu.VMEM((ns, t, LRU_SLAB), F32),
                        pltpu.VMEM((ns, t, LRU_SLAB), F32), pltpu.VMEM((ns, t, LRU_SLAB), F32),
                        pltpu.VMEM((ns, t, LRU_SLAB), F32), pltpu.VMEM((ns, t, LRU_SLAB), F32),
                        pltpu.VMEM((ns, w, LRU_SLAB), F32), pltpu.VMEM((ns, w, LRU_SLAB), F32)],
        compiler_params=_cparams(("arbitrary", "arbitrary")),
        name="lru_scan",
    )(p, cw, cb, wg, bg, lam, h0)


def _lru_gate_weights(w_a, b_a, w_x, b_x):
    n_slab = LRU_WIDTH // LRU_SLAB
    per = LRU_SLAB // LRU_BLOCK

    def slab_diag(wm):
        wm = wm.reshape(n_slab, per, LRU_BLOCK, LRU_BLOCK)
        eye = jnp.eye(per, dtype=wm.dtype)
        return jnp.einsum('spkj,pq->spkqj', wm, eye).reshape(n_slab, LRU_SLAB, LRU_SLAB)

    wg = jnp.concatenate([slab_diag(w_a[0]), slab_diag(w_x[0]), slab_diag(w_a[1]), slab_diag(w_x[1])], axis=2)
    bg = jnp.concatenate([b_a[0].reshape(n_slab, 1, LRU_SLAB), b_x[0].reshape(n_slab, 1, LRU_SLAB),
                          b_a[1].reshape(n_slab, 1, LRU_SLAB), b_x[1].reshape(n_slab, 1, LRU_SLAB)], axis=2)
    return (0.5 * wg).astype(BF16), (0.5 * bg).astype(F32)


def _ssd_core(xs, bm, cm, dtr, alog, dbias, e_ref, h_scr, y_scr, reverse):
    q = SSD_CHUNK
    a = -jnp.exp(alog)
    dt = _softplus(dtr + dbias)
    adt = dt * a
    row = lax.broadcasted_iota(jnp.int32, (q, q), 0)
    col = lax.broadcasted_iota(jnp.int32, (q, q), 1)
    tri = (col >= row) if reverse else (col <= row)
    cs = jnp.dot(tri.astype(F32), adt, preferred_element_type=F32, precision=HIGHEST)
    cs_t = cs.T
    dt_t = dt.T
    end = 0 if reverse else q - 1
    tot = cs[end:end + 1, :]
    e = e_ref[...]
    w_x = _split_dot(dt * jnp.exp(tot - cs), e)
    ecs_x = _split_dot(jnp.exp(cs), e)
    xsb = xs.astype(BF16)
    xw = (xs * w_x).astype(BF16)
    decay_row = ecs_x[end:end + 1, :]

    for g in range(SSD_GROUPS):
        cg = cm[:, g * SSD_STATE:(g + 1) * SSD_STATE]
        bg = bm[:, g * SSD_STATE:(g + 1) * SSD_STATE]
        sc = lax.dot_general(cg, bg, (((1,), (1,)), ((), ())), preferred_element_type=F32)
        for j in range(SSD_HPG):
            hd = g * SSD_HPG + j
            cols = slice(hd * SSD_HEAD_DIM, (hd + 1) * SSD_HEAD_DIM)
            diff = cs[:, hd:hd + 1] - cs_t[hd:hd + 1, :]
            lm = jnp.where(tri, jnp.exp(diff), 0.0) * dt_t[hd:hd + 1, :]
            y_scr[:, cols] = jnp.dot((sc * lm).astype(BF16), xsb[:, cols], preferred_element_type=F32)
        ch = slice(g * SSD_GW, (g + 1) * SSD_GW)
        rows = slice(g * SSD_STATE, (g + 1) * SSD_STATE)
        h_old = h_scr[rows, :]
        y_off = jnp.dot(cg, h_old.astype(BF16), preferred_element_type=F32) * ecs_x[:, ch]
        y_scr[:, ch] = y_scr[:, ch] + y_off
        st = lax.dot_general(bg, xw[:, ch], (((0,), (0,)), ((), ())), preferred_element_type=F32)
        h_scr[rows, :] = decay_row[:, ch] * h_old + st
    return y_scr[...]


def _ssd_fwd_kernel(x_ref, xprev_ref, xnext_ref, dt_ref, cw_ref, cb_ref, alog_ref, dbias_ref, e_ref, h0_ref,
                    y_ref, xc_ref, ht_ref, h_scr, y_scr, xcf_scr, *, nc):
    c = pl.program_id(1)

    @pl.when(c == 0)
    def _():
        h_scr[...] = h0_ref[...]

    q = x_ref.shape[0]
    x = x_ref[...].astype(F32)
    prev = jnp.where(c > 0, xprev_ref[...].astype(F32), 0.0)
    nxt = jnp.where(c < nc - 1, xnext_ref[...].astype(F32), 0.0)
    row8 = lax.broadcasted_iota(jnp.int32, (8, SSD_XBC), 0)

    def behind(k):
        xr = pltpu.roll(x, k, 0)
        top = jnp.where(row8 < k, pltpu.roll(prev, k, 0)[0:8], xr[0:8])
        return jnp.concatenate([top, xr[8:]], axis=0)

    xm2, xm1 = behind(2), behind(1)
    xr = pltpu.roll(x, q - 1, 0)
    bottom = jnp.where(row8 == 7, pltpu.roll(nxt, 15, 0)[8:16], xr[q - 8:q])
    xp1 = jnp.concatenate([xr[:q - 8], bottom], axis=0)
    conv = (cb_ref[...] + cw_ref[0:1, :] * xm2 + cw_ref[1:2, :] * xm1
            + cw_ref[2:3, :] * x + cw_ref[3:4, :] * xp1)
    xc = _silu(conv)
    xc_ref[...] = xc.astype(xc_ref.dtype)
    xcf_scr[...] = xc
    for k in range(q // SSD_CHUNK):
        r = slice(k * SSD_CHUNK, (k + 1) * SSD_CHUNK)
        xs = xcf_scr[r, :SSD_INNER]
        bm = xcf_scr[r, SSD_INNER:SSD_INNER + SSD_GROUPS * SSD_STATE].astype(BF16)
        cm = xcf_scr[r, SSD_INNER + SSD_GROUPS * SSD_STATE:].astype(BF16)
        y_ref[r, :] = _ssd_core(xs, bm, cm, dt_ref[r, :], alog_ref[...], dbias_ref[...], e_ref, h_scr, y_scr, False)

    @pl.when(c == nc - 1)
    def _():
        ht_ref[...] = h_scr[...]


def _ssd_bwd_kernel(xc_ref, dt_ref, yf_ref, alog_ref, dbias_ref, dskip_ref, e_ref, h0_ref,
                    y_ref, ht_ref, h_scr, y_scr, *, nc):
    c = pl.program_id(1)

    @pl.when(c == 0)
    def _():
        h_scr[...] = h0_ref[...]

    for k in reversed(range(xc_ref.shape[0] // SSD_CHUNK)):
        r = slice(k * SSD_CHUNK, (k + 1) * SSD_CHUNK)
        xs = xc_ref[r, :SSD_INNER].astype(F32)
        bm = xc_ref[r, SSD_INNER:SSD_INNER + SSD_GROUPS * SSD_STATE]
        cm = xc_ref[r, SSD_INNER + SSD_GROUPS * SSD_STATE:]
        y = _ssd_core(xs, bm, cm, dt_ref[r, :], alog_ref[...], dbias_ref[...], e_ref, h_scr, y_scr, True)
        y_ref[r, :] = y + yf_ref[r, :] + dskip_ref[...] * xs

    @pl.when(c == nc - 1)
    def _():
        ht_ref[...] = h_scr[...]


def _ssd_scan(p, dt, cw, cb, a_log, dt_bias, dskip_x, e_mat, h0f, h0b):
    b, t, _ = p.shape
    q = min(SSD_STEP_CHUNKS, t // SSD_CHUNK) * SSD_CHUNK
    nc = t // q
    assert nc * q == t
    xbc_blk = P_XBC // SSD_XBC
    n16 = t // 16
    hs = (SSD_GROUPS * SSD_STATE, SSD_GW)
    const2 = lambda b_, c: (0, 0)
    state_spec = pl.BlockSpec((None,) + hs, lambda b_, c: (b_, 0, 0))

    yf, xc, htf = pl.pallas_call(
        functools.partial(_ssd_fwd_kernel, nc=nc),
        grid=(b, nc),
        in_specs=[pl.BlockSpec((None, q, SSD_XBC), lambda b_, c: (b_, c, xbc_blk)),
                  pl.BlockSpec((None, 16, SSD_XBC),
                               lambda b_, c: (b_, jnp.maximum(c * (q // 16) - 1, 0), xbc_blk)),
                  pl.BlockSpec((None, 16, SSD_XBC),
                               lambda b_, c: (b_, jnp.minimum((c + 1) * (q // 16), n16 - 1), xbc_blk)),
                  pl.BlockSpec((None, q, LANE), lambda b_, c: (b_, c, 0)),
                  pl.BlockSpec((SSD_CONV, SSD_XBC), const2),
                  pl.BlockSpec((1, SSD_XBC), const2),
                  pl.BlockSpec((1, LANE), const2),
                  pl.BlockSpec((1, LANE), const2),
                  pl.BlockSpec((LANE, SSD_INNER), const2),
                  state_spec],
        out_specs=[pl.BlockSpec((None, q, SSD_INNER), lambda b_, c: (b_, c, 0)),
                   pl.BlockSpec((None, q, SSD_XBC), lambda b_, c: (b_, c, 0)),
                   state_spec],
        out_shape=[jax.ShapeDtypeStruct((b, t, SSD_INNER), F32),
                   jax.ShapeDtypeStruct((b, t, SSD_XBC), BF16),
                   jax.ShapeDtypeStruct((b,) + hs, F32)],
        scratch_shapes=[pltpu.VMEM(hs, F32), pltpu.VMEM((SSD_CHUNK, SSD_INNER), F32),
                        pltpu.VMEM((q, SSD_XBC), F32)],
        compiler_params=_cparams(("arbitrary", "arbitrary")),
        name="ssd_fwd",
    )(p, p, p, dt, cw, cb, a_log[0:1], dt_bias[0:1], e_mat, h0f)

    rev = lambda b_, c: (b_, nc - 1 - c, 0)
    y, htb = pl.pallas_call(
        functools.partial(_ssd_bwd_kernel, nc=nc),
        grid=(b, nc),
        in_specs=[pl.BlockSpec((None, q, SSD_XBC), rev),
                  pl.BlockSpec((None, q, LANE), rev),
                  pl.BlockSpec((None, q, SSD_INNER), rev),
                  pl.BlockSpec((1, LANE), const2),
                  pl.BlockSpec((1, LANE), const2),
                  pl.BlockSpec((1, SSD_INNER), const2),
                  pl.BlockSpec((LANE, SSD_INNER), const2),
                  state_spec],
        out_specs=[pl.BlockSpec((None, q, SSD_INNER), rev), state_spec],
        out_shape=[jax.ShapeDtypeStruct((b, t, SSD_INNER), F32),
                   jax.ShapeDtypeStruct((b,) + hs, F32)],
        scratch_shapes=[pltpu.VMEM(hs, F32), pltpu.VMEM((SSD_CHUNK, SSD_INNER), F32)],
        compiler_params=_cparams(("arbitrary", "arbitrary")),
        name="ssd_bwd",
    )(xc, dt, yf, a_log[1:2], dt_bias[1:2], dskip_x, e_mat, h0b)
    return y, htf, htb


def _merge_kernel(x_ref, mod_ref, u_ref, ya_ref, yb_ref, g_ref, yc_ref, z_ref, gates_ref,
                  d5_ref, wglu_ref, snw_ref, wa_ref, wb_ref, wc_ref, wo_ref, xo_ref):
    ya = jax.nn.gelu(ya_ref[...].astype(F32) + d5_ref[...] * u_ref[...].astype(F32))
    ya = ya * _sigmoid(jnp.dot(ya.astype(BF16), wglu_ref[...], preferred_element_type=F32))
    yb = yb_ref[...].astype(F32) * jax.nn.gelu(g_ref[...].astype(F32))
    yc = yc_ref[...] * _silu(z_ref[...].astype(F32))
    yc = yc * lax.rsqrt(jnp.mean(yc * yc, axis=-1, keepdims=True) + EPS) * snw_ref[...]
    mix = _sigmoid(gates_ref[:, 0:D_MODEL].astype(F32)) * jnp.dot(
        ya.astype(BF16), wa_ref[...], preferred_element_type=F32)
    mix = mix + _sigmoid(gates_ref[:, D_MODEL:2 * D_MODEL].astype(F32)) * jnp.dot(
        yb.astype(BF16), wb_ref[...], preferred_element_type=F32)
    mix = mix + _sigmoid(gates_ref[:, 2 * D_MODEL:3 * D_MODEL].astype(F32)) * jnp.dot(
        yc.astype(BF16), wc_ref[...], preferred_element_type=F32)
    y = jnp.dot(mix.astype(BF16), wo_ref[...], preferred_element_type=F32)
    xo_ref[...] = x_ref[...] + mod_ref[2:3, :] * y


def _merge(x, mod, p, ya, yb, yc, d5, wglu, snw, wa, wb, wc, wo):
    b, t, _ = x.shape
    tm = _tile(t, 512)
    tok = lambda width, blk: pl.BlockSpec((None, tm, width), lambda b_, i: (b_, i, blk))
    full = lambda r, c_: pl.BlockSpec((r, c_), lambda b_, i: (0, 0))
    return pl.pallas_call(
        _merge_kernel,
        grid=(b, t // tm),
        in_specs=[tok(D_MODEL, 0),
                  pl.BlockSpec((None, 8, D_MODEL), lambda b_, i: (b_, 0, 0)),
                  tok(S5_WIDTH, P_U // S5_WIDTH),
                  tok(S5_WIDTH, 0),
                  tok(LRU_WIDTH, 0),
                  tok(LRU_WIDTH, P_LG // LRU_WIDTH),
                  tok(SSD_INNER, 0),
                  tok(SSD_INNER, P_Z // SSD_INNER),
                  tok(3 * D_MODEL, P_GATE // (3 * D_MODEL)),
                  full(1, S5_WIDTH), full(S5_WIDTH, S5_WIDTH), full(1, SSD_INNER),
                  full(S5_WIDTH, D_MODEL), full(LRU_WIDTH, D_MODEL), full(SSD_INNER, D_MODEL),
                  full(D_MODEL, D_MODEL)],
        out_specs=tok(D_MODEL, 0),
        out_shape=jax.ShapeDtypeStruct((b, t, D_MODEL), F32),
        compiler_params=_cparams(("arbitrary", "arbitrary")),
        name="merge",
    )(x, mod, p, ya, yb, p, yc, p, p, d5, wglu, snw, wa, wb, wc, wo)


def _ffn_kernel(x_ref, mod_ref, nw_ref, w1_ref, w3_ref, w2_ref, fnw_ref, xo_ref, h_scr, acc_scr, *, nf, final_norm):
    f = pl.program_id(2)

    @pl.when(f == 0)
    def _():
        h_scr[...] = _rms_mod(x_ref[...], nw_ref[...], mod_ref[3:4, :], mod_ref[4:5, :]).astype(BF16)
        acc_scr[...] = jnp.zeros_like(acc_scr)

    h = h_scr[...]
    a = jnp.dot(h, w1_ref[...], preferred_element_type=F32)
    g = jnp.dot(h, w3_ref[...], preferred_element_type=F32)
    acc_scr[...] += jnp.dot((_silu(a) * g).astype(BF16), w2_ref[...], preferred_element_type=F32)

    @pl.when(f == nf - 1)
    def _():
        xo = x_ref[...] + mod_ref[5:6, :] * acc_scr[...]
        if final_norm:
            xo = xo * lax.rsqrt(jnp.mean(xo * xo, axis=-1, keepdims=True) + EPS) * fnw_ref[...]
        xo_ref[...] = xo


def _ffn(x, mod, nw, w1, w3, w2, fnw, final_norm):
    b, t, _ = x.shape
    dff = w1.shape[1]
    tm = _tile(t, 512)
    tf = dff // 2
    nf = dff // tf
    return pl.pallas_call(
        functools.partial(_ffn_kernel, nf=nf, final_norm=final_norm),
        grid=(b, t // tm, nf),
        in_specs=[pl.BlockSpec((None, tm, D_MODEL), lambda b_, i, f: (b_, i, 0)),
                  pl.BlockSpec((None, 8, D_MODEL), lambda b_, i, f: (b_, 0, 0)),
                  pl.BlockSpec((1, D_MODEL), lambda b_, i, f: (0, 0)),
                  pl.BlockSpec((D_MODEL, tf), lambda b_, i, f: (0, f)),
                  pl.BlockSpec((D_MODEL, tf), lambda b_, i, f: (0, f)),
                  pl.BlockSpec((tf, D_MODEL), lambda b_, i, f: (f, 0)),
                  pl.BlockSpec((1, D_MODEL), lambda b_, i, f: (0, 0))],
        out_specs=pl.BlockSpec((None, tm, D_MODEL), lambda b_, i, f: (b_, i, 0)),
        out_shape=jax.ShapeDtypeStruct((b, t, D_MODEL), F32),
        scratch_shapes=[pltpu.VMEM((tm, D_MODEL), BF16), pltpu.VMEM((tm, D_MODEL), F32)],
        compiler_params=_cparams(("arbitrary", "arbitrary", "arbitrary")),
        name="ffn",
    )(x, mod, nw, w1, w3, w2, fnw)


MOE_BLK = 512
MOE_PER_EBLK = 2
SEL_LANE = 8


def _router_kernel(x_ref, mod_ref, nw_ref, wr_ref, br_ref, h_ref, g_ref):
    h = _rms_mod(x_ref[...], nw_ref[...], mod_ref[3:4, :], mod_ref[4:5, :])
    h_ref[...] = h.astype(BF16)
    logits = jnp.dot(h, wr_ref[...], preferred_element_type=F32, precision=HIGHEST) + br_ref[...]
    lane = lax.broadcasted_iota(jnp.int32, logits.shape, 1)
    logits = jnp.where(lane < N_EXPERTS, logits, -jnp.inf)
    t1 = jnp.max(logits, axis=-1, keepdims=True)
    i1 = jnp.min(jnp.where(logits == t1, lane, LANE), axis=-1, keepdims=True)
    rest = jnp.where(lane == i1, -jnp.inf, logits)
    t2 = jnp.max(rest, axis=-1, keepdims=True)
    i2 = jnp.min(jnp.where(rest == t2, lane, LANE), axis=-1, keepdims=True)
    e2 = jnp.exp(t2 - t1)
    den = 1.0 + e2
    gate = jnp.where(lane == i1, 1.0 / den, 0.0) + jnp.where(lane == i2, e2 / den, 0.0)
    flag = jnp.where((lane == i1 + SEL_LANE) | (lane == i2 + SEL_LANE), 1.0, 0.0)
    g_ref[...] = gate + flag


def _router(x, mod, nw, wr, br):
    b, t, _ = x.shape
    tm = _tile(t, 512)
    return pl.pallas_call(
        _router_kernel,
        grid=(b, t // tm),
        in_specs=[pl.BlockSpec((None, tm, D_MODEL), lambda b_, i: (b_, i, 0)),
                  pl.BlockSpec((None, 8, D_MODEL), lambda b_, i: (b_, 0, 0)),
                  pl.BlockSpec((1, D_MODEL), lambda b_, i: (0, 0)),
                  pl.BlockSpec((D_MODEL, LANE), lambda b_, i: (0, 0)),
                  pl.BlockSpec((1, LANE), lambda b_, i: (0, 0))],
        out_specs=[pl.BlockSpec((None, tm, D_MODEL), lambda b_, i: (b_, i, 0)),
                   pl.BlockSpec((None, tm, LANE), lambda b_, i: (b_, i, 0))],
        out_shape=[jax.ShapeDtypeStruct((b, t, D_MODEL), BF16),
                   jax.ShapeDtypeStruct((b, t, LANE), F32)],
        compiler_params=_cparams(("arbitrary", "arbitrary")),
        name="router",
    )(x, mod, nw, wr, br)


def _moe_plan(g, ts):
    n = g.shape[0]
    blk = MOE_BLK
    eblk = MOE_PER_EBLK * MOE_BLK
    nt = n // ts
    n_eblocks = TOP_K * n // eblk + N_EXPERTS
    n_blocks = n_eblocks * MOE_PER_EBLK
    max_pairs = n_blocks + N_EXPERTS * nt
    sel = g[:, SEL_LANE:SEL_LANE + N_EXPERTS] > 0.5
    cum = jnp.cumsum(sel.astype(jnp.int32), axis=0)
    cnt = cum[-1]
    neb = (cnt + eblk - 1) // eblk
    eb_end = jnp.cumsum(neb)
    eb_beg = eb_end - neb
    seg_off = eb_beg * eblk
    n_used = eb_end[-1]
    pos = jnp.where(sel, seg_off[None, :] + cum - 1, -1).astype(jnp.int32)
    eb = jnp.arange(n_eblocks)
    bexp = jnp.minimum(jnp.sum(eb_end[None, :] <= eb[:, None], axis=1), N_EXPERTS - 1)

    tile_end = cum[ts - 1::ts]
    tile_beg = jnp.concatenate([jnp.zeros((1, N_EXPERTS), jnp.int32), tile_end[:-1]], axis=0)
    d_lo = (seg_off[None, :] + tile_beg) // blk
    d_hi = (seg_off[None, :] + tile_end - 1) // blk
    npairs = jnp.where(tile_end > tile_beg, d_hi - d_lo + 1, 0)

    def expand(cnt_flat, cell_j, cell_e):
        cend = jnp.cumsum(cnt_flat)
        total = cend[-1]
        k = jnp.minimum(jnp.arange(max_pairs), total - 1)
        cell = jnp.sum(cend[None, :] <= k[:, None], axis=1)
        i = k - (cend[cell] - cnt_flat[cell])
        j, e = cell_j[cell], cell_e[cell]
        d = d_lo[j, e] + i
        valid = (jnp.arange(max_pairs) < total).astype(jnp.int32)
        return j.astype(jnp.int32), d.astype(jnp.int32), e.astype(jnp.int32), valid

    jj, ee = jnp.meshgrid(jnp.arange(nt), jnp.arange(N_EXPERTS), indexing='ij')
    by_tile = expand(npairs.reshape(-1), jj.reshape(-1), ee.reshape(-1))

    blk_e = jnp.repeat(bexp, MOE_PER_EBLK)
    q0 = jnp.arange(n_blocks) * blk - seg_off[blk_e]
    q1 = jnp.minimum(cnt[blk_e], q0 + blk) - 1
    ends = tile_end.T[blk_e]
    has = q0 < cnt[blk_e]
    j_lo = jnp.where(has, jnp.sum(ends <= q0[:, None], axis=1), 0)
    j_hi = jnp.where(has, jnp.sum(ends <= q1[:, None], axis=1), -1)

    def edges(key, valid):
        prev = jnp.concatenate([jnp.full((1,), -1, jnp.int32), key[:-1]])
        nxt = jnp.concatenate([key[1:], jnp.full((1,), -1, jnp.int32)])
        nxt_valid = jnp.concatenate([valid[1:], jnp.zeros((1,), jnp.int32)])
        first = ((key != prev) & (valid == 1)).astype(jnp.int32)
        last = (((key != nxt) | (nxt_valid == 0)) & (valid == 1)).astype(jnp.int32)
        return first, last

    tj, td, te, tv = by_tile
    t_first, t_last = edges(tj, tv)
    return dict(pos=pos, pos_t=pos.T, gate=g[:, :N_EXPERTS], bexp=bexp.astype(jnp.int32),
                n_used=n_used.astype(jnp.int32).reshape(1), n_blocks=n_blocks, n_eblocks=n_eblocks,
                max_pairs=max_pairs,
                disp=(blk_e.astype(jnp.int32), j_lo.astype(jnp.int32), j_hi.astype(jnp.int32)),
                comb=(tj, td, te, t_first, t_last, tv))


def _dispatch_kernel(be, jlo, jhi, h_ref, post_ref, xs_ref, acc_scr, *, ts):
    d = pl.program_id(0)
    e = be[d]
    row = d * MOE_BLK + lax.broadcasted_iota(jnp.int32, (MOE_BLK, 1), 0)
    acc_scr[...] = jnp.zeros_like(acc_scr)

    def tile(j, _):
        t0 = pl.multiple_of(j * ts, ts)
        hit = post_ref[pl.ds(e, 1), pl.ds(t0, ts)] == row
        acc_scr[...] += jnp.dot(jnp.where(hit, 1.0, 0.0).astype(BF16), h_ref[pl.ds(t0, ts), :],
                                preferred_element_type=F32)
        return 0
    lax.fori_loop(jlo[d], jhi[d] + 1, tile, 0)
    xs_ref[...] = acc_scr[...].astype(xs_ref.dtype)


def _dispatch(h, plan, ts):
    n = h.shape[0]
    rows = plan['n_blocks'] * MOE_BLK
    grid_spec = pltpu.PrefetchScalarGridSpec(
        num_scalar_prefetch=3,
        grid=(plan['n_blocks'],),
        in_specs=[pl.BlockSpec((n, D_MODEL), lambda d, be, lo, hi: (0, 0), pipeline_mode=pl.Buffered(1)),
                  pl.BlockSpec((N_EXPERTS, n), lambda d, be, lo, hi: (0, 0), pipeline_mode=pl.Buffered(1))],
        out_specs=pl.BlockSpec((MOE_BLK, D_MODEL), lambda d, be, lo, hi: (d, 0)),
        scratch_shapes=[pltpu.VMEM((MOE_BLK, D_MODEL), F32)],
    )
    return pl.pallas_call(
        functools.partial(_dispatch_kernel, ts=ts),
        grid_spec=grid_spec,
        out_shape=jax.ShapeDtypeStruct((rows, D_MODEL), BF16),
        compiler_params=_cparams(("arbitrary",)),
        name="moe_dispatch",
    )(*plan['disp'], h, plan['pos_t'])


def _experts_kernel(bexp, nused, x_ref, w1_ref, w3_ref, w2_ref, y_ref, acc_scr, *, nf):
    d = pl.program_id(0)
    f = pl.program_id(1)

    @pl.when(d < nused[0])
    def _():
        @pl.when(f == 0)
        def _():
            acc_scr[...] = jnp.zeros_like(acc_scr)

        x = x_ref[...]
        a = jnp.dot(x, w1_ref[...].astype(BF16), preferred_element_type=F32)
        g = jnp.dot(x, w3_ref[...].astype(BF16), preferred_element_type=F32)
        acc_scr[...] += jnp.dot((_silu(a) * g).astype(BF16), w2_ref[...].astype(BF16),
                                preferred_element_type=F32)

        @pl.when(f == nf - 1)
        def _():
            y_ref[...] = acc_scr[...].astype(y_ref.dtype)


def _experts(xs, plan, w1, w3, w2):
    dff = w1.shape[2]
    nf = 7
    tf = dff // nf
    assert tf * nf == dff and tf % LANE == 0
    eblk = MOE_PER_EBLK * MOE_BLK

    def blk(d, nused):
        return jnp.minimum(d, nused[0] - 1)

    def fidx(d, f, nused):
        return jnp.where(d < nused[0], f, nf - 1)

    grid_spec = pltpu.PrefetchScalarGridSpec(
        num_scalar_prefetch=2,
        grid=(plan['n_eblocks'], nf),
        in_specs=[pl.BlockSpec((eblk, D_MODEL), lambda d, f, be, nu: (blk(d, nu), 0)),
                  pl.BlockSpec((None, D_MODEL, tf), lambda d, f, be, nu: (be[blk(d, nu)], 0, fidx(d, f, nu))),
                  pl.BlockSpec((None, D_MODEL, tf), lambda d, f, be, nu: (be[blk(d, nu)], 0, fidx(d, f, nu))),
                  pl.BlockSpec((None, tf, D_MODEL), lambda d, f, be, nu: (be[blk(d, nu)], fidx(d, f, nu), 0))],
        out_specs=pl.BlockSpec((eblk, D_MODEL), lambda d, f, be, nu: (blk(d, nu), 0)),
        scratch_shapes=[pltpu.VMEM((eblk, D_MODEL), F32)],
    )
    return pl.pallas_call(
        functools.partial(_experts_kernel, nf=nf),
        grid_spec=grid_spec,
        out_shape=jax.ShapeDtypeStruct(xs.shape, BF16),
        compiler_params=_cparams(("arbitrary", "arbitrary")),
        name="moe_experts",
    )(plan['bexp'], plan['n_used'], xs, w1, w3, w2)


def _combine_kernel(pj, pd, pe, pfirst, plast, pvalid, ys_ref, pos_ref, gate_ref, x_ref, mod_ref, fnw_ref,
                    xo_ref, acc_scr, *, ts, final_norm):
    k = pl.program_id(0)

    @pl.when(pvalid[k] == 1)
    def _():
        e = pe[k]
        lane = lax.broadcasted_iota(jnp.int32, (ts, N_EXPERTS), 1)
        pos_e = jnp.sum(jnp.where(lane == e, pos_ref[...], 0), axis=1, keepdims=True)
        gate_e = jnp.sum(jnp.where(lane == e, gate_ref[...], 0.0), axis=1, keepdims=True)
        row = pd[k] * MOE_BLK + lax.broadcasted_iota(jnp.int32, (1, MOE_BLK), 1)
        hit = pos_e == row
        got = jnp.dot(jnp.where(hit, 1.0, 0.0).astype(BF16), ys_ref[...], preferred_element_type=F32)

        @pl.when(pfirst[k] == 1)
        def _():
            acc_scr[...] = gate_e * got

        @pl.when(pfirst[k] == 0)
        def _():
            acc_scr[...] += gate_e * got

    @pl.when(plast[k] == 1)
    def _():
        xo = x_ref[...] + mod_ref[5:6, :] * acc_scr[...]
        if final_norm:
            xo = xo * lax.rsqrt(jnp.mean(xo * xo, axis=-1, keepdims=True) + EPS) * fnw_ref[...]
        xo_ref[...] = xo


def _combine(ys, plan, x, mod, fnw, ts, final_norm):
    n = x.shape[0]
    tiles_per_batch = n // (mod.shape[0] * ts)
    sp = lambda fn: (lambda k, pj, pd, pe, pf, pl_, pv: fn(k, pj, pd))
    grid_spec = pltpu.PrefetchScalarGridSpec(
        num_scalar_prefetch=6,
        grid=(plan['max_pairs'],),
        in_specs=[pl.BlockSpec((MOE_BLK, D_MODEL), sp(lambda k, pj, pd: (pd[k], 0))),
                  pl.BlockSpec((ts, N_EXPERTS), sp(lambda k, pj, pd: (pj[k], 0))),
                  pl.BlockSpec((ts, N_EXPERTS), sp(lambda k, pj, pd: (pj[k], 0))),
                  pl.BlockSpec((ts, D_MODEL), sp(lambda k, pj, pd: (pj[k], 0))),
                  pl.BlockSpec((None, 8, D_MODEL), sp(lambda k, pj, pd: (pj[k] // tiles_per_batch, 0, 0))),
                  pl.BlockSpec((1, D_MODEL), sp(lambda k, pj, pd: (0, 0)))],
        out_specs=pl.BlockSpec((ts, D_MODEL), sp(lambda k, pj, pd: (pj[k], 0))),
        scratch_shapes=[pltpu.VMEM((ts, D_MODEL), F32)],
    )
    return pl.pallas_call(
        functools.partial(_combine_kernel, ts=ts, final_norm=final_norm),
        grid_spec=grid_spec,
        out_shape=jax.ShapeDtypeStruct((n, D_MODEL), F32),
        compiler_params=_cparams(("arbitrary",)),
        name="moe_combine",
    )(*plan['comb'], ys, plan['pos'], plan['gate'], x, mod, fnw)


def _moe(x, mod, nw, wr, br, w1, w3, w2, fnw, final_norm):
    b, t, _ = x.shape
    ts = _tile(t, 1024)
    h, g = _router(x, mod, nw, wr, br)
    plan = _moe_plan(g.reshape(b * t, LANE), ts)
    xs = _dispatch(h.reshape(b * t, D_MODEL), plan, ts)
    ys = _experts(xs, plan, w1, w3, w2)
    out = _combine(ys, plan, x.reshape(b * t, D_MODEL), mod, fnw, ts, final_norm)
    return out.reshape(b, t, D_MODEL)


def _pad_lanes(v, fill=0.0):
    return jnp.pad(v.astype(F32), (0, LANE - v.shape[0]), constant_values=fill)[None, :]


def kernel(x, c, ctx, c_ctx, w_mod, b_mod, norm_w, w_in, s5_lam_re, s5_lam_im, s5_log_dt, s5_b_re, s5_b_im, s5_c_re, s5_c_im, s5_d, s5_w_glu, lru_conv_w, lru_conv_b, lru_w_a, lru_b_a, lru_w_x, lru_b_x, lru_lam, ssd_conv_w, ssd_conv_b, ssd_a_log, ssd_dt_bias, ssd_d, ssd_norm_w, w_br_a, w_br_b, w_br_c, w_out, ffn_w1, ffn_w3, ffn_w2, moe_w_router, moe_b_router, moe_w1, moe_w3, moe_w2, final_norm_w):
    depth = w_mod.shape[0]
    nb, t_lat, _ = x.shape
    t_ctx = ctx.shape[1]
    assert t_lat % GRID_W == 0 and t_lat % SSD_CHUNK == 0 and t_ctx % SSD_CHUNK == 0

    cond_in = jnp.zeros((8, D_MODEL), F32).at[:nb].set(c).at[nb].set(c_ctx)
    mods = _modulation(cond_in, w_mod, b_mod).reshape(depth, 8, N_MOD, D_MODEL)
    mods = jnp.pad(mods, ((0, 0), (0, 0), (0, 8 - N_MOD), (0, 0)))

    head_of = jnp.arange(SSD_INNER) // SSD_HEAD_DIM
    e_mat = (jnp.arange(LANE)[:, None] == head_of[None, :]).astype(BF16)
    s5_perm = _s5_perm()
    n_lev_lat = max(1, math.ceil(math.log2(t_lat // S5_CHUNK)))
    s5_prep = jax.vmap(functools.partial(_s5_setup, n_levels=n_lev_lat))(
        s5_lam_re, s5_lam_im, s5_log_dt, s5_b_re, s5_b_im, s5_c_re, s5_c_im)
    lru_prep = jax.vmap(_lru_gate_weights)(lru_w_a, lru_b_a, lru_w_x, lru_b_x)

    x_lat, x_ctx = x, ctx
    for l in range(depth):
        ctx_out = l < depth - 1
        last = l == depth - 1
        mod_lat = mods[l, :nb]
        mod_ctx = jnp.broadcast_to(mods[l, nb][None], (nb, 8, D_MODEL))
        nw0 = norm_w[l, 0][None, :]
        nw1 = norm_w[l, 1][None, :]

        wl = w_in[l]
        o_lx = S5_WIDTH
        o_lg = o_lx + LRU_WIDTH
        o_z = o_lg + LRU_WIDTH
        o_xbc = o_z + SSD_INNER
        o_dt = o_xbc + SSD_XBC
        o_g = o_dt + SSD_HEADS
        w_main = jnp.concatenate(
            [wl[:, :o_lx], jnp.zeros((D_MODEL, P_LX - S5_WIDTH), wl.dtype), wl[:, o_lx:o_dt], wl[:, o_g:]],
            axis=1).astype(BF16)
        w_dt = jnp.pad(wl[:, o_dt:o_g], ((0, 0), (0, LANE - SSD_HEADS))).astype(BF16)
        kcat, pm, rm, atab = s5_prep
        wg, bg = lru_prep
        lcw = lru_conv_w[l].astype(F32)
        lcb = lru_conv_b[l][None, :].astype(F32)
        scw = ssd_conv_w[l].astype(F32)
        scb = ssd_conv_b[l][None, :].astype(F32)
        a_log = jnp.concatenate([_pad_lanes(ssd_a_log[l, 0]), _pad_lanes(ssd_a_log[l, 1])], axis=0)
        dt_bias = jnp.concatenate([_pad_lanes(ssd_dt_bias[l, 0]), _pad_lanes(ssd_dt_bias[l, 1])], axis=0)
        dskip_x = jnp.repeat(ssd_d[l].astype(F32), SSD_HEAD_DIM)[None, :]

        def mixers(p, dt, t, w, s5_h0, lru_h0, ssd_h0f, ssd_h0b):
            y5, s5_ht = _s5_scan(_s5_chunks(p, nb, t), s5_perm, kcat, pm, rm, atab, s5_h0, l)
            ya = _s5_unchunk(y5, nb, t)
            yb, lru_ht = _lru_scan(p, lcw, lcb, wg, bg, lru_lam[l].astype(F32), lru_h0, w,
                                   1 if w > 1 else LRU_WIDTH // LRU_SLAB, l)
            yc, ssd_htf, ssd_htb = _ssd_scan(p, dt, scw, scb, a_log, dt_bias, dskip_x, e_mat, ssd_h0f, ssd_h0b)
            return ya, yb, yc, s5_ht, lru_ht, ssd_htf, ssd_htb

        p_ctx, dt_ctx = _inproj(x_ctx, mod_ctx, nw0, w_main, w_dt)
        zs5 = jnp.zeros((S5_GROUPS, nb, 4 * S5_STATE), F32)
        zlru = jnp.zeros((nb, 2, LRU_WIDTH), F32)
        zssd = jnp.zeros((nb, SSD_GROUPS * SSD_STATE, SSD_GW), F32)
        ya_c, yb_c, yc_c, s5_h, lru_h, ssd_hf, ssd_hb = mixers(p_ctx, dt_ctx, t_ctx, 1, zs5, zlru, zssd, zssd)

        p_lat, dt_lat = _inproj(x_lat, mod_lat, nw0, w_main, w_dt)
        ya, yb, yc, _, _, _, _ = mixers(p_lat, dt_lat, t_lat, GRID_W, s5_h, lru_h, ssd_hf, ssd_hb)

        mw = (s5_d[l][None, :].astype(F32), s5_w_glu[l].astype(BF16), ssd_norm_w[l][None, :].astype(F32),
              w_br_a[l].astype(BF16), w_br_b[l].astype(BF16), w_br_c[l].astype(BF16), w_out[l].astype(BF16))
        x_lat = _merge(x_lat, mod_lat, p_lat, ya, yb, yc, *mw)
        if ctx_out:
            x_ctx = _merge(x_ctx, mod_ctx, p_ctx, ya_c, yb_c, yc_c, *mw)

        fnw = final_norm_w[None, :].astype(F32)
        if l % 2 == 0:
            fw = (ffn_w1[l // 2].astype(BF16), ffn_w3[l // 2].astype(BF16), ffn_w2[l // 2].astype(BF16))
            x_lat = _ffn(x_lat, mod_lat, nw1, *fw, fnw, last)
            if ctx_out:
                x_ctx = _ffn(x_ctx, mod_ctx, nw1, *fw, fnw, False)
        else:
            wr = jnp.pad(moe_w_router[l // 2].astype(F32), ((0, 0), (0, LANE - N_EXPERTS)))
            br = _pad_lanes(moe_b_router[l // 2])
            ew = (moe_w1[l // 2], moe_w3[l // 2], moe_w2[l // 2])
            x_lat = _moe(x_lat, mod_lat, nw1, wr, br, *ew, fnw, last)
            if ctx_out:
                x_ctx = _moe(x_ctx, mod_ctx, nw1, wr, br, *ew, fnw, False)
    return x_lat
```

```python
import functools
import math

import jax
import jax.numpy as jnp
from jax import lax
from jax.experimental import pallas as pl
from jax.experimental.pallas import tpu as pltpu

F32 = jnp.float32
BF16 = jnp.bfloat16
HIGHEST = lax.Precision.HIGHEST

D_MODEL = 1024
GRID_W = 64
N_MOD = 6
EPS = 1e-6

S5_WIDTH = 768
S5_GROUP = 16
S5_GROUPS = S5_WIDTH // S5_GROUP
S5_STATE = 64
S5_CHUNK = 32
S5_CW = S5_CHUNK * S5_GROUP

LRU_WIDTH = 1024
LRU_BLOCKS = 16
LRU_BLOCK = LRU_WIDTH // LRU_BLOCKS
LRU_CONV = 4
LRU_C = 8.0
LRU_SLAB = 128

SSD_INNER = 1024
SSD_HEAD_DIM = 64
SSD_HEADS = SSD_INNER // SSD_HEAD_DIM
SSD_GROUPS = 4
SSD_HPG = SSD_HEADS // SSD_GROUPS
SSD_STATE = 128
SSD_CONV = 4
SSD_CHUNK = 128
SSD_STEP_CHUNKS = 4
SSD_XBC = SSD_INNER + 2 * SSD_GROUPS * SSD_STATE
SSD_GW = SSD_HPG * SSD_HEAD_DIM

N_EXPERTS = 8
TOP_K = 2
LANE = 128

P_U, P_LX, P_LG, P_Z, P_XBC, P_GATE = 0, 1024, 2048, 3072, 4096, 6144
P_TOTAL = 9216

VMEM_LIMIT = 56 * 1024 * 1024


def _cparams(sem):
    return pltpu.CompilerParams(dimension_semantics=sem, vmem_limit_bytes=VMEM_LIMIT)


def _tile(n, pref):
    t = min(n, pref)
    while n % t:
        t //= 2
    return t


def _sigmoid(x):
    return jax.nn.sigmoid(x)


def _silu(x):
    return x * jax.nn.sigmoid(x)


def _softplus(x):
    return jnp.maximum(x, 0.0) + jnp.log(1.0 + jnp.exp(-jnp.abs(x)))


def _split_dot(x, w_bf16):
    hi = x.astype(BF16)
    lo = (x - hi.astype(F32)).astype(BF16)
    return (jnp.dot(hi, w_bf16, preferred_element_type=F32)
            + jnp.dot(lo, w_bf16, preferred_element_type=F32))


def _rms_mod(x, nw, shift, scale):
    y = x * lax.rsqrt(jnp.mean(x * x, axis=-1, keepdims=True) + EPS) * nw
    return y * (1.0 + scale) + shift


def _mod_kernel(c_ref, w_ref, b_ref, o_ref):
    cond = _silu(c_ref[...])
    o_ref[...] = jnp.dot(cond, w_ref[...], preferred_element_type=F32, precision=HIGHEST) + b_ref[...]


def _modulation(cond_in, w_mod, b_mod):
    depth = w_mod.shape[0]
    n = w_mod.shape[2]
    tn = 1024
    return pl.pallas_call(
        _mod_kernel,
        grid=(depth, n // tn),
        in_specs=[pl.BlockSpec((8, D_MODEL), lambda l, j: (0, 0)),
                  pl.BlockSpec((None, D_MODEL, tn), lambda l, j: (l, 0, j)),
                  pl.BlockSpec((None, 1, tn), lambda l, j: (l, 0, j))],
        out_specs=pl.BlockSpec((None, 8, tn), lambda l, j: (l, 0, j)),
        out_shape=jax.ShapeDtypeStruct((depth, 8, n), F32),
        compiler_params=_cparams(("arbitrary", "arbitrary")),
        name="modulation",
    )(cond_in, w_mod, b_mod.reshape(depth, 1, n))


def _inproj_kernel(x_ref, mod_ref, nw_ref, w_ref, wdt_ref, p_ref, dt_ref, h_scr):
    @pl.when(pl.program_id(2) == 0)
    def _():
        h = _rms_mod(x_ref[...], nw_ref[...], mod_ref[0:1, :], mod_ref[1:2, :]).astype(BF16)
        h_scr[...] = h
        dt_ref[...] = jnp.dot(h, wdt_ref[...], preferred_element_type=F32)

    p_ref[...] = jnp.dot(h_scr[...], w_ref[...], preferred_element_type=F32).astype(p_ref.dtype)


def _inproj(x, mod, nw, w, wdt):
    b, t, _ = x.shape
    tm = _tile(t, 1024)
    tn = 2304
    return pl.pallas_call(
        _inproj_kernel,
        grid=(b, t // tm, P_TOTAL // tn),
        in_specs=[pl.BlockSpec((None, tm, D_MODEL), lambda b_, i, j: (b_, i, 0)),
                  pl.BlockSpec((None, 8, D_MODEL), lambda b_, i, j: (b_, 0, 0)),
                  pl.BlockSpec((1, D_MODEL), lambda b_, i, j: (0, 0)),
                  pl.BlockSpec((D_MODEL, tn), lambda b_, i, j: (0, j)),
                  pl.BlockSpec((D_MODEL, LANE), lambda b_, i, j: (0, 0))],
        out_specs=[pl.BlockSpec((None, tm, tn), lambda b_, i, j: (b_, i, j)),
                   pl.BlockSpec((None, tm, LANE), lambda b_, i, j: (b_, i, 0))],
        out_shape=[jax.ShapeDtypeStruct((b, t, P_TOTAL), BF16),
                   jax.ShapeDtypeStruct((b, t, LANE), F32)],
        scratch_shapes=[pltpu.VMEM((tm, D_MODEL), BF16)],
        compiler_params=_cparams(("arbitrary", "arbitrary", "arbitrary")),
        name="inproj",
    )(x, mod, nw, w, wdt)


def _s5_setup(lam_re, lam_im, log_dt, b_re, b_im, c_re, c_im, n_levels):
    t = S5_CHUNK
    dt = jnp.exp(log_dt.astype(F32))[..., None]
    lre, lim = lam_re.astype(F32), lam_im.astype(F32)
    are, aim = lre * dt, lim * dt
    k = jnp.arange(t + 1, dtype=F32)[:, None]
    mag = jnp.exp(are[:, :, None, :] * k)
    pw_re = mag * jnp.cos(aim[:, :, None, :] * k)
    pw_im = mag * jnp.sin(aim[:, :, None, :] * k)
    a_re, a_im = pw_re[:, :, 1], pw_im[:, :, 1]
    den = lre * lre + lim * lim
    q_re = ((a_re - 1.0) * lre + a_im * lim) / den
    q_im = (a_im * lre - (a_re - 1.0) * lim) / den
    bb_re = q_re[..., None] * b_re - q_im[..., None] * b_im
    bb_im = q_re[..., None] * b_im + q_im[..., None] * b_re
    cr, ci = c_re.astype(F32), c_im.astype(F32)

    cb_re = cr[..., None] * bb_re[:, :, None] - ci[..., None] * bb_im[:, :, None]
    cb_im = cr[..., None] * bb_im[:, :, None] + ci[..., None] * bb_re[:, :, None]
    kern = (jnp.einsum('dgkp,dgjpi->dgkji', pw_re[:, :, :t], cb_re, precision=HIGHEST)
            - jnp.einsum('dgkp,dgjpi->dgkji', pw_im[:, :, :t], cb_im, precision=HIGHEST))
    kf, kb = kern[0], kern[1]
    taps = jnp.concatenate([kb[:, 1:][:, ::-1], (kf[:, 0] + kb[:, 0])[:, None], kf[:, 1:]], axis=1)
    kcat = taps.transpose(0, 3, 1, 2).reshape(S5_GROUPS, S5_GROUP, (2 * t - 1) * S5_GROUP)
    kcat = jnp.pad(kcat, ((0, 0), (0, 0), (0, S5_GROUP)))

    pf_re, pf_im = pw_re[0, :, :t][:, ::-1], pw_im[0, :, :t][:, ::-1]
    pb_re, pb_im = pw_re[1, :, :t], pw_im[1, :, :t]

    def in_map(p_re, p_im, d):
        w_re = p_re[:, :, None, :] * bb_re[d].transpose(0, 2, 1)[:, None] - p_im[:, :, None, :] * bb_im[d].transpose(0, 2, 1)[:, None]
        w_im = p_re[:, :, None, :] * bb_im[d].transpose(0, 2, 1)[:, None] + p_im[:, :, None, :] * bb_re[d].transpose(0, 2, 1)[:, None]
        return w_re, w_im

    wf_re, wf_im = in_map(pf_re, pf_im, 0)
    wb_re, wb_im = in_map(pb_re, pb_im, 1)
    pm = jnp.concatenate([wf_re, wf_im, wb_re, wb_im], axis=-1).reshape(S5_GROUPS, S5_CW, 4 * S5_STATE)

    def out_map(p_re, p_im, d):
        c_r = cr[d].transpose(0, 2, 1)[:, :, None, :]
        c_i = ci[d].transpose(0, 2, 1)[:, :, None, :]
        e_r = p_re.transpose(0, 2, 1)[..., None]
        e_i = p_im.transpose(0, 2, 1)[..., None]
        m_re = c_r * e_r - c_i * e_i
        m_im = c_r * e_i + c_i * e_r
        return m_re, -m_im

    rf_re, rf_im = out_map(pw_re[0, :, 1:t + 1], pw_im[0, :, 1:t + 1], 0)
    rb_re, rb_im = out_map(pw_re[1, :, 1:t + 1][:, ::-1], pw_im[1, :, 1:t + 1][:, ::-1], 1)
    rm = jnp.concatenate([rf_re, rf_im, rb_re, rb_im], axis=1).reshape(S5_GROUPS, 4 * S5_STATE, S5_CW)

    lev = (t * 2.0 ** jnp.arange(n_levels, dtype=F32))[:, None]
    lmag = jnp.exp(are[:, :, None, :] * lev)
    l_re = lmag * jnp.cos(aim[:, :, None, :] * lev)
    l_im = lmag * jnp.sin(aim[:, :, None, :] * lev)
    row_a = jnp.concatenate([l_re[0], l_re[0], l_re[1], l_re[1]], axis=-1)
    row_b = jnp.concatenate([-l_im[0], l_im[0], -l_im[1], l_im[1]], axis=-1)
    atab = jnp.stack([row_a, row_b], axis=2)
    return kcat, pm.astype(BF16), rm.astype(BF16), atab


def _shift_rows(h, d, down):
    n = h.shape[0]
    if d >= n:
        return jnp.zeros_like(h)
    row = lax.broadcasted_iota(jnp.int32, h.shape, 0)
    if down:
        return jnp.where(row >= d, pltpu.roll(h, d, 0), 0.0)
    return jnp.where(row < n - d, pltpu.roll(h, n - d, 0), 0.0)


S5_SLAB_GROUPS = LANE // S5_GROUP
S5_PIECES = S5_CW // LANE
S5_PER_PIECE = LANE // S5_GROUP


def _s5_kernel(u_ref, perm_ref, kc_ref, pm_ref, rm_ref, at_ref, h0_ref, y_ref, ht_ref,
               x_scr, yg_scr, tz_scr, hin_scr, *, nb, nc, n_levels):
    half = 2 * S5_STATE
    row = lax.broadcasted_iota(jnp.int32, (nc, half), 0)

    for v in range(S5_PIECES):
        a = jnp.concatenate(
            [jnp.concatenate([u_ref[b, S5_PER_PIECE * v + sl] for b in range(nb)], axis=0)
             for sl in range(S5_PER_PIECE)], axis=1)
        xp = jnp.dot(a.astype(BF16), perm_ref[...], preferred_element_type=F32).astype(BF16)
        for gl in range(S5_SLAB_GROUPS):
            x_scr[gl, :, v * LANE:(v + 1) * LANE] = xp[:, gl * LANE:(gl + 1) * LANE]

    def group(g, _):
        kc = kc_ref[g]
        for s in range(S5_CHUNK):
            off = (S5_CHUNK - 1 - s) * S5_GROUP
            tz_scr[s * S5_GROUP:(s + 1) * S5_GROUP, :] = kc[:, off:off + S5_CW].astype(BF16)
        x = x_scr[g]
        s_all = jnp.dot(x, pm_ref[g], preferred_element_type=F32)

        def cmul(h, k, lo):
            a = at_ref[g, k, 0:1, lo:lo + half]
            b = at_ref[g, k, 1:2, lo:lo + half]
            return a * h + b * pltpu.roll(h, S5_STATE, 1)

        for b in range(nb):
            for d in range(2):
                lo = d * half
                h = s_all[b * nc:(b + 1) * nc, lo:lo + half]
                h0 = h0_ref[g, b:b + 1, lo:lo + half]
                edge = 0 if d == 0 else nc - 1
                h = h + jnp.where(row == edge, cmul(jnp.broadcast_to(h0, (nc, half)), 0, lo), 0.0)
                for k in range(n_levels):
                    if (1 << k) < nc:
                        h = h + cmul(_shift_rows(h, 1 << k, d == 0), k, lo)
                ht_ref[g, b:b + 1, lo:lo + half] = h[nc - 1 - edge:nc - edge]
                hin = _shift_rows(h, 1, d == 0)
                hin = jnp.where(row == edge, jnp.broadcast_to(h0, (nc, half)), hin)
                hin_scr[b * nc:(b + 1) * nc, lo:lo + half] = hin

        y = jnp.dot(x, tz_scr[...], preferred_element_type=F32)
        y = y + jnp.dot(hin_scr[...].astype(BF16), rm_ref[g], preferred_element_type=F32)
        yg_scr[g] = y.astype(BF16)
        return 0
    lax.fori_loop(0, S5_SLAB_GROUPS, group, 0)

    for v in range(S5_PIECES):
        cat = jnp.concatenate([yg_scr[gl, :, v * LANE:(v + 1) * LANE] for gl in range(S5_SLAB_GROUPS)], axis=1)
        yp = jnp.dot(cat, perm_ref[...], preferred_element_type=F32)
        for tl in range(S5_PER_PIECE):
            for b in range(nb):
                y_ref[b, S5_PER_PIECE * v + tl] = yp[b * nc:(b + 1) * nc, tl * LANE:(tl + 1) * LANE].astype(y_ref.dtype)


def _s5_scan(u, perm, kcat, pm, rm, atab, h0, layer):
    nb, _, nc, _ = u.shape
    rows = nb * nc
    n_levels = atab.shape[2]
    gs = S5_SLAB_GROUPS
    kern = functools.partial(_s5_kernel, nb=nb, nc=nc, n_levels=n_levels)
    tok_spec = pl.BlockSpec((nb, S5_CHUNK, nc, LANE), lambda i: (0, 0, 0, i))
    return pl.pallas_call(
        kern,
        grid=(S5_GROUPS // gs,),
        in_specs=[tok_spec,
                  pl.BlockSpec((gs * LANE, gs * LANE), lambda i: (0, 0)),
                  pl.BlockSpec((None, gs, S5_GROUP, 2 * S5_CW), lambda i: (layer, i, 0, 0)),
                  pl.BlockSpec((None, gs, S5_CW, 4 * S5_STATE), lambda i: (layer, i, 0, 0)),
                  pl.BlockSpec((None, gs, 4 * S5_STATE, S5_CW), lambda i: (layer, i, 0, 0)),
                  pl.BlockSpec((None, gs, n_levels, 2, 4 * S5_STATE), lambda i: (layer, i, 0, 0, 0)),
                  pl.BlockSpec((gs, nb, 4 * S5_STATE), lambda i: (i, 0, 0))],
        out_specs=[tok_spec,
                   pl.BlockSpec((gs, nb, 4 * S5_STATE), lambda i: (i, 0, 0))],
        out_shape=[jax.ShapeDtypeStruct(u.shape, u.dtype),
                   jax.ShapeDtypeStruct((S5_GROUPS, nb, 4 * S5_STATE), F32)],
        scratch_shapes=[pltpu.VMEM((gs, rows, S5_CW), BF16), pltpu.VMEM((gs, rows, S5_CW), BF16),
                        pltpu.VMEM((S5_CW, S5_CW), BF16), pltpu.VMEM((rows, 4 * S5_STATE), F32)],
        compiler_params=_cparams(("arbitrary",)),
        name="s5_scan",
    )(u, perm, kcat, pm, rm, atab, h0)


def _s5_chunks(p, nb, t):
    nc = t // S5_CHUNK
    u = p[:, :, P_U:P_U + S5_WIDTH].reshape(nb, nc, S5_CHUNK, S5_WIDTH).transpose(0, 2, 1, 3)
    return u if nc % 16 == 0 else u.astype(F32)


def _s5_unchunk(y, nb, t):
    return y.transpose(0, 2, 1, 3).reshape(nb, t, S5_WIDTH)


def _s5_perm():
    a = jnp.arange(S5_SLAB_GROUPS * LANE)
    s, g, i = a // LANE, (a % LANE) // S5_GROUP, a % S5_GROUP
    dst = g * LANE + s * S5_GROUP + i
    return (dst[:, None] == a[None, :]).astype(BF16)


def _lru_kernel(x_ref, cw_ref, cb_ref, wg_ref, bg_ref, lam_ref, h0_ref, out_ref, ht_ref,
                xp_scr, xc_scr, af_scr, bf_scr, ab_scr, bb_scr, cf_scr, cr_scr, *, w, s, pad_top, ch, ns):
    l = w * s
    n_ch = l // ch
    slabs = range(ns)
    lanes = [slice(k * LRU_SLAB, (k + 1) * LRU_SLAB) for k in slabs]
    zero_slab = jnp.zeros((w, LRU_SLAB), F32)
    one_slab = jnp.ones((w, LRU_SLAB), F32)

    for k in slabs:
        xp_scr[k, 0:pad_top, :] = jnp.zeros((pad_top, LRU_SLAB), F32)
        xp_scr[k, pad_top + l:pad_top + l + pad_top, :] = jnp.zeros((pad_top, LRU_SLAB), F32)

    def copy_body(i, _):
        r0 = pl.multiple_of(i * ch, ch)
        for k in slabs:
            xp_scr[k, pl.ds(pad_top + r0, ch), :] = x_ref[pl.ds(r0, ch), lanes[k]].astype(F32)
        return 0
    lax.fori_loop(0, n_ch, copy_body, 0)

    def conv_body(i, _):
        r0 = pl.multiple_of(i * ch, ch)
        for k in slabs:
            acc = cb_ref[:, lanes[k]] + cw_ref[2:3, lanes[k]] * xp_scr[k, pl.ds(pad_top + r0, ch), :]
            acc = acc + cw_ref[0:1, lanes[k]] * xp_scr[k, pl.ds(pad_top + r0 - 2 * w, ch), :]
            acc = acc + cw_ref[1:2, lanes[k]] * xp_scr[k, pl.ds(pad_top + r0 - w, ch), :]
            acc = acc + cw_ref[3:4, lanes[k]] * xp_scr[k, pl.ds(pad_top + r0 + w, ch), :]
            xc_scr[k, pl.ds(r0, ch), :] = acc
        return 0
    lax.fori_loop(0, n_ch, conv_body, 0)

    if w > 1:
        def prev_col(v):
            return _shift_rows(v, 1, True)

        def next_col(v):
            return _shift_rows(v, 1, False)

        for k in slabs:
            x_last = xp_scr[k, pad_top + (s - 1) * w:pad_top + s * w, :]
            x_last2 = xp_scr[k, pad_top + (s - 2) * w:pad_top + (s - 1) * w, :]
            x_first = xp_scr[k, pad_top:pad_top + w, :]
            w0, w1, w3 = cw_ref[0:1, lanes[k]], cw_ref[1:2, lanes[k]], cw_ref[3:4, lanes[k]]
            xc_scr[k, 0:w, :] = xc_scr[k, 0:w, :] + w0 * prev_col(x_last2) + w1 * prev_col(x_last)
            xc_scr[k, w:2 * w, :] = xc_scr[k, w:2 * w, :] + w0 * prev_col(x_last)
            xc_scr[k, (s - 1) * w:s * w, :] = xc_scr[k, (s - 1) * w:s * w, :] + w3 * next_col(x_first)

    c_half = (0.5 * LRU_C) * -_softplus(-lam_ref[...])

    def coef_body(i, _):
        r0 = pl.multiple_of(i * ch, ch)
        for k in slabs:
            xc = xc_scr[k, pl.ds(r0, ch), :]
            hx = 0.5 * xc
            g = jnp.dot(xc.astype(BF16), wg_ref[k], preferred_element_type=F32) + bg_ref[k]
            for d, (a_scr, b_scr) in enumerate(((af_scr, bf_scr), (ab_scr, bb_scr))):
                t_r = jnp.tanh(g[:, (2 * d) * LRU_SLAB:(2 * d + 1) * LRU_SLAB])
                t_i = jnp.tanh(g[:, (2 * d + 1) * LRU_SLAB:(2 * d + 2) * LRU_SLAB])
                c = c_half[d:d + 1, lanes[k]]
                a = jnp.exp(c * t_r + c)
                a_scr[k, pl.ds(r0, ch), :] = a
                b_scr[k, pl.ds(r0, ch), :] = jnp.sqrt(1.0 - a * a) * (hx * (t_i + 1.0))
        return 0
    lax.fori_loop(0, n_ch, coef_body, 0)

    def scan_body(r, carry):
        rf = pl.multiple_of(r * w, w)
        rb = pl.multiple_of((s - 1 - r) * w, w)
        out = []
        for k in slabs:
            hf, pf, hb, pb = carry[k]
            a = af_scr[k, pl.ds(rf, w), :]
            hf = a * hf + bf_scr[k, pl.ds(rf, w), :]
            pf = a * pf
            bf_scr[k, pl.ds(rf, w), :] = hf
            af_scr[k, pl.ds(rf, w), :] = pf
            a = ab_scr[k, pl.ds(rb, w), :]
            hb = a * hb + bb_scr[k, pl.ds(rb, w), :]
            pb = a * pb
            bb_scr[k, pl.ds(rb, w), :] = hb
            ab_scr[k, pl.ds(rb, w), :] = pb
            out.append((hf, pf, hb, pb))
        return tuple(out)
    lax.fori_loop(0, s, scan_body, tuple((zero_slab, one_slab, zero_slab, one_slab) for _ in slabs))

    def carry_f(c, carry):
        out = []
        for k in slabs:
            cf_scr[k, pl.ds(c, 1), :] = carry[k]
            out.append(bf_scr[k, pl.ds((s - 1) * w + c, 1), :] + af_scr[k, pl.ds((s - 1) * w + c, 1), :] * carry[k])
        return tuple(out)
    fin = lax.fori_loop(0, w, carry_f, tuple(h0_ref[0:1, lanes[k]] for k in slabs))
    for k in slabs:
        ht_ref[0:1, lanes[k]] = fin[k]

    def carry_b(i, carry):
        c = w - 1 - i
        out = []
        for k in slabs:
            cr_scr[k, pl.ds(c, 1), :] = carry[k]
            out.append(bb_scr[k, pl.ds(c, 1), :] + ab_scr[k, pl.ds(c, 1), :] * carry[k])
        return tuple(out)
    fin = lax.fori_loop(0, w, carry_b, tuple(h0_ref[1:2, lanes[k]] for k in slabs))
    for k in slabs:
        ht_ref[1:2, lanes[k]] = fin[k]

    def out_body(r, _):
        r0 = pl.multiple_of(r * w, w)
        for k in slabs:
            y = (bf_scr[k, pl.ds(r0, w), :] + af_scr[k, pl.ds(r0, w), :] * cf_scr[k]
                 + bb_scr[k, pl.ds(r0, w), :] + ab_scr[k, pl.ds(r0, w), :] * cr_scr[k])
            bf_scr[k, pl.ds(r0, w), :] = y
        return 0
    lax.fori_loop(0, s, out_body, 0)

    def store_body(i, _):
        r0 = pl.multiple_of(i * ch, ch)
        for k in slabs:
            out_ref[pl.ds(r0, ch), lanes[k]] = bf_scr[k, pl.ds(r0, ch), :].astype(out_ref.dtype)
        return 0
    lax.fori_loop(0, n_ch, store_body, 0)


def _lru_scan(p, cw, cb, wg, bg, lam, h0, w, ns, layer):
    b, t, _ = p.shape
    s = t // w
    assert s * w == t and s >= 4
    ch = _tile(t, 256)
    pad_top = max(2 * w, 8)
    cwid = ns * LRU_SLAB
    col0 = P_LX // cwid
    kern = functools.partial(_lru_kernel, w=w, s=s, pad_top=pad_top, ch=ch, ns=ns)
    return pl.pallas_call(
        kern,
        grid=(b, LRU_WIDTH // cwid),
        in_specs=[pl.BlockSpec((None, t, cwid), lambda b_, k: (b_, 0, col0 + k)),
                  pl.BlockSpec((LRU_CONV, cwid), lambda b_, k: (0, k)),
                  pl.BlockSpec((1, cwid), lambda b_, k: (0, k)),
                  pl.BlockSpec((None, ns, LRU_SLAB, 4 * LRU_SLAB), lambda b_, k: (layer, k, 0, 0)),
                  pl.BlockSpec((None, ns, 1, 4 * LRU_SLAB), lambda b_, k: (layer, k, 0, 0)),
                  pl.BlockSpec((2, cwid), lambda b_, k: (0, k)),
                  pl.BlockSpec((None, 2, cwid), lambda b_, k: (b_, 0, k))],
        out_specs=[pl.BlockSpec((None, t, cwid), lambda b_, k: (b_, 0, k)),
                   pl.BlockSpec((None, 2, cwid), lambda b_, k: (b_, 0, k))],
        out_shape=[jax.ShapeDtypeStruct((b, t, LRU_WIDTH), BF16),
                   jax.ShapeDtypeStruct((b, 2, LRU_WIDTH), F32)],
        scratch_shapes=[pltpu.VMEM((ns, t + 2 * pad_top, LRU_SLAB), F32),
                        pltpu.VMEM((ns, t, LRU_SLAB), F32),
                        pltpu.VMEM((ns, t, LRU_SLAB), F32), pltpu.VMEM((ns, t, LRU_SLAB), F32),
                        pltpu.VMEM((ns, t, LRU_SLAB), F32), pltpu.VMEM((ns, t, LRU_SLAB), F32),
                        pltpu.VMEM((ns, w, LRU_SLAB), F32), pltpu.VMEM((ns, w, LRU_SLAB), F32)],
        compiler_params=_cparams(("arbitrary", "arbitrary")),
        name="lru_scan",
    )(p, cw, cb, wg, bg, lam, h0)


def _lru_gate_weights(w_a, b_a, w_x, b_x):
    n_slab = LRU_WIDTH // LRU_SLAB
    per = LRU_SLAB // LRU_BLOCK

    def slab_diag(wm):
        wm = wm.reshape(n_slab, per, LRU_BLOCK, LRU_BLOCK)
        eye = jnp.eye(per, dtype=wm.dtype)
        return jnp.einsum('spkj,pq->spkqj', wm, eye).reshape(n_slab, LRU_SLAB, LRU_SLAB)

    wg = jnp.concatenate([slab_diag(w_a[0]), slab_diag(w_x[0]), slab_diag(w_a[1]), slab_diag(w_x[1])], axis=2)
    bg = jnp.concatenate([b_a[0].reshape(n_slab, 1, LRU_SLAB), b_x[0].reshape(n_slab, 1, LRU_SLAB),
                          b_a[1].reshape(n_slab, 1, LRU_SLAB), b_x[1].reshape(n_slab, 1, LRU_SLAB)], axis=2)
    return (0.5 * wg).astype(BF16), (0.5 * bg).astype(F32)


def _ssd_core(xs, bm, cm, dtr, alog, dbias, e_ref, h_scr, y_scr, reverse):
    q = SSD_CHUNK
    a = -jnp.exp(alog)
    dt = _softplus(dtr + dbias)
    adt = dt * a
    row = lax.broadcasted_iota(jnp.int32, (q, q), 0)
    col = lax.broadcasted_iota(jnp.int32, (q, q), 1)
    tri = (col >= row) if reverse else (col <= row)
    cs = jnp.dot(tri.astype(F32), adt, preferred_element_type=F32, precision=HIGHEST)
    cs_t = cs.T
    dt_t = dt.T
    end = 0 if reverse else q - 1
    tot = cs[end:end + 1, :]
    e = e_ref[...]
    w_x = _split_dot(dt * jnp.exp(tot - cs), e)
    ecs_x = _split_dot(jnp.exp(cs), e)
    xsb = xs.astype(BF16)
    xw = (xs * w_x).astype(BF16)
    decay_row = ecs_x[end:end + 1, :]

    for g in range(SSD_GROUPS):
        cg = cm[:, g * SSD_STATE:(g + 1) * SSD_STATE]
        bg = bm[:, g * SSD_STATE:(g + 1) * SSD_STATE]
        sc = lax.dot_general(cg, bg, (((1,), (1,)), ((), ())), preferred_element_type=F32)
        for j in range(SSD_HPG):
            hd = g * SSD_HPG + j
            cols = slice(hd * SSD_HEAD_DIM, (hd + 1) * SSD_HEAD_DIM)
            diff = cs[:, hd:hd + 1] - cs_t[hd:hd + 1, :]
            lm = jnp.where(tri, jnp.exp(diff), 0.0) * dt_t[hd:hd + 1, :]
            y_scr[:, cols] = jnp.dot((sc * lm).astype(BF16), xsb[:, cols], preferred_element_type=F32)
        ch = slice(g * SSD_GW, (g + 1) * SSD_GW)
        rows = slice(g * SSD_STATE, (g + 1) * SSD_STATE)
        h_old = h_scr[rows, :]
        y_off = jnp.dot(cg, h_old.astype(BF16), preferred_element_type=F32) * ecs_x[:, ch]
        y_scr[:, ch] = y_scr[:, ch] + y_off
        st = lax.dot_general(bg, xw[:, ch], (((0,), (0,)), ((), ())), preferred_element_type=F32)
        h_scr[rows, :] = decay_row[:, ch] * h_old + st
    return y_scr[...]


def _ssd_fwd_kernel(x_ref, xprev_ref, xnext_ref, dt_ref, cw_ref, cb_ref, alog_ref, dbias_ref, e_ref, h0_ref,
                    y_ref, xc_ref, ht_ref, h_scr, y_scr, xcf_scr, *, nc):
    c = pl.program_id(1)

    @pl.when(c == 0)
    def _():
        h_scr[...] = h0_ref[...]

    q = x_ref.shape[0]
    x = x_ref[...].astype(F32)
    prev = jnp.where(c > 0, xprev_ref[...].astype(F32), 0.0)
    nxt = jnp.where(c < nc - 1, xnext_ref[...].astype(F32), 0.0)
    row8 = lax.broadcasted_iota(jnp.int32, (8, SSD_XBC), 0)

    def behind(k):
        xr = pltpu.roll(x, k, 0)
        top = jnp.where(row8 < k, pltpu.roll(prev, k, 0)[0:8], xr[0:8])
        return jnp.concatenate([top, xr[8:]], axis=0)

    xm2, xm1 = behind(2), behind(1)
    xr = pltpu.roll(x, q - 1, 0)
    bottom = jnp.where(row8 == 7, pltpu.roll(nxt, 15, 0)[8:16], xr[q - 8:q])
    xp1 = jnp.concatenate([xr[:q - 8], bottom], axis=0)
    conv = (cb_ref[...] + cw_ref[0:1, :] * xm2 + cw_ref[1:2, :] * xm1
            + cw_ref[2:3, :] * x + cw_ref[3:4, :] * xp1)
    xc = _silu(conv)
    xc_ref[...] = xc.astype(xc_ref.dtype)
    xcf_scr[...] = xc
    for k in range(q // SSD_CHUNK):
        r = slice(k * SSD_CHUNK, (k + 1) * SSD_CHUNK)
        xs = xcf_scr[r, :SSD_INNER]
        bm = xcf_scr[r, SSD_INNER:SSD_INNER + SSD_GROUPS * SSD_STATE].astype(BF16)
        cm = xcf_scr[r, SSD_INNER + SSD_GROUPS * SSD_STATE:].astype(BF16)
        y_ref[r, :] = _ssd_core(xs, bm, cm, dt_ref[r, :], alog_ref[...], dbias_ref[...], e_ref, h_scr, y_scr, False)

    @pl.when(c == nc - 1)
    def _():
        ht_ref[...] = h_scr[...]


def _ssd_bwd_kernel(xc_ref, dt_ref, yf_ref, alog_ref, dbias_ref, dskip_ref, e_ref, h0_ref,
                    y_ref, ht_ref, h_scr, y_scr, *, nc):
    c = pl.program_id(1)

    @pl.when(c == 0)
    def _():
        h_scr[...] = h0_ref[...]

    for k in reversed(range(xc_ref.shape[0] // SSD_CHUNK)):
        r = slice(k * SSD_CHUNK, (k + 1) * SSD_CHUNK)
        xs = xc_ref[r, :SSD_INNER].astype(F32)
        bm = xc_ref[r, SSD_INNER:SSD_INNER + SSD_GROUPS * SSD_STATE]
        cm = xc_ref[r, SSD_INNER + SSD_GROUPS * SSD_STATE:]
        y = _ssd_core(xs, bm, cm, dt_ref[r, :], alog_ref[...], dbias_ref[...], e_ref, h_scr, y_scr, True)
        y_ref[r, :] = y + yf_ref[r, :] + dskip_ref[...] * xs

    @pl.when(c == nc - 1)
    def _():
        ht_ref[...] = h_scr[...]


def _ssd_scan(p, dt, cw, cb, a_log, dt_bias, dskip_x, e_mat, h0f, h0b):
    b, t, _ = p.shape
    q = min(SSD_STEP_CHUNKS, t // SSD_CHUNK) * SSD_CHUNK
    nc = t // q
    assert nc * q == t
    xbc_blk = P_XBC // SSD_XBC
    n16 = t // 16
    hs = (SSD_GROUPS * SSD_STATE, SSD_GW)
    const2 = lambda b_, c: (0, 0)
    state_spec = pl.BlockSpec((None,) + hs, lambda b_, c: (b_, 0, 0))

    yf, xc, htf = pl.pallas_call(
        functools.partial(_ssd_fwd_kernel, nc=nc),
        grid=(b, nc),
        in_specs=[pl.BlockSpec((None, q, SSD_XBC), lambda b_, c: (b_, c, xbc_blk)),
                  pl.BlockSpec((None, 16, SSD_XBC),
                               lambda b_, c: (b_, jnp.maximum(c * (q // 16) - 1, 0), xbc_blk)),
                  pl.BlockSpec((None, 16, SSD_XBC),
                               lambda b_, c: (b_, jnp.minimum((c + 1) * (q // 16), n16 - 1), xbc_blk)),
                  pl.BlockSpec((None, q, LANE), lambda b_, c: (b_, c, 0)),
                  pl.BlockSpec((SSD_CONV, SSD_XBC), const2),
                  pl.BlockSpec((1, SSD_XBC), const2),
                  pl.BlockSpec((1, LANE), const2),
                  pl.BlockSpec((1, LANE), const2),
                  pl.BlockSpec((LANE, SSD_INNER), const2),
                  state_spec],
        out_specs=[pl.BlockSpec((None, q, SSD_INNER), lambda b_, c: (b_, c, 0)),
                   pl.BlockSpec((None, q, SSD_XBC), lambda b_, c: (b_, c, 0)),
                   state_spec],
        out_shape=[jax.ShapeDtypeStruct((b, t, SSD_INNER), F32),
                   jax.ShapeDtypeStruct((b, t, SSD_XBC), BF16),
                   jax.ShapeDtypeStruct((b,) + hs, F32)],
        scratch_shapes=[pltpu.VMEM(hs, F32), pltpu.VMEM((SSD_CHUNK, SSD_INNER), F32),
                        pltpu.VMEM((q, SSD_XBC), F32)],
        compiler_params=_cparams(("arbitrary", "arbitrary")),
        name="ssd_fwd",
    )(p, p, p, dt, cw, cb, a_log[0:1], dt_bias[0:1], e_mat, h0f)

    rev = lambda b_, c: (b_, nc - 1 - c, 0)
    y, htb = pl.pallas_call(
        functools.partial(_ssd_bwd_kernel, nc=nc),
        grid=(b, nc),
        in_specs=[pl.BlockSpec((None, q, SSD_XBC), rev),
                  pl.BlockSpec((None, q, LANE), rev),
                  pl.BlockSpec((None, q, SSD_INNER), rev),
                  pl.BlockSpec((1, LANE), const2),
                  pl.BlockSpec((1, LANE), const2),
                  pl.BlockSpec((1, SSD_INNER), const2),
                  pl.BlockSpec((LANE, SSD_INNER), const2),
                  state_spec],
        out_specs=[pl.BlockSpec((None, q, SSD_INNER), rev), state_spec],
        out_shape=[jax.ShapeDtypeStruct((b, t, SSD_INNER), F32),
                   jax.ShapeDtypeStruct((b,) + hs, F32)],
        scratch_shapes=[pltpu.VMEM(hs, F32), pltpu.VMEM((SSD_CHUNK, SSD_INNER), F32)],
        compiler_params=_cparams(("arbitrary", "arbitrary")),
        name="ssd_bwd",
    )(xc, dt, yf, a_log[1:2], dt_bias[1:2], dskip_x, e_mat, h0b)
    return y, htf, htb


def _merge_kernel(x_ref, mod_ref, u_ref, ya_ref, yb_ref, g_ref, yc_ref, z_ref, gates_ref,
                  d5_ref, wglu_ref, snw_ref, wa_ref, wb_ref, wc_ref, wo_ref, xo_ref):
    ya = jax.nn.gelu(ya_ref[...].astype(F32) + d5_ref[...] * u_ref[...].astype(F32))
    ya = ya * _sigmoid(jnp.dot(ya.astype(BF16), wglu_ref[...], preferred_element_type=F32))
    yb = yb_ref[...].astype(F32) * jax.nn.gelu(g_ref[...].astype(F32))
    yc = yc_ref[...] * _silu(z_ref[...].astype(F32))
    yc = yc * lax.rsqrt(jnp.mean(yc * yc, axis=-1, keepdims=True) + EPS) * snw_ref[...]
    mix = _sigmoid(gates_ref[:, 0:D_MODEL].astype(F32)) * jnp.dot(
        ya.astype(BF16), wa_ref[...], preferred_element_type=F32)
    mix = mix + _sigmoid(gates_ref[:, D_MODEL:2 * D_MODEL].astype(F32)) * jnp.dot(
        yb.astype(BF16), wb_ref[...], preferred_element_type=F32)
    mix = mix + _sigmoid(gates_ref[:, 2 * D_MODEL:3 * D_MODEL].astype(F32)) * jnp.dot(
        yc.astype(BF16), wc_ref[...], preferred_element_type=F32)
    y = jnp.dot(mix.astype(BF16), wo_ref[...], preferred_element_type=F32)
    xo_ref[...] = x_ref[...] + mod_ref[2:3, :] * y


def _merge(x, mod, p, ya, yb, yc, d5, wglu, snw, wa, wb, wc, wo):
    b, t, _ = x.shape
    tm = _tile(t, 512)
    tok = lambda width, blk: pl.BlockSpec((None, tm, width), lambda b_, i: (b_, i, blk))
    full = lambda r, c_: pl.BlockSpec((r, c_), lambda b_, i: (0, 0))
    return pl.pallas_call(
        _merge_kernel,
        grid=(b, t // tm),
        in_specs=[tok(D_MODEL, 0),
                  pl.BlockSpec((None, 8, D_MODEL), lambda b_, i: (b_, 0, 0)),
                  tok(S5_WIDTH, P_U // S5_WIDTH),
                  tok(S5_WIDTH, 0),
                  tok(LRU_WIDTH, 0),
                  tok(LRU_WIDTH, P_LG // LRU_WIDTH),
                  tok(SSD_INNER, 0),
                  tok(SSD_INNER, P_Z // SSD_INNER),
                  tok(3 * D_MODEL, P_GATE // (3 * D_MODEL)),
                  full(1, S5_WIDTH), full(S5_WIDTH, S5_WIDTH), full(1, SSD_INNER),
                  full(S5_WIDTH, D_MODEL), full(LRU_WIDTH, D_MODEL), full(SSD_INNER, D_MODEL),
                  full(D_MODEL, D_MODEL)],
        out_specs=tok(D_MODEL, 0),
        out_shape=jax.ShapeDtypeStruct((b, t, D_MODEL), F32),
        compiler_params=_cparams(("arbitrary", "arbitrary")),
        name="merge",
    )(x, mod, p, ya, yb, p, yc, p, p, d5, wglu, snw, wa, wb, wc, wo)


def _ffn_kernel(x_ref, mod_ref, nw_ref, w1_ref, w3_ref, w2_ref, fnw_ref, xo_ref, h_scr, acc_scr, *, nf, final_norm):
    f = pl.program_id(2)

    @pl.when(f == 0)
    def _():
        h_scr[...] = _rms_mod(x_ref[...], nw_ref[...], mod_ref[3:4, :], mod_ref[4:5, :]).astype(BF16)
        acc_scr[...] = jnp.zeros_like(acc_scr)

    h = h_scr[...]
    a = jnp.dot(h, w1_ref[...], preferred_element_type=F32)
    g = jnp.dot(h, w3_ref[...], preferred_element_type=F32)
    acc_scr[...] += jnp.dot((_silu(a) * g).astype(BF16), w2_ref[...], preferred_element_type=F32)

    @pl.when(f == nf - 1)
    def _():
        xo = x_ref[...] + mod_ref[5:6, :] * acc_scr[...]
        if final_norm:
            xo = xo * lax.rsqrt(jnp.mean(xo * xo, axis=-1, keepdims=True) + EPS) * fnw_ref[...]
        xo_ref[...] = xo


def _ffn(x, mod, nw, w1, w3, w2, fnw, final_norm):
    b, t, _ = x.shape
    dff = w1.shape[1]
    tm = _tile(t, 512)
    tf = dff // 2
    nf = dff // tf
    return pl.pallas_call(
        functools.partial(_ffn_kernel, nf=nf, final_norm=final_norm),
        grid=(b, t // tm, nf),
        in_specs=[pl.BlockSpec((None, tm, D_MODEL), lambda b_, i, f: (b_, i, 0)),
                  pl.BlockSpec((None, 8, D_MODEL), lambda b_, i, f: (b_, 0, 0)),
                  pl.BlockSpec((1, D_MODEL), lambda b_, i, f: (0, 0)),
                  pl.BlockSpec((D_MODEL, tf), lambda b_, i, f: (0, f)),
                  pl.BlockSpec((D_MODEL, tf), lambda b_, i, f: (0, f)),
                  pl.BlockSpec((tf, D_MODEL), lambda b_, i, f: (f, 0)),
                  pl.BlockSpec((1, D_MODEL), lambda b_, i, f: (0, 0))],
        out_specs=pl.BlockSpec((None, tm, D_MODEL), lambda b_, i, f: (b_, i, 0)),
        out_shape=jax.ShapeDtypeStruct((b, t, D_MODEL), F32),
        scratch_shapes=[pltpu.VMEM((tm, D_MODEL), BF16), pltpu.VMEM((tm, D_MODEL), F32)],
        compiler_params=_cparams(("arbitrary", "arbitrary", "arbitrary")),
        name="ffn",
    )(x, mod, nw, w1, w3, w2, fnw)


MOE_BLK = 512
MOE_PER_EBLK = 2
SEL_LANE = 8


def _router_kernel(x_ref, mod_ref, nw_ref, wr_ref, br_ref, h_ref, g_ref):
    h = _rms_mod(x_ref[...], nw_ref[...], mod_ref[3:4, :], mod_ref[4:5, :])
    h_ref[...] = h.astype(BF16)
    logits = jnp.dot(h, wr_ref[...], preferred_element_type=F32, precision=HIGHEST) + br_ref[...]
    lane = lax.broadcasted_iota(jnp.int32, logits.shape, 1)
    logits = jnp.where(lane < N_EXPERTS, logits, -jnp.inf)
    t1 = jnp.max(logits, axis=-1, keepdims=True)
    i1 = jnp.min(jnp.where(logits == t1, lane, LANE), axis=-1, keepdims=True)
    rest = jnp.where(lane == i1, -jnp.inf, logits)
    t2 = jnp.max(rest, axis=-1, keepdims=True)
    i2 = jnp.min(jnp.where(rest == t2, lane, LANE), axis=-1, keepdims=True)
    e2 = jnp.exp(t2 - t1)
    den = 1.0 + e2
    gate = jnp.where(lane == i1, 1.0 / den, 0.0) + jnp.where(lane == i2, e2 / den, 0.0)
    flag = jnp.where((lane == i1 + SEL_LANE) | (lane == i2 + SEL_LANE), 1.0, 0.0)
    g_ref[...] = gate + flag


def _router(x, mod, nw, wr, br):
    b, t, _ = x.shape
    tm = _tile(t, 512)
    return pl.pallas_call(
        _router_kernel,
        grid=(b, t // tm),
        in_specs=[pl.BlockSpec((None, tm, D_MODEL), lambda b_, i: (b_, i, 0)),
                  pl.BlockSpec((None, 8, D_MODEL), lambda b_, i: (b_, 0, 0)),
                  pl.BlockSpec((1, D_MODEL), lambda b_, i: (0, 0)),
                  pl.BlockSpec((D_MODEL, LANE), lambda b_, i: (0, 0)),
                  pl.BlockSpec((1, LANE), lambda b_, i: (0, 0))],
        out_specs=[pl.BlockSpec((None, tm, D_MODEL), lambda b_, i: (b_, i, 0)),
                   pl.BlockSpec((None, tm, LANE), lambda b_, i: (b_, i, 0))],
        out_shape=[jax.ShapeDtypeStruct((b, t, D_MODEL), BF16),
                   jax.ShapeDtypeStruct((b, t, LANE), F32)],
        compiler_params=_cparams(("arbitrary", "arbitrary")),
        name="router",
    )(x, mod, nw, wr, br)


def _moe_plan(g, ts):
    n = g.shape[0]
    blk = MOE_BLK
    eblk = MOE_PER_EBLK * MOE_BLK
    nt = n // ts
    n_eblocks = TOP_K * n // eblk + N_EXPERTS
    n_blocks = n_eblocks * MOE_PER_EBLK
    max_pairs = n_blocks + N_EXPERTS * nt
    sel = g[:, SEL_LANE:SEL_LANE + N_EXPERTS] > 0.5
    cum = jnp.cumsum(sel.astype(jnp.int32), axis=0)
    cnt = cum[-1]
    neb = (cnt + eblk - 1) // eblk
    eb_end = jnp.cumsum(neb)
    eb_beg = eb_end - neb
    seg_off = eb_beg * eblk
    n_used = eb_end[-1]
    pos = jnp.where(sel, seg_off[None, :] + cum - 1, -1).astype(jnp.int32)
    eb = jnp.arange(n_eblocks)
    bexp = jnp.minimum(jnp.sum(eb_end[None, :] <= eb[:, None], axis=1), N_EXPERTS - 1)

    tile_end = cum[ts - 1::ts]
    tile_beg = jnp.concatenate([jnp.zeros((1, N_EXPERTS), jnp.int32), tile_end[:-1]], axis=0)
    d_lo = (seg_off[None, :] + tile_beg) // blk
    d_hi = (seg_off[None, :] + tile_end - 1) // blk
    npairs = jnp.where(tile_end > tile_beg, d_hi - d_lo + 1, 0)

    def expand(cnt_flat, cell_j, cell_e):
        cend = jnp.cumsum(cnt_flat)
        total = cend[-1]
        k = jnp.minimum(jnp.arange(max_pairs), total - 1)
        cell = jnp.sum(cend[None, :] <= k[:, None], axis=1)
        i = k - (cend[cell] - cnt_flat[cell])
        j, e = cell_j[cell], cell_e[cell]
        d = d_lo[j, e] + i
        valid = (jnp.arange(max_pairs) < total).astype(jnp.int32)
        return j.astype(jnp.int32), d.astype(jnp.int32), e.astype(jnp.int32), valid

    jj, ee = jnp.meshgrid(jnp.arange(nt), jnp.arange(N_EXPERTS), indexing='ij')
    by_tile = expand(npairs.reshape(-1), jj.reshape(-1), ee.reshape(-1))

    blk_e = jnp.repeat(bexp, MOE_PER_EBLK)
    q0 = jnp.arange(n_blocks) * blk - seg_off[blk_e]
    q1 = jnp.minimum(cnt[blk_e], q0 + blk) - 1
    ends = tile_end.T[blk_e]
    has = q0 < cnt[blk_e]
    j_lo = jnp.where(has, jnp.sum(ends <= q0[:, None], axis=1), 0)
    j_hi = jnp.where(has, jnp.sum(ends <= q1[:, None], axis=1), -1)

    def edges(key, valid):
        prev = jnp.concatenate([jnp.full((1,), -1, jnp.int32), key[:-1]])
        nxt = jnp.concatenate([key[1:], jnp.full((1,), -1, jnp.int32)])
        nxt_valid = jnp.concatenate([valid[1:], jnp.zeros((1,), jnp.int32)])
        first = ((key != prev) & (valid == 1)).astype(jnp.int32)
        last = (((key != nxt) | (nxt_valid == 0)) & (valid == 1)).astype(jnp.int32)
        return first, last

    tj, td, te, tv = by_tile
    t_first, t_last = edges(tj, tv)
    return dict(pos=pos, pos_t=pos.T, gate=g[:, :N_EXPERTS], bexp=bexp.astype(jnp.int32),
                n_used=n_used.astype(jnp.int32).reshape(1), n_blocks=n_blocks, n_eblocks=n_eblocks,
                max_pairs=max_pairs,
                disp=(blk_e.astype(jnp.int32), j_lo.astype(jnp.int32), j_hi.astype(jnp.int32)),
                comb=(tj, td, te, t_first, t_last, tv))


def _dispatch_kernel(be, jlo, jhi, h_ref, post_ref, xs_ref, acc_scr, *, ts):
    d = pl.program_id(0)
    e = be[d]
    row = d * MOE_BLK + lax.broadcasted_iota(jnp.int32, (MOE_BLK, 1), 0)
    acc_scr[...] = jnp.zeros_like(acc_scr)

    def tile(j, _):
        t0 = pl.multiple_of(j * ts, ts)
        hit = post_ref[pl.ds(e, 1), pl.ds(t0, ts)] == row
        acc_scr[...] += jnp.dot(jnp.where(hit, 1.0, 0.0).astype(BF16), h_ref[pl.ds(t0, ts), :],
                                preferred_element_type=F32)
        return 0
    lax.fori_loop(jlo[d], jhi[d] + 1, tile, 0)
    xs_ref[...] = acc_scr[...].astype(xs_ref.dtype)


def _dispatch(h, plan, ts):
    n = h.shape[0]
    rows = plan['n_blocks'] * MOE_BLK
    grid_spec = pltpu.PrefetchScalarGridSpec(
        num_scalar_prefetch=3,
        grid=(plan['n_blocks'],),
        in_specs=[pl.BlockSpec((n, D_MODEL), lambda d, be, lo, hi: (0, 0), pipeline_mode=pl.Buffered(1)),
                  pl.BlockSpec((N_EXPERTS, n), lambda d, be, lo, hi: (0, 0), pipeline_mode=pl.Buffered(1))],
        out_specs=pl.BlockSpec((MOE_BLK, D_MODEL), lambda d, be, lo, hi: (d, 0)),
        scratch_shapes=[pltpu.VMEM((MOE_BLK, D_MODEL), F32)],
    )
    return pl.pallas_call(
        functools.partial(_dispatch_kernel, ts=ts),
        grid_spec=grid_spec,
        out_shape=jax.ShapeDtypeStruct((rows, D_MODEL), BF16),
        compiler_params=_cparams(("arbitrary",)),
        name="moe_dispatch",
    )(*plan['disp'], h, plan['pos_t'])


def _experts_kernel(bexp, nused, x_ref, w1_ref, w3_ref, w2_ref, y_ref, acc_scr, *, nf):
    d = pl.program_id(0)
    f = pl.program_id(1)

    @pl.when(d < nused[0])
    def _():
        @pl.when(f == 0)
        def _():
            acc_scr[...] = jnp.zeros_like(acc_scr)

        x = x_ref[...]
        a = jnp.dot(x, w1_ref[...].astype(BF16), preferred_element_type=F32)
        g = jnp.dot(x, w3_ref[...].astype(BF16), preferred_element_type=F32)
        acc_scr[...] += jnp.dot((_silu(a) * g).astype(BF16), w2_ref[...].astype(BF16),
                                preferred_element_type=F32)

        @pl.when(f == nf - 1)
        def _():
            y_ref[...] = acc_scr[...].astype(y_ref.dtype)


def _experts(xs, plan, w1, w3, w2):
    dff = w1.shape[2]
    nf = 7
    tf = dff // nf
    assert tf * nf == dff and tf % LANE == 0
    eblk = MOE_PER_EBLK * MOE_BLK

    def blk(d, nused):
        return jnp.minimum(d, nused[0] - 1)

    def fidx(d, f, nused):
        return jnp.where(d < nused[0], f, nf - 1)

    grid_spec = pltpu.PrefetchScalarGridSpec(
        num_scalar_prefetch=2,
        grid=(plan['n_eblocks'], nf),
        in_specs=[pl.BlockSpec((eblk, D_MODEL), lambda d, f, be, nu: (blk(d, nu), 0)),
                  pl.BlockSpec((None, D_MODEL, tf), lambda d, f, be, nu: (be[blk(d, nu)], 0, fidx(d, f, nu))),
                  pl.BlockSpec((None, D_MODEL, tf), lambda d, f, be, nu: (be[blk(d, nu)], 0, fidx(d, f, nu))),
                  pl.BlockSpec((None, tf, D_MODEL), lambda d, f, be, nu: (be[blk(d, nu)], fidx(d, f, nu), 0))],
        out_specs=pl.BlockSpec((eblk, D_MODEL), lambda d, f, be, nu: (blk(d, nu), 0)),
        scratch_shapes=[pltpu.VMEM((eblk, D_MODEL), F32)],
    )
    return pl.pallas_call(
        functools.partial(_experts_kernel, nf=nf),
        grid_spec=grid_spec,
        out_shape=jax.ShapeDtypeStruct(xs.shape, BF16),
        compiler_params=_cparams(("arbitrary", "arbitrary")),
        name="moe_experts",
    )(plan['bexp'], plan['n_used'], xs, w1, w3, w2)


def _combine_kernel(pj, pd, pe, pfirst, plast, pvalid, ys_ref, pos_ref, gate_ref, x_ref, mod_ref, fnw_ref,
                    xo_ref, acc_scr, *, ts, final_norm):
    k = pl.program_id(0)

    @pl.when(pfirst[k] == 1)
    def _():
        acc_scr[...] = jnp.zeros_like(acc_scr)

    @pl.when(pvalid[k] == 1)
    def _():
        e = pe[k]
        lane = lax.broadcasted_iota(jnp.int32, (ts, N_EXPERTS), 1)
        pos_e = jnp.sum(jnp.where(lane == e, pos_ref[...], 0), axis=1, keepdims=True)
        gate_e = jnp.sum(jnp.where(lane == e, gate_ref[...], 0.0), axis=1, keepdims=True)
        row = pd[k] * MOE_BLK + lax.broadcasted_iota(jnp.int32, (1, MOE_BLK), 1)
        hit = pos_e == row
        got = jnp.dot(jnp.where(hit, 1.0, 0.0).astype(BF16), ys_ref[...], preferred_element_type=F32)
        acc_scr[...] += gate_e * got

    @pl.when(plast[k] == 1)
    def _():
        xo = x_ref[...] + mod_ref[5:6, :] * acc_scr[...]
        if final_norm:
            xo = xo * lax.rsqrt(jnp.mean(xo * xo, axis=-1, keepdims=True) + EPS) * fnw_ref[...]
        xo_ref[...] = xo


def _combine(ys, plan, x, mod, fnw, ts, final_norm):
    n = x.shape[0]
    tiles_per_batch = n // (mod.shape[0] * ts)
    sp = lambda fn: (lambda k, pj, pd, pe, pf, pl_, pv: fn(k, pj, pd))
    grid_spec = pltpu.PrefetchScalarGridSpec(
        num_scalar_prefetch=6,
        grid=(plan['max_pairs'],),
        in_specs=[pl.BlockSpec((MOE_BLK, D_MODEL), sp(lambda k, pj, pd: (pd[k], 0))),
                  pl.BlockSpec((ts, N_EXPERTS), sp(lambda k, pj, pd: (pj[k], 0))),
                  pl.BlockSpec((ts, N_EXPERTS), sp(lambda k, pj, pd: (pj[k], 0))),
                  pl.BlockSpec((ts, D_MODEL), sp(lambda k, pj, pd: (pj[k], 0))),
                  pl.BlockSpec((None, 8, D_MODEL), sp(lambda k, pj, pd: (pj[k] // tiles_per_batch, 0, 0))),
                  pl.BlockSpec((1, D_MODEL), sp(lambda k, pj, pd: (0, 0)))],
        out_specs=pl.BlockSpec((ts, D_MODEL), sp(lambda k, pj, pd: (pj[k], 0))),
        scratch_shapes=[pltpu.VMEM((ts, D_MODEL), F32)],
    )
    return pl.pallas_call(
        functools.partial(_combine_kernel, ts=ts, final_norm=final_norm),
        grid_spec=grid_spec,
        out_shape=jax.ShapeDtypeStruct((n, D_MODEL), F32),
        compiler_params=_cparams(("arbitrary",)),
        name="moe_combine",
    )(*plan['comb'], ys, plan['pos'], plan['gate'], x, mod, fnw)


def _moe(x, mod, nw, wr, br, w1, w3, w2, fnw, final_norm):
    b, t, _ = x.shape
    ts = _tile(t, 1024)
    h, g = _router(x, mod, nw, wr, br)
    plan = _moe_plan(g.reshape(b * t, LANE), ts)
    xs = _dispatch(h.reshape(b * t, D_MODEL), plan, ts)
    ys = _experts(xs, plan, w1, w3, w2)
    out = _combine(ys, plan, x.reshape(b * t, D_MODEL), mod, fnw, ts, final_norm)
    return out.reshape(b, t, D_MODEL)


def _pad_lanes(v, fill=0.0):
    return jnp.pad(v.astype(F32), (0, LANE - v.shape[0]), constant_values=fill)[None, :]


def kernel(x, c, ctx, c_ctx, w_mod, b_mod, norm_w, w_in, s5_lam_re, s5_lam_im, s5_log_dt, s5_b_re, s5_b_im, s5_c_re, s5_c_im, s5_d, s5_w_glu, lru_conv_w, lru_conv_b, lru_w_a, lru_b_a, lru_w_x, lru_b_x, lru_lam, ssd_conv_w, ssd_conv_b, ssd_a_log, ssd_dt_bias, ssd_d, ssd_norm_w, w_br_a, w_br_b, w_br_c, w_out, ffn_w1, ffn_w3, ffn_w2, moe_w_router, moe_b_router, moe_w1, moe_w3, moe_w2, final_norm_w):
    depth = w_mod.shape[0]
    nb, t_lat, _ = x.shape
    t_ctx = ctx.shape[1]
    assert t_lat % GRID_W == 0 and t_lat % SSD_CHUNK == 0 and t_ctx % SSD_CHUNK == 0

    cond_in = jnp.zeros((8, D_MODEL), F32).at[:nb].set(c).at[nb].set(c_ctx)
    mods = _modulation(cond_in, w_mod, b_mod).reshape(depth, 8, N_MOD, D_MODEL)
    mods = jnp.pad(mods, ((0, 0), (0, 0), (0, 8 - N_MOD), (0, 0)))

    head_of = jnp.arange(SSD_INNER) // SSD_HEAD_DIM
    e_mat = (jnp.arange(LANE)[:, None] == head_of[None, :]).astype(BF16)
    s5_perm = _s5_perm()
    n_lev_lat = max(1, math.ceil(math.log2(t_lat // S5_CHUNK)))
    s5_prep = jax.vmap(functools.partial(_s5_setup, n_levels=n_lev_lat))(
        s5_lam_re, s5_lam_im, s5_log_dt, s5_b_re, s5_b_im, s5_c_re, s5_c_im)
    lru_prep = jax.vmap(_lru_gate_weights)(lru_w_a, lru_b_a, lru_w_x, lru_b_x)

    x_lat, x_ctx = x, ctx
    for l in range(depth):
        ctx_out = l < depth - 1
        last = l == depth - 1
        mod_lat = mods[l, :nb]
        mod_ctx = jnp.broadcast_to(mods[l, nb][None], (nb, 8, D_MODEL))
        nw0 = norm_w[l, 0][None, :]
        nw1 = norm_w[l, 1][None, :]

        wl = w_in[l]
        o_lx = S5_WIDTH
        o_lg = o_lx + LRU_WIDTH
        o_z = o_lg + LRU_WIDTH
        o_xbc = o_z + SSD_INNER
        o_dt = o_xbc + SSD_XBC
        o_g = o_dt + SSD_HEADS
        w_main = jnp.concatenate(
            [wl[:, :o_lx], jnp.zeros((D_MODEL, P_LX - S5_WIDTH), wl.dtype), wl[:, o_lx:o_dt], wl[:, o_g:]],
            axis=1).astype(BF16)
        w_dt = jnp.pad(wl[:, o_dt:o_g], ((0, 0), (0, LANE - SSD_HEADS))).astype(BF16)
        kcat, pm, rm, atab = s5_prep
        wg, bg = lru_prep
        lcw = lru_conv_w[l].astype(F32)
        lcb = lru_conv_b[l][None, :].astype(F32)
        scw = ssd_conv_w[l].astype(F32)
        scb = ssd_conv_b[l][None, :].astype(F32)
        a_log = jnp.concatenate([_pad_lanes(ssd_a_log[l, 0]), _pad_lanes(ssd_a_log[l, 1])], axis=0)
        dt_bias = jnp.concatenate([_pad_lanes(ssd_dt_bias[l, 0]), _pad_lanes(ssd_dt_bias[l, 1])], axis=0)
        dskip_x = jnp.repeat(ssd_d[l].astype(F32), SSD_HEAD_DIM)[None, :]

        def mixers(p, dt, t, w, s5_h0, lru_h0, ssd_h0f, ssd_h0b):
            y5, s5_ht = _s5_scan(_s5_chunks(p, nb, t), s5_perm, kcat, pm, rm, atab, s5_h0, l)
            ya = _s5_unchunk(y5, nb, t)
            yb, lru_ht = _lru_scan(p, lcw, lcb, wg, bg, lru_lam[l].astype(F32), lru_h0, w,
                                   1 if w > 1 else LRU_WIDTH // LRU_SLAB, l)
            yc, ssd_htf, ssd_htb = _ssd_scan(p, dt, scw, scb, a_log, dt_bias, dskip_x, e_mat, ssd_h0f, ssd_h0b)
            return ya, yb, yc, s5_ht, lru_ht, ssd_htf, ssd_htb

        p_ctx, dt_ctx = _inproj(x_ctx, mod_ctx, nw0, w_main, w_dt)
        zs5 = jnp.zeros((S5_GROUPS, nb, 4 * S5_STATE), F32)
        zlru = jnp.zeros((nb, 2, LRU_WIDTH), F32)
        zssd = jnp.zeros((nb, SSD_GROUPS * SSD_STATE, SSD_GW), F32)
        ya_c, yb_c, yc_c, s5_h, lru_h, ssd_hf, ssd_hb = mixers(p_ctx, dt_ctx, t_ctx, 1, zs5, zlru, zssd, zssd)

        p_lat, dt_lat = _inproj(x_lat, mod_lat, nw0, w_main, w_dt)
        ya, yb, yc, _, _, _, _ = mixers(p_lat, dt_lat, t_lat, GRID_W, s5_h, lru_h, ssd_hf, ssd_hb)

        mw = (s5_d[l][None, :].astype(F32), s5_w_glu[l].astype(BF16), ssd_norm_w[l][None, :].astype(F32),
              w_br_a[l].astype(BF16), w_br_b[l].astype(BF16), w_br_c[l].astype(BF16), w_out[l].astype(BF16))
        x_lat = _merge(x_lat, mod_lat, p_lat, ya, yb, yc, *mw)
        if ctx_out:
            x_ctx = _merge(x_ctx, mod_ctx, p_ctx, ya_c, yb_c, yc_c, *mw)

        fnw = final_norm_w[None, :].astype(F32)
        if l % 2 == 0:
            fw = (ffn_w1[l // 2].astype(BF16), ffn_w3[l // 2].astype(BF16), ffn_w2[l // 2].astype(BF16))
            x_lat = _ffn(x_lat, mod_lat, nw1, *fw, fnw, last)
            if ctx_out:
                x_ctx = _ffn(x_ctx, mod_ctx, nw1, *fw, fnw, False)
        else:
            wr = jnp.pad(moe_w_router[l // 2].astype(F32), ((0, 0), (0, LANE - N_EXPERTS)))
            br = _pad_lanes(moe_b_router[l // 2])
            ew = (moe_w1[l // 2], moe_w3[l // 2], moe_w2[l // 2])
            x_lat = _moe(x_lat, mod_lat, nw1, wr, br, *ew, fnw, last)
            if ctx_out:
                x_ctx = _moe(x_ctx, mod_ctx, nw1, wr, br, *ew, fnw, False)
    return x_lat
```
